```python
import math
import jax
import jax.numpy as jnp
from jax import lax
import numpy as np

D_MODEL = 1024
BATCH = 8
SEQ = 4096
DEPTH = 4

CTX_LEN = 256
GRID_W = 64
NORM_EPS = 1e-6

HYENA_WIDTH = D_MODEL // 2
HYENA_ORDER = 2
HYENA_PROJ = (HYENA_ORDER + 1) * HYENA_WIDTH
SHORT_CONV = 3
FILTER_EMB = 33
FILTER_HIDDEN = 64
FILTER_FAST_DECAY = 0.3
FILTER_SLOW_DECAY = 1.5
FILTER_TARGET = 1e-2

DIFF_HEADS = 4
DIFF_HEAD_DIM = 64
DIFF_V_DIM = 2 * DIFF_HEAD_DIM
DIFF_QK_WIDTH = DIFF_HEADS * 2 * DIFF_HEAD_DIM
DIFF_WIDTH = DIFF_HEADS * DIFF_V_DIM
ROPE_BASE = 10000.0
ATTN_BLOCK = 128

IN_WIDTH = HYENA_PROJ + 2 * DIFF_QK_WIDTH + DIFF_WIDTH
MIX_WIDTH = HYENA_WIDTH + DIFF_WIDTH

POOL_WINDOWS = (2, 4, 8, 16)
POOL_GROUP = D_MODEL // len(POOL_WINDOWS)

N_EXPERTS = 16
EXPERT_FF = 2 * D_MODEL
EC_CAPACITY = 2

kernel_name = 'hybrid_hyena_diffattn_pool_ecmoe_dit'


def rms_norm(x, g):
    xf = x.astype(jnp.float32)
    y = xf * lax.rsqrt(jnp.mean(xf * xf, axis=-1, keepdims=True) + NORM_EPS)
    return (y * g.astype(jnp.float32)).astype(x.dtype)


def rope_1d(x, pos):
    nf = x.shape[-1] // 2
    inv = ROPE_BASE ** (-jnp.arange(nf, dtype=jnp.float32) / nf)
    ang = pos.astype(jnp.float32)[:, None] * inv[None, :]
    shape = (1, ang.shape[0]) + (1,) * (x.ndim - 3) + (nf,)
    cos = jnp.cos(ang).reshape(shape).astype(x.dtype)
    sin = jnp.sin(ang).reshape(shape).astype(x.dtype)
    x1, x2 = x[..., :nf], x[..., nf:]
    return jnp.concatenate([x1 * cos - x2 * sin, x1 * sin + x2 * cos], axis=-1)


def axial_rope(x, row, col):
    half = x.shape[-1] // 2
    return jnp.concatenate([rope_1d(x[..., :half], row), rope_1d(x[..., half:], col)], axis=-1)


def short_conv_centred(u, w, b):
    L = u.shape[1]
    r = SHORT_CONV // 2
    up = jnp.pad(u, ((0, 0), (r, r), (0, 0)))
    return sum(up[:, j:j + L] * w[j] for j in range(SHORT_CONV)) + b


def hyena_filters(L, w1, b1, w2, b2, w3, b3, freq, wout):
    f32 = jnp.float32
    t = jnp.linspace(0.0, 1.0, L, dtype=f32)[:, None]
    bands = (FILTER_EMB - 1) // 2
    w = 2.0 * math.pi * jnp.arange(L, dtype=f32)[:, None] / L
    f = jnp.linspace(1e-4, bands - 1, bands, dtype=f32)[None, :]
    z = jnp.concatenate([t, jnp.cos(f * w), -jnp.sin(f * w)], axis=-1)
    fr = freq.astype(f32)
    h = jnp.sin(fr * (z @ w1.astype(f32) + b1.astype(f32)))
    h = jnp.sin(fr * (h @ w2.astype(f32) + b2.astype(f32)))
    h = jnp.sin(fr * (h @ w3.astype(f32) + b3.astype(f32)))
    h = h @ wout.astype(f32)
    max_decay = math.log(FILTER_TARGET) / FILTER_FAST_DECAY
    min_decay = math.log(FILTER_TARGET) / FILTER_SLOW_DECAY
    deltas = jnp.abs(jnp.linspace(min_decay, max_decay, HYENA_WIDTH, dtype=f32))
    decay = jnp.exp(-t * deltas[None, :])
    return h.reshape(L, HYENA_ORDER, 2, HYENA_WIDTH) * decay[:, None, None, :]


def long_conv_bidir(u, h_fwd, h_bwd, bias):
    L = u.shape[1]
    n = 2 * L
    uf = u.astype(jnp.float32)
    hf = jnp.fft.rfft(h_fwd, n=n, axis=0)
    hb = jnp.fft.rfft(h_bwd, n=n, axis=0)
    y_f = jnp.fft.irfft(jnp.fft.rfft(uf, n=n, axis=1) * hf, n=n, axis=1)[:, :L]
    y_b = jnp.fft.irfft(jnp.fft.rfft(uf[:, ::-1], n=n, axis=1) * hb, n=n, axis=1)[:, :L][:, ::-1]
    return (y_f + y_b + uf * bias.astype(jnp.float32)).astype(u.dtype)


def hyena_operator(p, conv_w, conv_b, filt, bias):
    uc = short_conv_centred(p, conv_w, conv_b)
    *gates, z = jnp.split(uc, HYENA_ORDER + 1, axis=-1)
    for o, gate in enumerate(gates):
        z = gate * long_conv_bidir(z, filt[:, o, 0], filt[:, o, 1], bias[o])
    return z


def qk_heads(p):
    return p.reshape(p.shape[:2] + (DIFF_HEADS, 2, DIFF_HEAD_DIM))


def v_heads(p):
    return p.reshape(p.shape[:2] + (DIFF_HEADS, DIFF_V_DIM))


def diff_attend(q, k, v, lam):
    s = jnp.einsum('bqhmd,bkhmd->bhmqk', q, k).astype(jnp.float32) * (DIFF_HEAD_DIM ** -0.5)
    p = jax.nn.softmax(s, axis=-1)
    a = p[:, :, 0] - lam * p[:, :, 1]
    return jnp.einsum('bhqk,bkhe->bqhe', a.astype(v.dtype), v)


def latent_diff_attention(q, k_lat, v_lat, k_ctx, v_ctx, lam):
    B, S = q.shape[:2]
    k_all = jnp.concatenate([k_ctx, k_lat], axis=1)
    v_all = jnp.concatenate([v_ctx, v_lat], axis=1)
    nblk = S // ATTN_BLOCK
    qb = jnp.moveaxis(q.reshape((B, nblk, ATTN_BLOCK) + q.shape[2:]), 1, 0)
    o = lax.map(lambda qi: diff_attend(qi, k_all, v_all, lam), qb)
    return jnp.moveaxis(o, 0, 1).reshape((B, S) + o.shape[3:])


def diff_out(o, subln_g, lam_init):
    return (rms_norm(o, subln_g) * (1.0 - lam_init)).reshape(o.shape[:2] + (DIFF_WIDTH,))


def hybrid_mixer(a, ac, ctx_full, lam_init, row, col, w_in, w_out, conv_w, conv_b,
                 filt_params, hy_bias, q_g, k_g, lam_vec, subln_g):
    lv = lam_vec.astype(jnp.float32)
    lam = jnp.exp(jnp.sum(lv[0] * lv[1])) - jnp.exp(jnp.sum(lv[2] * lv[3])) + lam_init
    S = a.shape[1]
    q_off = HYENA_PROJ
    kv_off = HYENA_PROJ + DIFF_QK_WIDTH
    v_off = kv_off + DIFF_QK_WIDTH
    p = a @ w_in
    hy = hyena_operator(p[..., :HYENA_PROJ], conv_w, conv_b, hyena_filters(S, *filt_params), hy_bias)
    q = axial_rope(rms_norm(qk_heads(p[..., q_off:kv_off]), q_g), row, col)
    k = axial_rope(rms_norm(qk_heads(p[..., kv_off:v_off]), k_g), row, col)
    v = v_heads(p[..., v_off:])
    pc = ac @ (w_in if ctx_full else w_in[:, kv_off:])
    pc_kv = pc[..., kv_off:] if ctx_full else pc
    kc = rms_norm(qk_heads(pc_kv[..., :DIFF_QK_WIDTH]), k_g)
    vc = v_heads(pc_kv[..., DIFF_QK_WIDTH:])
    o = latent_diff_attention(q, k, v, kc, vc, lam)
    y = jnp.concatenate([hy, diff_out(o, subln_g, lam_init)], axis=-1) @ w_out
    if not ctx_full:
        return y, None
    Lc = ac.shape[1]
    hyc = hyena_operator(pc[..., :HYENA_PROJ], conv_w, conv_b, hyena_filters(Lc, *filt_params), hy_bias)
    qc = rms_norm(qk_heads(pc[..., q_off:kv_off]), q_g)
    oc = diff_attend(qc, kc, vc, lam)
    yc = jnp.concatenate([hyc, diff_out(oc, subln_g, lam_init)], axis=-1) @ w_out
    return y, yc


def multiscale_pool(h, pool_w, pool_scale):
    B, L, D = h.shape
    hf = h.astype(jnp.float32)
    cs = jnp.concatenate([jnp.zeros((B, 1, D), jnp.float32), jnp.cumsum(hf, axis=1)], axis=1)
    t = jnp.arange(L)
    groups = []
    for gi, w in enumerate(POOL_WINDOWS):
        lo = jnp.clip(t - w // 2, 0, L)
        hi = jnp.clip(t + w // 2, 0, L)
        sl = slice(gi * POOL_GROUP, (gi + 1) * POOL_GROUP)
        cg = cs[..., sl]
        cnt = (hi - lo).astype(jnp.float32)[None, :, None]
        mean = (jnp.take(cg, hi, axis=1) - jnp.take(cg, lo, axis=1)) / cnt
        groups.append(mean - hf[..., sl])
    p = jnp.stack(groups, axis=2).astype(h.dtype)
    y = jnp.einsum('blgc,gcd->blgd', p, pool_w).reshape(B, L, D)
    return y * pool_scale


def expert_choice_ffn(h, router_w, w_gate, w_up, w_down):
    B, n, D = h.shape
    cap = EC_CAPACITY * n // N_EXPERTS
    aff = jax.nn.softmax((h @ router_w).astype(jnp.float32), axis=-1)
    g, idx = lax.top_k(jnp.swapaxes(aff, 1, 2), cap)
    xs = jax.vmap(lambda hb, ib: hb[ib])(h, idx)
    a = jnp.einsum('becd,edf->becf', xs, w_gate)
    u = jnp.einsum('becd,edf->becf', xs, w_up)
    y = jnp.einsum('becf,efd->becd', jax.nn.silu(a) * u, w_down) * g[..., None].astype(h.dtype)
    return jax.vmap(lambda yb, ib: jnp.zeros((n, D), yb.dtype).at[ib.reshape(-1)].add(yb.reshape(-1, D)))(y, idx)


def setup_inputs(seed: int = 0) -> dict:
    key = jax.random.key(seed)
    ks = iter(jax.random.split(key, 32))

    def nrm(shape, scale):
        return jax.random.normal(next(ks), shape, jnp.float32) * scale

    D = D_MODEL
    n_even = (DEPTH + 1) // 2
    n_odd = DEPTH // 2
    return {
        'x': nrm((BATCH, SEQ, D), 1.0),
        'c': nrm((BATCH, D), 1.0),
        'ctx': nrm((BATCH, CTX_LEN, D), 1.0),
        'c_ctx': nrm((D,), 1.0),
        'ada_w': nrm((DEPTH, D, 6 * D), 0.5 * D ** -0.5),
        'ada_b': nrm((DEPTH, 6 * D), 0.02),
        'norm_mix_g': 1.0 + nrm((DEPTH, D), 0.02),
        'norm_ffn_g': 1.0 + nrm((DEPTH, D), 0.02),
        'w_in': nrm((n_even, D, IN_WIDTH), D ** -0.5),
        'w_out': nrm((n_even, MIX_WIDTH, D), MIX_WIDTH ** -0.5),
        'hy_conv_w': nrm((n_even, SHORT_CONV, HYENA_PROJ), SHORT_CONV ** -0.5),
        'hy_conv_b': nrm((n_even, HYENA_PROJ), 0.02),
        'hy_f_w1': nrm((n_even, FILTER_EMB, FILTER_HIDDEN), FILTER_EMB ** -0.5),
        'hy_f_b1': nrm((n_even, FILTER_HIDDEN), 0.02),
        'hy_f_w2': nrm((n_even, FILTER_HIDDEN, FILTER_HIDDEN), FILTER_HIDDEN ** -0.5),
        'hy_f_b2': nrm((n_even, FILTER_HIDDEN), 0.02),
        'hy_f_w3': nrm((n_even, FILTER_HIDDEN, FILTER_HIDDEN), FILTER_HIDDEN ** -0.5),
        'hy_f_b3': nrm((n_even, FILTER_HIDDEN), 0.02),
        'hy_f_freq': 1.0 + nrm((n_even, FILTER_HIDDEN), 0.02),
        'hy_f_wout': nrm((n_even, FILTER_HIDDEN, HYENA_ORDER * 2 * HYENA_WIDTH), 0.05 * FILTER_HIDDEN ** -0.5),
        'hy_bias': nrm((n_even, HYENA_ORDER, HYENA_WIDTH), 0.5),
        'q_norm_g': 1.0 + nrm((n_even, DIFF_HEAD_DIM), 0.02),
        'k_norm_g': 1.0 + nrm((n_even, DIFF_HEAD_DIM), 0.02),
        'diff_lambda': nrm((n_even, 4, DIFF_HEAD_DIM), 0.1),
        'subln_g': 1.0 + nrm((n_even, DIFF_V_DIM), 0.02),
        'pool_w': nrm((n_odd, len(POOL_WINDOWS), POOL_GROUP, POOL_GROUP), POOL_GROUP ** -0.5),
        'pool_scale': 1.0 + nrm((n_odd, D), 0.1),
        'router_w': nrm((DEPTH, D, N_EXPERTS), D ** -0.5),
        'exp_w_gate': nrm((DEPTH, N_EXPERTS, D, EXPERT_FF), D ** -0.5),
        'exp_w_up': nrm((DEPTH, N_EXPERTS, D, EXPERT_FF), D ** -0.5),
        'exp_w_down': nrm((DEPTH, N_EXPERTS, EXPERT_FF, D), EXPERT_FF ** -0.5),
    }


def reference(x, c, ctx, c_ctx, ada_w, ada_b, norm_mix_g, norm_ffn_g, w_in, w_out,
              hy_conv_w, hy_conv_b, hy_f_w1, hy_f_b1, hy_f_w2, hy_f_b2, hy_f_w3, hy_f_b3,
              hy_f_freq, hy_f_wout, hy_bias, q_norm_g, k_norm_g, diff_lambda, subln_g,
              pool_w, pool_scale, router_w, exp_w_gate, exp_w_up, exp_w_down):
    B, S, D = x.shape
    rows = S // GRID_W
    row = jnp.repeat(jnp.arange(rows, dtype=jnp.int32), GRID_W)
    col = jnp.tile(jnp.arange(GRID_W, dtype=jnp.int32), rows)
    last_attn = ((DEPTH - 1) // 2) * 2
    s_lat = jax.nn.silu(c)
    s_ctx = jax.nn.silu(c_ctx)
    h, hc = x, ctx
    for l in range(DEPTH):
        m = jnp.split((s_lat @ ada_w[l] + ada_b[l])[:, None, :], 6, axis=-1)
        mc = jnp.split(s_ctx @ ada_w[l] + ada_b[l], 6, axis=-1)
        ctx_full = l < last_attn
        a = rms_norm(h, norm_mix_g[l]) * (1 + m[1]) + m[0]
        ac = rms_norm(hc, norm_mix_g[l]) * (1 + mc[1]) + mc[0] if l <= last_attn else None
        if l % 2 == 0:
            e = l // 2
            lam_init = 0.8 - 0.6 * math.exp(-0.3 * l)
            filt_params = (hy_f_w1[e], hy_f_b1[e], hy_f_w2[e], hy_f_b2[e], hy_f_w3[e], hy_f_b3[e],
                           hy_f_freq[e], hy_f_wout[e])
            y, yc = hybrid_mixer(a, ac, ctx_full, lam_init, row, col, w_in[e], w_out[e],
                                 hy_conv_w[e], hy_conv_b[e], filt_params, hy_bias[e],
                                 q_norm_g[e], k_norm_g[e], diff_lambda[e], subln_g[e])
        else:
            o = l // 2
            y = multiscale_pool(a, pool_w[o], pool_scale[o])
            yc = multiscale_pool(ac, pool_w[o], pool_scale[o]) if ctx_full else None
        h = h + m[2] * y
        f_in = rms_norm(h, norm_ffn_g[l]) * (1 + m[4]) + m[3]
        h = h + m[5] * expert_choice_ffn(f_in, router_w[l], exp_w_gate[l], exp_w_up[l], exp_w_down[l])
        if ctx_full:
            hc = hc + mc[2] * yc
            fc_in = rms_norm(hc, norm_ffn_g[l]) * (1 + mc[4]) + mc[3]
            hc = hc + mc[5] * expert_choice_ffn(fc_in, router_w[l], exp_w_gate[l], exp_w_up[l], exp_w_down[l])
    return h
```

```python
import functools
import math

import jax
import jax.numpy as jnp
from jax import lax
from jax.experimental import pallas as pl
from jax.experimental.pallas import tpu as pltpu

F32 = jnp.float32
BF16 = jnp.bfloat16
I32 = jnp.int32

NORM_EPS = 1e-6
GRID_W = 64
HYENA_ORDER = 2
FILTER_EMB = 33
FILTER_FAST_DECAY = 0.3
FILTER_SLOW_DECAY = 1.5
FILTER_TARGET = 1e-2
DIFF_HEADS = 4
DIFF_HEAD_DIM = 64
ROPE_BASE = 10000.0
POOL_WINDOWS = (2, 4, 8, 16)
EC_CAPACITY = 2
LANES = 128
SUBLANES = 8
POOL_HALO = 8


def _params(sem, vmem_mb):
    return pltpu.CompilerParams(dimension_semantics=sem, vmem_limit_bytes=vmem_mb * 1024 * 1024)


def _dot(a, b):
    return jnp.dot(a, b, preferred_element_type=F32)


def _norm_mod(x, g, scale, shift):
    ms = jnp.mean(x * x, axis=-1, keepdims=True)
    return (x * lax.rsqrt(ms + NORM_EPS) * g) * (1.0 + scale) + shift


def _mm_kernel(a_ref, b_ref, o_ref, acc_ref, *, nk):
    k = pl.program_id(2)
    part = _dot(a_ref[...].astype(BF16), b_ref[...].astype(BF16))

    @pl.when(k == 0)
    def _():
        acc_ref[...] = part

    @pl.when(k > 0)
    def _():
        acc_ref[...] += part

    @pl.when(k == nk - 1)
    def _():
        o_ref[...] = acc_ref[...].astype(o_ref.dtype)


def matmul(a, b, tm, tn, tk):
    M, K = a.shape
    N = b.shape[1]
    nk = K // tk
    return pl.pallas_call(
        functools.partial(_mm_kernel, nk=nk),
        grid=(M // tm, N // tn, nk),
        in_specs=[pl.BlockSpec((tm, tk), lambda i, j, k: (i, k)),
                  pl.BlockSpec((tk, tn), lambda i, j, k: (k, j))],
        out_specs=pl.BlockSpec((tm, tn), lambda i, j, k: (i, j)),
        out_shape=jax.ShapeDtypeStruct((M, N), F32),
        scratch_shapes=[pltpu.VMEM((tm, tn), F32)],
        compiler_params=_params(("parallel", "parallel", "arbitrary"), 40),
        name="matmul",
    )(a, b)


def _ada_kernel(s_ref, w_ref, b_ref, o_ref):
    o_ref[...] = _dot(s_ref[...].astype(BF16), w_ref[...].astype(BF16)) + b_ref[...]


def ada_modulation(s, ada_w, ada_b):
    depth, D, N = ada_w.shape
    R = s.shape[0]
    tn = 1024
    return pl.pallas_call(
        _ada_kernel,
        grid=(depth, N // tn),
        in_specs=[pl.BlockSpec((R, D), lambda l, j: (0, 0)),
                  pl.BlockSpec((None, D, tn), lambda l, j: (l, 0, j)),
                  pl.BlockSpec((None, 1, tn), lambda l, j: (l, 0, j))],
        out_specs=pl.BlockSpec((None, R, tn), lambda l, j: (l, 0, j)),
        out_shape=jax.ShapeDtypeStruct((depth, R, N), F32),
        compiler_params=_params(("parallel", "parallel"), 32),
        name="ada_modulation",
    )(s, ada_w, ada_b.reshape(depth, 1, N))


def _nmm_kernel(h_ref, g_ref, sc_ref, sh_ref, w_ref, o_ref):
    a = _norm_mod(h_ref[0], g_ref[...], sc_ref[0], sh_ref[0])
    o_ref[0] = _dot(a.astype(BF16), w_ref[...])


def norm_mod_matmul(h, g, scale, shift, w, tm):
    B, S, D = h.shape
    N = w.shape[1]
    return pl.pallas_call(
        _nmm_kernel,
        grid=(B, S // tm),
        in_specs=[pl.BlockSpec((1, tm, D), lambda b, i: (b, i, 0)),
                  pl.BlockSpec((1, D), lambda b, i: (0, 0)),
                  pl.BlockSpec((1, 1, D), lambda b, i: (b, 0, 0)),
                  pl.BlockSpec((1, 1, D), lambda b, i: (b, 0, 0)),
                  pl.BlockSpec((D, N), lambda b, i: (0, 0))],
        out_specs=pl.BlockSpec((1, tm, N), lambda b, i: (b, i, 0)),
        out_shape=jax.ShapeDtypeStruct((B, S, N), F32),
        compiler_params=_params(("parallel", "parallel"), 48),
        name="norm_mod_matmul",
    )(h, g.reshape(1, D), scale, shift, w)


def _sconv_kernel(p_ref, w_ref, b_ref, o_ref):
    x = p_ref[0]
    S = x.shape[0]
    row = lax.broadcasted_iota(I32, x.shape, 0)
    xm = jnp.where(row == 0, 0.0, pltpu.roll(x, 1, 0))
    xp = jnp.where(row == S - 1, 0.0, pltpu.roll(x, S - 1, 0))
    w = w_ref[...]
    o_ref[0] = xm * w[0:1] + x * w[1:2] + xp * w[2:3] + b_ref[...]


def short_conv(p, conv_w, conv_b, width, tc):
    B, S, _ = p.shape
    return pl.pallas_call(
        _sconv_kernel,
        grid=(B, width // tc),
        in_specs=[pl.BlockSpec((1, S, tc), lambda b, c: (b, 0, c)),
                  pl.BlockSpec((3, tc), lambda b, c: (0, c)),
                  pl.BlockSpec((1, tc), lambda b, c: (0, c))],
        out_specs=pl.BlockSpec((1, S, tc), lambda b, c: (b, 0, c)),
        out_shape=jax.ShapeDtypeStruct((B, S, width), F32),
        compiler_params=_params(("parallel", "parallel"), 48),
        name="short_conv",
    )(p, conv_w, conv_b.reshape(1, width))


def _dft_fwd_kernel(c_ref, s_ref, u_ref, kre_ref, kim_ref, y_ref, accc_ref, accs_ref, *, nb, nk):
    k = pl.program_id(2)

    @pl.when(k == 0)
    def _():
        accc_ref[...] = jnp.zeros_like(accc_ref)
        accs_ref[...] = jnp.zeros_like(accs_ref)

    c = c_ref[...]
    s = s_ref[...]
    for j in range(nb):
        u = u_ref[j].astype(BF16)
        accc_ref[j] += _dot(c, u)
        accs_ref[j] += _dot(s, u)

    @pl.when(k == nk - 1)
    def _():
        kre = kre_ref[...]
        kim = kim_ref[...]
        for j in range(nb):
            ure = accc_ref[j]
            uim = -accs_ref[j]
            y_ref[j, 0] = (kre * ure - kim * uim).astype(BF16)
            y_ref[j, 1] = (kre * uim + kim * ure).astype(BF16)


def dft_forward(cf, sf, u_arr, u_col, kre, kim, nb, tf, tk):
    B, L, _ = u_arr.shape
    C = kre.shape[1]
    nk = L // tk
    return pl.pallas_call(
        functools.partial(_dft_fwd_kernel, nb=nb, nk=nk),
        grid=(B // nb, L // tf, nk),
        in_specs=[pl.BlockSpec((tf, tk), lambda b, f, k: (f, k)),
                  pl.BlockSpec((tf, tk), lambda b, f, k: (f, k)),
                  pl.BlockSpec((nb, tk, C), lambda b, f, k: (b, k, u_col)),
                  pl.BlockSpec((tf, C), lambda b, f, k: (f, 0)),
                  pl.BlockSpec((tf, C), lambda b, f, k: (f, 0))],
        out_specs=pl.BlockSpec((nb, 2, tf, C), lambda b, f, k: (b, 0, f, 0)),
        out_shape=jax.ShapeDtypeStruct((B, 2, L, C), BF16),
        scratch_shapes=[pltpu.VMEM((nb, tf, C), F32), pltpu.VMEM((nb, tf, C), F32)],
        compiler_params=_params(("parallel", "parallel", "arbitrary"), 48),
        name="dft_forward",
    )(cf, sf, u_arr, kre, kim)


def _dft_inv_kernel(ct_ref, st_ref, y_ref, gate_ref, o_ref, acc_ref, *, nb, nk, scale):
    k = pl.program_id(2)

    @pl.when(k == 0)
    def _():
        acc_ref[...] = jnp.zeros_like(acc_ref)

    ct = ct_ref[...]
    st = st_ref[...]
    for j in range(nb):
        acc_ref[j] += _dot(ct, y_ref[j, 0]) - _dot(st, y_ref[j, 1])

    @pl.when(k == nk - 1)
    def _():
        for j in range(nb):
            o_ref[j] = gate_ref[j] * (acc_ref[j] * scale)


def dft_inverse(ct, st, y, gate_arr, gate_col, nb, tt, tk):
    B, _, L, C = y.shape
    nk = L // tk
    return pl.pallas_call(
        functools.partial(_dft_inv_kernel, nb=nb, nk=nk, scale=1.0 / L),
        grid=(B // nb, L // tt, nk),
        in_specs=[pl.BlockSpec((tt, tk), lambda b, t, k: (t, k)),
                  pl.BlockSpec((tt, tk), lambda b, t, k: (t, k)),
                  pl.BlockSpec((nb, 2, tk, C), lambda b, t, k: (b, 0, k, 0)),
                  pl.BlockSpec((nb, tt, C), lambda b, t, k: (b, t, gate_col))],
        out_specs=pl.BlockSpec((nb, tt, C), lambda b, t, k: (b, t, 0)),
        out_shape=jax.ShapeDtypeStruct((B, L, C), F32),
        scratch_shapes=[pltpu.VMEM((nb, tt, C), F32)],
        compiler_params=_params(("parallel", "parallel", "arbitrary"), 48),
        name="dft_inverse",
    )(ct, st, y, gate_arr)


def dft_tables(L):
    n = 2 * L
    f = lax.broadcasted_iota(I32, (L, L), 0)
    t = lax.broadcasted_iota(I32, (L, L), 1)
    m = ((2 * f + 1) * t) % (2 * n)
    ang = m.astype(F32) * (math.pi / n)
    cf = jnp.cos(ang)
    sf = jnp.sin(ang)
    return cf.astype(BF16), sf.astype(BF16), cf.T.astype(BF16), sf.T.astype(BF16)


def hyena_filter_taps(L, w1, b1, w2, b2, w3, b3, freq, wout, width):
    hp = lax.Precision.HIGHEST
    t = jnp.linspace(0.0, 1.0, L, dtype=F32)[:, None]
    bands = (FILTER_EMB - 1) // 2
    w = 2.0 * math.pi * jnp.arange(L, dtype=F32)[:, None] / L
    f = jnp.linspace(1e-4, bands - 1, bands, dtype=F32)[None, :]
    z = jnp.concatenate([t, jnp.cos(f * w), -jnp.sin(f * w)], axis=-1)
    h = jnp.sin(freq * (jnp.dot(z, w1, precision=hp) + b1))
    h = jnp.sin(freq * (jnp.dot(h, w2, precision=hp) + b2))
    h = jnp.sin(freq * (jnp.dot(h, w3, precision=hp) + b3))
    h = jnp.dot(h, wout, precision=hp)
    max_decay = math.log(FILTER_TARGET) / FILTER_FAST_DECAY
    min_decay = math.log(FILTER_TARGET) / FILTER_SLOW_DECAY
    deltas = jnp.abs(jnp.linspace(min_decay, max_decay, width, dtype=F32))
    decay = jnp.exp(-t * deltas[None, :])
    return h.reshape(L, HYENA_ORDER, 2, width) * decay[:, None, None, :]


def hyena_operator(uc, tables, taps, bias, nb, tile):
    cf, sf, ct, st = tables
    L, C = taps.shape[0], taps.shape[3]
    first = (lax.broadcasted_iota(I32, (L, 1), 0) == 0)
    sums, diffs = [], []
    for o in range(HYENA_ORDER):
        h_fwd, h_bwd = taps[:, o, 0], taps[:, o, 1]
        kp = h_fwd + jnp.where(first, h_bwd + bias[o][None, :], 0.0)
        km = jnp.where(first, 0.0, h_bwd)
        sums.append(kp + km)
        diffs.append(kp - km)
    tmm = min(L, 512)
    kre = matmul(cf, jnp.concatenate(sums, axis=1), tmm, tmm, tmm)
    kim = -matmul(sf, jnp.concatenate(diffs, axis=1), tmm, tmm, tmm)
    z_arr, z_col = uc, 2
    for o in range(HYENA_ORDER):
        y = dft_forward(cf, sf, z_arr, z_col, kre[:, o * C:(o + 1) * C], kim[:, o * C:(o + 1) * C], nb, tile, tile)
        z_arr, z_col = dft_inverse(ct, st, y, uc, o, nb, tile, tile), 0
    return z_arr


def _head_norm_rope(x, g, cos, sin_signed, scale):
    lane = lax.broadcasted_iota(I32, x.shape, 1)
    lo = lane < DIFF_HEAD_DIM
    x2 = x * x
    s_lo = jnp.sum(jnp.where(lo, x2, 0.0), axis=-1, keepdims=True)
    s_hi = jnp.sum(jnp.where(lo, 0.0, x2), axis=-1, keepdims=True)
    ms = jnp.where(lo, s_lo, s_hi) * (1.0 / DIFF_HEAD_DIM)
    xn = x * lax.rsqrt(ms + NORM_EPS) * g
    first = (lane & 16) == 0
    partner = jnp.where(first, pltpu.roll(xn, LANES - 16, 1), pltpu.roll(xn, 16, 1))
    return (xn * cos + partner * sin_signed) * scale


def _qkv_kernel(p_ref, cos_ref, sin_ref, qg_ref, kg_ref, *o_refs, has_q, width):
    cos = cos_ref[...]
    sin = sin_ref[...]
    x = p_ref[0]
    col = 0
    outs = list(o_refs)
    if has_q:
        q_ref = outs.pop(0)
        for hd in range(width // LANES):
            sl = slice(col + hd * LANES, col + (hd + 1) * LANES)
            q_ref[0, :, hd * LANES:(hd + 1) * LANES] = _head_norm_rope(
                x[:, sl], qg_ref[...], cos, sin, DIFF_HEAD_DIM ** -0.5).astype(BF16)
        col += width
    k_ref, v_ref = outs
    for hd in range(width // LANES):
        sl = slice(col + hd * LANES, col + (hd + 1) * LANES)
        k_ref[0, :, hd * LANES:(hd + 1) * LANES] = _head_norm_rope(x[:, sl], kg_ref[...], cos, sin, 1.0).astype(BF16)
    col += width
    v_ref[0] = x[:, col:col + width].astype(BF16)


def qkv_prep(p, col_block, has_q, cos, sin_signed, q_g, k_g, tm):
    B, S, _ = p.shape
    width = DIFF_HEADS * 2 * DIFF_HEAD_DIM
    n_out = 3 if has_q else 2
    g2 = lambda g: jnp.concatenate([g, g]).reshape(1, LANES)
    outs = pl.pallas_call(
        functools.partial(_qkv_kernel, has_q=has_q, width=width),
        grid=(B, S // tm),
        in_specs=[pl.BlockSpec((1, tm, n_out * width), lambda b, i: (b, i, col_block)),
                  pl.BlockSpec((tm, LANES), lambda b, i: (i, 0)),
                  pl.BlockSpec((tm, LANES), lambda b, i: (i, 0)),
                  pl.BlockSpec((1, LANES), lambda b, i: (0, 0)),
                  pl.BlockSpec((1, LANES), lambda b, i: (0, 0))],
        out_specs=[pl.BlockSpec((1, tm, width), lambda b, i: (b, i, 0))] * n_out,
        out_shape=[jax.ShapeDtypeStruct((B, S, width), BF16)] * n_out,
        compiler_params=_params(("parallel", "parallel"), 32),
        name="qkv_prep",
    )(p, cos, sin_signed, g2(q_g), g2(k_g))
    return outs


def rope_tables(S, use_rope):
    if not use_rope:
        return jnp.ones((S, LANES), F32), jnp.zeros((S, LANES), F32)
    t = jnp.arange(S, dtype=I32)
    row = (t // GRID_W).astype(F32)[:, None]
    colp = (t % GRID_W).astype(F32)[:, None]
    nf = DIFF_HEAD_DIM // 4
    inv = ROPE_BASE ** (-jnp.arange(nf, dtype=F32) / nf)
    lane = jnp.arange(LANES)
    grp = (lane % DIFF_HEAD_DIM) // nf
    j = lane % nf
    pos = jnp.where((grp < 2)[None, :], row, colp)
    ang = pos * inv[j][None, :]
    sign = jnp.where((grp % 2 == 0)[None, :], -1.0, 1.0)
    return jnp.cos(ang), jnp.sin(ang) * sign


def _attn_kernel(lam_ref, q_ref, kc_ref, vc_ref, *rest, has_lat, out_scale):
    if has_lat:
        kl_ref, vl_ref, g_ref, o_ref = rest
    else:
        g_ref, o_ref = rest
    lam = lam_ref[0]
    q = q_ref[0]
    lane = lax.broadcasted_iota(I32, q.shape, 1)
    nt = (((1,), (1,)), ((), ()))
    probs = []
    for mp in range(2):
        keep = (lane < DIFF_HEAD_DIM) if mp == 0 else (lane >= DIFF_HEAD_DIM)
        qm = jnp.where(keep, q, jnp.zeros_like(q))
        sc = lax.dot_general(qm, kc_ref[0], nt, preferred_element_type=F32)
        mx = jnp.max(sc, axis=-1, keepdims=True)
        if has_lat:
            sl = lax.dot_general(qm, kl_ref[0], nt, preferred_element_type=F32)
            mx = jnp.maximum(mx, jnp.max(sl, axis=-1, keepdims=True))
            el = jnp.exp(sl - mx)
        ec = jnp.exp(sc - mx)
        den = jnp.sum(ec, axis=-1, keepdims=True)
        if has_lat:
            den = den + jnp.sum(el, axis=-1, keepdims=True)
        inv = 1.0 / den
        if has_lat:
            probs.append((ec * inv, el * inv))
        else:
            probs.append((ec * inv,))
    a_c = probs[0][0] - lam * probs[1][0]
    o = _dot(a_c.astype(BF16), vc_ref[0])
    if has_lat:
        a_l = probs[0][1] - lam * probs[1][1]
        o = o + _dot(a_l.astype(BF16), vl_ref[0])
    ms = jnp.mean(o * o, axis=-1, keepdims=True)
    o_ref[0] = ((o * lax.rsqrt(ms + NORM_EPS) * g_ref[...]) * out_scale).astype(BF16)


def diff_attention(lam, q, k_ctx, v_ctx, k_lat, v_lat, subln_g, out_scale, tq):
    B, Sq, W = q.shape
    H = W // LANES
    has_lat = k_lat is not None
    Sc = k_ctx.shape[1]
    head = lambda S: pl.BlockSpec((1, S, LANES), lambda b, h, i: (b, 0, h))
    in_specs = [pl.BlockSpec(memory_space=pltpu.SMEM),
                pl.BlockSpec((1, tq, LANES), lambda b, h, i: (b, i, h)), head(Sc), head(Sc)]
    args = [lam.reshape(1), q, k_ctx, v_ctx]
    if has_lat:
        in_specs += [head(k_lat.shape[1]), head(k_lat.shape[1])]
        args += [k_lat, v_lat]
    in_specs.append(pl.BlockSpec((1, LANES), lambda b, h, i: (0, 0)))
    args.append(subln_g.reshape(1, LANES))
    return pl.pallas_call(
        functools.partial(_attn_kernel, has_lat=has_lat, out_scale=out_scale),
        grid=(B, H, Sq // tq),
        in_specs=in_specs,
        out_specs=pl.BlockSpec((1, tq, LANES), lambda b, h, i: (b, i, h)),
        out_shape=jax.ShapeDtypeStruct((B, Sq, W), BF16),
        compiler_params=_params(("parallel", "parallel", "parallel"), 56),
        name="diff_attention",
    )(*args)


def _oproj_kernel(hy_ref, o_ref, w_ref, h_ref, gate_ref, out_ref, *, half):
    y = _dot(hy_ref[0].astype(BF16), w_ref[0:half, :]) + _dot(o_ref[0], w_ref[half:, :])
    out_ref[0] = h_ref[0] + gate_ref[0] * y


def out_proj(hy, o, w_out, h, gate, tm):
    B, S, D = h.shape
    half = hy.shape[2]
    return pl.pallas_call(
        functools.partial(_oproj_kernel, half=half),
        grid=(B, S // tm),
        in_specs=[pl.BlockSpec((1, tm, half), lambda b, i: (b, i, 0)),
                  pl.BlockSpec((1, tm, half), lambda b, i: (b, i, 0)),
                  pl.BlockSpec((2 * half, D), lambda b, i: (0, 0)),
                  pl.BlockSpec((1, tm, D), lambda b, i: (b, i, 0)),
                  pl.BlockSpec((1, 1, D), lambda b, i: (b, 0, 0))],
        out_specs=pl.BlockSpec((1, tm, D), lambda b, i: (b, i, 0)),
        out_shape=jax.ShapeDtypeStruct((B, S, D), F32),
        compiler_params=_params(("parallel", "parallel"), 40),
        name="out_proj",
    )(hy, o, w_out, h, gate)


def _shift_rows(x, d):
    return pltpu.roll(x, (-d) % x.shape[0], 0)


def _pool_kernel(hp_ref, hc_ref, hn_ref, g_ref, sc_ref, sh_ref, gate_ref, pw_ref, ps_ref, o_ref, *, T, L):
    i = pl.program_id(1)
    nt = pl.num_programs(1)
    g, sc, sh = g_ref[...], sc_ref[0], sh_ref[0]
    hc = hc_ref[0]
    a_c = _norm_mod(hc, g, sc, sh)
    a_p = jnp.where(i == 0, 0.0, _norm_mod(hp_ref[0], g, sc, sh))
    a_n = jnp.where(i == nt - 1, 0.0, _norm_mod(hn_ref[0], g, sc, sh))
    ext = jnp.concatenate([a_p, a_c, a_n], axis=0)
    tok = i * T + lax.broadcasted_iota(I32, (T, 1), 0)
    G = ext.shape[1] // len(POOL_WINDOWS)
    ys = []
    for gi, w in enumerate(POOL_WINDOWS):
        xg = ext[:, gi * G:(gi + 1) * G]
        s = _shift_rows(xg, -1) + xg
        step = 1
        while 2 * step < w:
            s = _shift_rows(s, -step) + _shift_rows(s, step)
            step *= 2
        cnt = (jnp.minimum(tok + w // 2, L) - jnp.maximum(tok - w // 2, 0)).astype(F32)
        p = s[POOL_HALO:POOL_HALO + T] / cnt - a_c[:, gi * G:(gi + 1) * G]
        ys.append(_dot(p.astype(BF16), pw_ref[gi]))
    y = jnp.concatenate(ys, axis=1) * ps_ref[...]
    o_ref[0] = hc + gate_ref[0] * y


def pool_mixer(h, g, scale, shift, gate, pool_w, pool_scale, T):
    B, S, D = h.shape
    nh = T // POOL_HALO
    last = S // POOL_HALO - 1
    mod = pl.BlockSpec((1, 1, D), lambda b, i: (b, 0, 0))
    return pl.pallas_call(
        functools.partial(_pool_kernel, T=T, L=S),
        grid=(B, S // T),
        in_specs=[pl.BlockSpec((1, POOL_HALO, D), lambda b, i: (b, jnp.maximum(i * nh - 1, 0), 0)),
                  pl.BlockSpec((1, T, D), lambda b, i: (b, i, 0)),
                  pl.BlockSpec((1, POOL_HALO, D), lambda b, i: (b, jnp.minimum((i + 1) * nh, last), 0)),
                  pl.BlockSpec((1, D), lambda b, i: (0, 0)), mod, mod, mod,
                  pl.BlockSpec(pool_w.shape, lambda b, i: (0, 0, 0)),
                  pl.BlockSpec((1, D), lambda b, i: (0, 0))],
        out_specs=pl.BlockSpec((1, T, D), lambda b, i: (b, i, 0)),
        out_shape=jax.ShapeDtypeStruct((B, S, D), F32),
        compiler_params=_params(("parallel", "parallel"), 48),
        name="pool_mixer",
    )(h, h, h, g.reshape(1, D), scale, shift, gate, pool_w, pool_scale.reshape(1, D))


def _ffn_prep_kernel(h_ref, g_ref, sc_ref, sh_ref, rw_ref, x_ref, aff_ref, *, n_exp):
    a = _norm_mod(h_ref[0], g_ref[...], sc_ref[0], sh_ref[0]).astype(BF16)
    x_ref[0] = a
    logits = _dot(a, rw_ref[...])
    lane = lax.broadcasted_iota(I32, logits.shape, 1)
    valid = lane < n_exp
    mx = jnp.max(jnp.where(valid, logits, -jnp.inf), axis=-1, keepdims=True)
    e = jnp.where(valid, jnp.exp(logits - mx), 0.0)
    aff_ref[0] = e / jnp.sum(e, axis=-1, keepdims=True)


def ffn_prep(h, g, scale, shift, router_w, tm):
    B, S, D = h.shape
    n_exp = router_w.shape[1]
    rw = jnp.pad(router_w, ((0, 0), (0, LANES - n_exp))).astype(BF16)
    mod = pl.BlockSpec((1, 1, D), lambda b, i: (b, 0, 0))
    return pl.pallas_call(
        functools.partial(_ffn_prep_kernel, n_exp=n_exp),
        grid=(B, S // tm),
        in_specs=[pl.BlockSpec((1, tm, D), lambda b, i: (b, i, 0)),
                  pl.BlockSpec((1, D), lambda b, i: (0, 0)), mod, mod,
                  pl.BlockSpec((D, LANES), lambda b, i: (0, 0))],
        out_specs=[pl.BlockSpec((1, tm, D), lambda b, i: (b, i, 0)),
                   pl.BlockSpec((1, tm, LANES), lambda b, i: (b, i, 0))],
        out_shape=[jax.ShapeDtypeStruct((B, S, D), BF16), jax.ShapeDtypeStruct((B, S, LANES), F32)],
        compiler_params=_params(("parallel", "parallel"), 32),
        name="ffn_prep",
    )(h, g.reshape(1, D), scale, shift, rw)


def _prefix_excl(m, tri, tb):
    S = m.shape[0]
    carry = jnp.zeros((1, m.shape[1]), F32)
    outs = []
    for blk in range(S // tb):
        mb = m[blk * tb:(blk + 1) * tb]
        outs.append(_dot(tri, mb.astype(BF16)) + carry)
        carry = carry + jnp.sum(mb, axis=0, keepdims=True)
    return jnp.concatenate(outs, axis=0) if len(outs) > 1 else outs[0]


def _select_kernel(aff_ref, tri_ref, pos_ref, post_ref, *, cap, tb, n_exp):
    bits = pltpu.bitcast(aff_ref[0], I32)

    def body(i, cur):
        cand = cur | (jnp.int32(1) << (30 - i))
        cnt = jnp.sum((bits >= cand).astype(F32), axis=0, keepdims=True)
        return jnp.where(cnt >= cap, cand, cur)

    thr = lax.fori_loop(0, 31, body, jnp.zeros((1, bits.shape[1]), I32))
    tri = tri_ref[...]
    gt = (bits > thr).astype(F32)
    eq = (bits == thr).astype(F32)
    need = cap - jnp.sum(gt, axis=0, keepdims=True)
    sel = gt + eq * (_prefix_excl(eq, tri, tb) < need).astype(F32)
    sp = jnp.where(sel > 0.0, _prefix_excl(sel, tri, tb), -1.0)
    pos_ref[0] = sp
    for blk in range(sp.shape[0] // tb):
        post_ref[0, :, blk * tb:(blk + 1) * tb] = sp[blk * tb:(blk + 1) * tb].T[0:n_exp]


def select_tokens(aff, cap, n_exp):
    B, S, _ = aff.shape
    tb = min(S, 256)
    r = lax.broadcasted_iota(I32, (tb, tb), 0)
    c = lax.broadcasted_iota(I32, (tb, tb), 1)
    tri = (c < r).astype(BF16)
    return pl.pallas_call(
        functools.partial(_select_kernel, cap=cap, tb=tb, n_exp=n_exp),
        grid=(B,),
        in_specs=[pl.BlockSpec((1, S, LANES), lambda b: (b, 0, 0)),
                  pl.BlockSpec((tb, tb), lambda b: (0, 0))],
        out_specs=[pl.BlockSpec((1, S, LANES), lambda b: (b, 0, 0)),
                   pl.BlockSpec((1, n_exp, S), lambda b: (b, 0, 0))],
        out_shape=[jax.ShapeDtypeStruct((B, S, LANES), F32), jax.ShapeDtypeStruct((B, n_exp, S), F32)],
        compiler_params=_params(("parallel",), 48),
        name="select_tokens",
    )(aff, tri)


def _gather_kernel(x_ref, post_ref, o_ref, *, cap):
    e = pl.program_id(1)
    slot = post_ref[0, pl.ds(e, 1), :]
    r = lax.broadcasted_iota(I32, (cap, slot.shape[1]), 0).astype(F32)
    onehot = (r == slot).astype(BF16)
    o_ref[0, 0] = _dot(onehot, x_ref[0]).astype(BF16)


def gather_tokens(x, post, cap):
    B, S, D = x.shape
    E = post.shape[1]
    return pl.pallas_call(
        functools.partial(_gather_kernel, cap=cap),
        grid=(B, E),
        in_specs=[pl.BlockSpec((1, S, D), lambda b, e: (b, 0, 0)),
                  pl.BlockSpec((1, E, S), lambda b, e: (b, 0, 0))],
        out_specs=pl.BlockSpec((1, 1, cap, D), lambda b, e: (e, b, 0, 0)),
        out_shape=jax.ShapeDtypeStruct((E, B, cap, D), BF16),
        compiler_params=_params(("parallel", "arbitrary"), 48),
        name="gather_tokens",
    )(x, post)


def _ffn_kernel(xs_ref, wg_ref, wu_ref, wd_ref, o_ref, acc_ref, *, nf):
    f = pl.program_id(2)
    x = xs_ref[0]
    a = _dot(x, wg_ref[...].astype(BF16))
    u = _dot(x, wu_ref[...].astype(BF16))
    hm = (a * jax.nn.sigmoid(a) * u).astype(BF16)
    part = _dot(hm, wd_ref[...].astype(BF16))

    @pl.when(f == 0)
    def _():
        acc_ref[...] = part

    @pl.when(f > 0)
    def _():
        acc_ref[...] += part

    @pl.when(f == nf - 1)
    def _():
        o_ref[0] = acc_ref[...].astype(BF16)


def expert_ffn(xs, w_gate, w_up, w_down, layer, tm, tf):
    E, M, D = xs.shape
    F = w_gate.shape[3]
    nf = F // tf
    return pl.pallas_call(
        functools.partial(_ffn_kernel, nf=nf),
        grid=(E, M // tm, nf),
        in_specs=[pl.BlockSpec((1, tm, D), lambda e, m, f: (e, m, 0)),
                  pl.BlockSpec((None, None, D, tf), lambda e, m, f: (layer, e, 0, f)),
                  pl.BlockSpec((None, None, D, tf), lambda e, m, f: (layer, e, 0, f)),
                  pl.BlockSpec((None, None, tf, D), lambda e, m, f: (layer, e, f, 0))],
        out_specs=pl.BlockSpec((1, tm, D), lambda e, m, f: (e, m, 0)),
        out_shape=jax.ShapeDtypeStruct((E, M, D), BF16),
        scratch_shapes=[pltpu.VMEM((tm, D), F32)],
        compiler_params=_params(("parallel", "parallel", "arbitrary"), 56),
        name="expert_ffn",
    )(xs, w_gate, w_up, w_down)


def _combine_kernel(y_ref, pos_ref, aff_ref, h_ref, gate_ref, o_ref, acc_ref, *, cap, n_exp):
    e = pl.program_id(2)
    sp = pos_ref[0]
    lane = lax.broadcasted_iota(I32, sp.shape, 1)
    mine = lane == e
    slot = jnp.sum(jnp.where(mine, sp, 0.0), axis=-1, keepdims=True)
    g = jnp.sum(jnp.where(mine, aff_ref[0], 0.0), axis=-1, keepdims=True)
    r = lax.broadcasted_iota(I32, (sp.shape[0], cap), 1).astype(F32)
    onehot = (slot == r).astype(BF16)
    contrib = g * _dot(onehot, y_ref[0, 0])

    @pl.when(e == 0)
    def _():
        acc_ref[...] = contrib

    @pl.when(e > 0)
    def _():
        acc_ref[...] += contrib

    @pl.when(e == n_exp - 1)
    def _():
        o_ref[0] = h_ref[0] + gate_ref[0] * acc_ref[...]


def combine_tokens(y, pos, aff, h, gate, T):
    E, B, cap, D = y.shape
    S = h.shape[1]
    return pl.pallas_call(
        functools.partial(_combine_kernel, cap=cap, n_exp=E),
        grid=(B, S // T, E),
        in_specs=[pl.BlockSpec((1, 1, cap, D), lambda b, t, e: (e, b, 0, 0)),
                  pl.BlockSpec((1, T, LANES), lambda b, t, e: (b, t, 0)),
                  pl.BlockSpec((1, T, LANES), lambda b, t, e: (b, t, 0)),
                  pl.BlockSpec((1, T, D), lambda b, t, e: (b, t, 0)),
                  pl.BlockSpec((1, 1, D), lambda b, t, e: (b, 0, 0))],
        out_specs=pl.BlockSpec((1, T, D), lambda b, t, e: (b, t, 0)),
        out_shape=jax.ShapeDtypeStruct((B, S, D), F32),
        scratch_shapes=[pltpu.VMEM((T, D), F32)],
        compiler_params=_params(("parallel", "parallel", "arbitrary"), 48),
        name="combine_tokens",
    )(y, pos, aff, h, gate)


def moe_block(h, g, scale, shift, gate, router_w, w_gate, w_up, w_down, layer):
    B, S, D = h.shape
    E = router_w.shape[1]
    cap = EC_CAPACITY * S // E
    slots = max(cap, LANES)
    x, aff = ffn_prep(h, g, scale, shift, router_w, min(S, 512))
    pos, post = select_tokens(aff, cap, E)
    xs = gather_tokens(x, post, slots)
    M = B * slots
    y = expert_ffn(xs.reshape(E, M, D), w_gate, w_up, w_down, layer, min(M, 1024), 512)
    return combine_tokens(y.reshape(E, B, slots, D), pos, aff, h, gate, min(S, 1024))


def kernel(x, c, ctx, c_ctx, ada_w, ada_b, norm_mix_g, norm_ffn_g, w_in, w_out, hy_conv_w, hy_conv_b, hy_f_w1, hy_f_b1, hy_f_w2, hy_f_b2, hy_f_w3, hy_f_b3, hy_f_freq, hy_f_wout, hy_bias, q_norm_g, k_norm_g, diff_lambda, subln_g, pool_w, pool_scale, router_w, exp_w_gate, exp_w_up, exp_w_down):
    B, S, D = x.shape
    Lc = ctx.shape[1]
    depth = ada_w.shape[0]
    hy_width = hy_bias.shape[2]
    hy_proj = (HYENA_ORDER + 1) * hy_width
    qk_width = DIFF_HEADS * 2 * DIFF_HEAD_DIM
    last_attn = ((depth - 1) // 2) * 2

    s_all = jnp.concatenate([jax.nn.silu(c), jax.nn.silu(c_ctx)[None, :]], axis=0)
    rows = -(-s_all.shape[0] // SUBLANES) * SUBLANES
    mods = ada_modulation(jnp.pad(s_all, ((0, rows - s_all.shape[0]), (0, 0))), ada_w, ada_b)

    tables_lat = dft_tables(S)
    tables_ctx = dft_tables(Lc)
    rope_lat = rope_tables(S, True)
    rope_ctx = rope_tables(Lc, False)

    h, hc = x, ctx
    for l in range(depth):
        m = [mods[l, :B, i * D:(i + 1) * D].reshape(B, 1, D) for i in range(6)]
        mc = [jnp.broadcast_to(mods[l, B, i * D:(i + 1) * D].reshape(1, 1, D), (B, 1, D)) for i in range(6)]
        ctx_full = l < last_attn
        if l % 2 == 0:
            e = l // 2
            lam_init = 0.8 - 0.6 * math.exp(-0.3 * l)
            lv = diff_lambda[e]
            lam = jnp.exp(jnp.sum(lv[0] * lv[1])) - jnp.exp(jnp.sum(lv[2] * lv[3])) + lam_init
            filt = (hy_f_w1[e], hy_f_b1[e], hy_f_w2[e], hy_f_b2[e], hy_f_w3[e], hy_f_b3[e], hy_f_freq[e], hy_f_wout[e])
            w_in_b = w_in[e].astype(BF16)
            w_out_b = w_out[e].astype(BF16)

            p = norm_mod_matmul(h, norm_mix_g[l], m[1], m[0], w_in_b, 512)
            uc = short_conv(p, hy_conv_w[e], hy_conv_b[e], hy_proj, 256)
            hy = hyena_operator(uc, tables_lat, hyena_filter_taps(S, *filt, hy_width), hy_bias[e], math.gcd(B, 4), 512)
            q, k, v = qkv_prep(p, 1, True, *rope_lat, q_norm_g[e], k_norm_g[e], 512)

            if l <= last_attn:
                if ctx_full:
                    pc = norm_mod_matmul(hc, norm_mix_g[l], mc[1], mc[0], w_in_b, Lc)
                    qc, kc, vc = qkv_prep(pc, 1, True, *rope_ctx, q_norm_g[e], k_norm_g[e], Lc)
                else:
                    pc = norm_mod_matmul(hc, norm_mix_g[l], mc[1], mc[0], w_in_b[:, hy_proj + qk_width:], Lc)
                    kc, vc = qkv_prep(pc, 0, False, *rope_ctx, q_norm_g[e], k_norm_g[e], Lc)
            o = diff_attention(lam, q, kc, vc, k, v, subln_g[e], 1.0 - lam_init, 256)
            h = out_proj(hy, o, w_out_b, h, m[2], 512)
            if ctx_full:
                ucc = short_conv(pc, hy_conv_w[e], hy_conv_b[e], hy_proj, 256)
                hyc = hyena_operator(ucc, tables_ctx, hyena_filter_taps(Lc, *filt, hy_width), hy_bias[e], B, Lc)
                oc = diff_attention(lam, qc, kc, vc, None, None, subln_g[e], 1.0 - lam_init, Lc)
                hc = out_proj(hyc, oc, w_out_b, hc, mc[2], Lc)
        else:
            o_idx = l // 2
            pw = pool_w[o_idx].astype(BF16)
            h = pool_mixer(h, norm_mix_g[l], m[1], m[0], m[2], pw, pool_scale[o_idx], 512)
            if ctx_full:
                hc = pool_mixer(hc, norm_mix_g[l], mc[1], mc[0], mc[2], pw, pool_scale[o_idx], Lc)
        h = moe_block(h, norm_ffn_g[l], m[4], m[3], m[5], router_w[l], exp_w_gate, exp_w_up, exp_w_down, l)
        if ctx_full:
            hc = moe_block(hc, norm_ffn_g[l], mc[4], mc[3], mc[5], router_w[l], exp_w_gate, exp_w_up, exp_w_down, l)
    return h
```

```python
import functools
import math

import jax
import jax.numpy as jnp
from jax import lax
from jax.experimental import pallas as pl
from jax.experimental.pallas import tpu as pltpu

F32 = jnp.float32
BF16 = jnp.bfloat16
I32 = jnp.int32

NORM_EPS = 1e-6
GRID_W = 64
HYENA_ORDER = 2
FILTER_EMB = 33
FILTER_FAST_DECAY = 0.3
FILTER_SLOW_DECAY = 1.5
FILTER_TARGET = 1e-2
DIFF_HEADS = 4
DIFF_HEAD_DIM = 64
ROPE_BASE = 10000.0
POOL_WINDOWS = (2, 4, 8, 16)
EC_CAPACITY = 2
LANES = 128
SUBLANES = 8
POOL_HALO = 8
LOG2E = 1.4426950408889634
MOE_ROW_CHUNK = 256
HYENA_BLOCK = 1024


def _params(sem, vmem_mb):
    return pltpu.CompilerParams(dimension_semantics=sem, vmem_limit_bytes=vmem_mb * 1024 * 1024)


def _dot(a, b):
    return jnp.dot(a, b, preferred_element_type=F32)


def _norm_mod(x, g, scale, shift):
    ms = jnp.mean(x * x, axis=-1, keepdims=True)
    return (x * lax.rsqrt(ms + NORM_EPS) * g) * (1.0 + scale) + shift


def _mm_kernel(a_ref, b_ref, o_ref, acc_ref, *, nk):
    k = pl.program_id(2)
    part = _dot(a_ref[...].astype(BF16), b_ref[...].astype(BF16))

    @pl.when(k == 0)
    def _():
        acc_ref[...] = part

    @pl.when(k > 0)
    def _():
        acc_ref[...] += part

    @pl.when(k == nk - 1)
    def _():
        o_ref[...] = acc_ref[...].astype(o_ref.dtype)


def matmul(a, b, tm, tn, tk):
    M, K = a.shape
    N = b.shape[1]
    nk = K // tk
    return pl.pallas_call(
        functools.partial(_mm_kernel, nk=nk),
        grid=(M // tm, N // tn, nk),
        in_specs=[pl.BlockSpec((tm, tk), lambda i, j, k: (i, k)),
                  pl.BlockSpec((tk, tn), lambda i, j, k: (k, j))],
        out_specs=pl.BlockSpec((tm, tn), lambda i, j, k: (i, j)),
        out_shape=jax.ShapeDtypeStruct((M, N), F32),
        scratch_shapes=[pltpu.VMEM((tm, tn), F32)],
        compiler_params=_params(("parallel", "parallel", "arbitrary"), 40),
        name="matmul",
    )(a, b)


def _ada_kernel(s_ref, w_ref, b_ref, o_ref):
    o_ref[...] = _dot(s_ref[...].astype(BF16), w_ref[...].astype(BF16)) + b_ref[...]


def ada_modulation(s, ada_w, ada_b):
    depth, D, N = ada_w.shape
    R = s.shape[0]
    tn = 1024
    return pl.pallas_call(
        _ada_kernel,
        grid=(depth, N // tn),
        in_specs=[pl.BlockSpec((R, D), lambda l, j: (0, 0)),
                  pl.BlockSpec((None, D, tn), lambda l, j: (l, 0, j)),
                  pl.BlockSpec((None, 1, tn), lambda l, j: (l, 0, j))],
        out_specs=pl.BlockSpec((None, R, tn), lambda l, j: (l, 0, j)),
        out_shape=jax.ShapeDtypeStruct((depth, R, N), F32),
        compiler_params=_params(("parallel", "parallel"), 32),
        name="ada_modulation",
    )(s, ada_w, ada_b.reshape(depth, 1, N))


def _nmm_kernel(h_ref, g_ref, sc_ref, sh_ref, w_ref, o_ref):
    a = _norm_mod(h_ref[0], g_ref[...], sc_ref[0], sh_ref[0])
    o_ref[0] = _dot(a.astype(BF16), w_ref[...])


def norm_mod_matmul(h, g, scale, shift, w, tm):
    B, S, D = h.shape
    N = w.shape[1]
    return pl.pallas_call(
        _nmm_kernel,
        grid=(B, S // tm),
        in_specs=[pl.BlockSpec((1, tm, D), lambda b, i: (b, i, 0)),
                  pl.BlockSpec((1, D), lambda b, i: (0, 0)),
                  pl.BlockSpec((1, 1, D), lambda b, i: (b, 0, 0)),
                  pl.BlockSpec((1, 1, D), lambda b, i: (b, 0, 0)),
                  pl.BlockSpec((D, N), lambda b, i: (0, 0))],
        out_specs=pl.BlockSpec((1, tm, N), lambda b, i: (b, i, 0)),
        out_shape=jax.ShapeDtypeStruct((B, S, N), F32),
        compiler_params=_params(("parallel", "parallel"), 48),
        name="norm_mod_matmul",
    )(h, g.reshape(1, D), scale, shift, w)


def _sconv_kernel(p_ref, w_ref, b_ref, o_ref):
    x = p_ref[0]
    S = x.shape[0]
    row = lax.broadcasted_iota(I32, x.shape, 0)
    xm = jnp.where(row == 0, 0.0, pltpu.roll(x, 1, 0))
    xp = jnp.where(row == S - 1, 0.0, pltpu.roll(x, S - 1, 0))
    w = w_ref[...]
    o_ref[0] = (xm * w[0:1] + x * w[1:2] + xp * w[2:3] + b_ref[...]).astype(o_ref.dtype)


def short_conv(p, conv_w, conv_b, col0, width, tc, out_dtype):
    B, S, _ = p.shape
    c0 = col0 // tc
    return pl.pallas_call(
        _sconv_kernel,
        grid=(B, width // tc),
        in_specs=[pl.BlockSpec((1, S, tc), lambda b, c: (b, 0, c + c0)),
                  pl.BlockSpec((3, tc), lambda b, c: (0, c + c0)),
                  pl.BlockSpec((1, tc), lambda b, c: (0, c + c0))],
        out_specs=pl.BlockSpec((1, S, tc), lambda b, c: (b, 0, c)),
        out_shape=jax.ShapeDtypeStruct((B, S, width), out_dtype),
        compiler_params=_params(("parallel", "parallel"), 48),
        name="short_conv",
    )(p, conv_w, conv_b.reshape(1, -1))


def _dft_fwd_kernel(c_ref, s_ref, u_ref, k_ref, y_ref, *, nblk):
    c = c_ref[...]
    s = s_ref[...]
    ure, uim = [], []
    for j in range(nblk):
        u = u_ref[0, j]
        ure.append(_dot(c, u))
        uim.append(-_dot(s, u))
    for i in range(nblk):
        yre = yim = None
        for j in range(nblk):
            d = i - j + nblk - 1
            kre = k_ref[d, 0]
            kim = k_ref[d, 1]
            tre = kre * ure[j] - kim * uim[j]
            tim = kre * uim[j] + kim * ure[j]
            yre = tre if yre is None else yre + tre
            yim = tim if yim is None else yim + tim
        y_ref[0, i, 0] = yre.astype(BF16)
        y_ref[0, i, 1] = yim.astype(BF16)


def dft_forward(cf, sf, u, kspec, tf):
    B, nblk, Lb, C = u.shape
    nd = 2 * nblk - 1
    return pl.pallas_call(
        functools.partial(_dft_fwd_kernel, nblk=nblk),
        grid=(Lb // tf, B),
        in_specs=[pl.BlockSpec((tf, Lb), lambda f, b: (f, 0)),
                  pl.BlockSpec((tf, Lb), lambda f, b: (f, 0)),
                  pl.BlockSpec((1, nblk, Lb, C), lambda f, b: (b, 0, 0, 0)),
                  pl.BlockSpec((nd, 2, tf, C), lambda f, b: (0, 0, f, 0))],
        out_specs=pl.BlockSpec((1, nblk, 2, tf, C), lambda f, b: (b, 0, 0, f, 0)),
        out_shape=jax.ShapeDtypeStruct((B, nblk, 2, Lb, C), BF16),
        compiler_params=_params(("parallel", "parallel"), 56),
        name="dft_forward",
    )(cf, sf, u, kspec)


def _dft_inv_kernel(ct_ref, st_ref, y_ref, gate_ref, o_ref, *, scale):
    acc = _dot(ct_ref[...], y_ref[0, 0, 0]) - _dot(st_ref[...], y_ref[0, 0, 1])
    o_ref[0, 0] = (gate_ref[0, 0] * (acc * scale)).astype(BF16)


def dft_inverse(ct, st, y, gates, gate_col):
    B, nblk, _, Lb, C = y.shape
    return pl.pallas_call(
        functools.partial(_dft_inv_kernel, scale=1.0 / Lb),
        grid=(B, nblk),
        in_specs=[pl.BlockSpec((Lb, Lb), lambda b, i: (0, 0)),
                  pl.BlockSpec((Lb, Lb), lambda b, i: (0, 0)),
                  pl.BlockSpec((1, 1, 2, Lb, C), lambda b, i: (b, i, 0, 0, 0)),
                  pl.BlockSpec((1, 1, Lb, C), lambda b, i: (b, i, 0, gate_col))],
        out_specs=pl.BlockSpec((1, 1, Lb, C), lambda b, i: (b, i, 0, 0)),
        out_shape=jax.ShapeDtypeStruct((B, nblk, Lb, C), BF16),
        compiler_params=_params(("parallel", "parallel"), 48),
        name="dft_inverse",
    )(ct, st, y, gates)


def dft_tables(L):
    n = 2 * L
    f = lax.broadcasted_iota(I32, (L, L), 0)
    t = lax.broadcasted_iota(I32, (L, L), 1)
    m = ((2 * f + 1) * t) % (2 * n)
    ang = m.astype(F32) * (math.pi / n)
    cf = jnp.cos(ang)
    sf = jnp.sin(ang)
    return cf.astype(BF16), sf.astype(BF16), cf.T.astype(BF16), sf.T.astype(BF16)


def hyena_filter_taps(L, w1, b1, w2, b2, w3, b3, freq, wout, width):
    hp = lax.Precision.HIGHEST
    t = jnp.linspace(0.0, 1.0, L, dtype=F32)[:, None]
    bands = (FILTER_EMB - 1) // 2
    w = 2.0 * math.pi * jnp.arange(L, dtype=F32)[:, None] / L
    f = jnp.linspace(1e-4, bands - 1, bands, dtype=F32)[None, :]
    z = jnp.concatenate([t, jnp.cos(f * w), -jnp.sin(f * w)], axis=-1)
    h = jnp.sin(freq * (jnp.dot(z, w1, precision=hp) + b1))
    h = jnp.sin(freq * (jnp.dot(h, w2, precision=hp) + b2))
    h = jnp.sin(freq * (jnp.dot(h, w3, precision=hp) + b3))
    h = jnp.dot(h, wout, precision=hp)
    max_decay = math.log(FILTER_TARGET) / FILTER_FAST_DECAY
    min_decay = math.log(FILTER_TARGET) / FILTER_SLOW_DECAY
    deltas = jnp.abs(jnp.linspace(min_decay, max_decay, width, dtype=F32))
    decay = jnp.exp(-t * deltas[None, :])
    return h.reshape(L, HYENA_ORDER, 2, width) * decay[:, None, None, :]


def filter_spectra(cf, sf, taps, bias, nblk):
    L, C = taps.shape[0], taps.shape[3]
    Lb = L // nblk
    nd = 2 * nblk - 1
    zero = jnp.zeros((1, C), F32)
    sums, diffs = [], []
    for o in range(HYENA_ORDER):
        h_fwd, h_bwd = taps[:, o, 0], taps[:, o, 1]
        kfull = jnp.concatenate([zero, h_bwd[:0:-1], (h_fwd[0] + h_bwd[0] + bias[o])[None, :], h_fwd[1:]], axis=0)
        for dlt in range(-(nblk - 1), nblk):
            base = L + dlt * Lb
            kp = kfull[base:base + Lb]
            km = jnp.concatenate([zero, kfull[base - Lb + 1:base][::-1]], axis=0)
            sums.append(kp + km)
            diffs.append(kp - km)
    tmm = min(Lb, 512)
    kre = matmul(cf, jnp.concatenate(sums, axis=1), tmm, tmm, tmm)
    kim = -matmul(sf, jnp.concatenate(diffs, axis=1), tmm, tmm, tmm)
    shape = (Lb, HYENA_ORDER, nd, C)
    return jnp.stack([kre.reshape(shape), kim.reshape(shape)], axis=0).transpose(2, 3, 0, 1, 4)


def hyena_operator(gates, v, tables, taps, bias, nblk, tf):
    cf, sf, ct, st = tables
    B, L, C = v.shape
    Lb = L // nblk
    kspec = filter_spectra(cf, sf, taps, bias, nblk)
    gates = gates.reshape(B, nblk, Lb, 2 * C)
    z = v.reshape(B, nblk, Lb, C)
    for o in range(HYENA_ORDER):
        y = dft_forward(cf, sf, z, kspec[o], tf)
        z = dft_inverse(ct, st, y, gates, o)
    return z.reshape(B, L, C)


def _head_norm_rope(x, g, cos, sin_signed, scale):
    lane = lax.broadcasted_iota(I32, x.shape, 1)
    lo = lane < DIFF_HEAD_DIM
    x2 = x * x
    s_lo = jnp.sum(jnp.where(lo, x2, 0.0), axis=-1, keepdims=True)
    s_hi = jnp.sum(jnp.where(lo, 0.0, x2), axis=-1, keepdims=True)
    ms = jnp.where(lo, s_lo, s_hi) * (1.0 / DIFF_HEAD_DIM)
    xn = x * lax.rsqrt(ms + NORM_EPS) * g
    first = (lane & 16) == 0
    partner = jnp.where(first, pltpu.roll(xn, LANES - 16, 1), pltpu.roll(xn, 16, 1))
    return (xn * cos + partner * sin_signed) * scale


def _qkv_kernel(p_ref, cos_ref, sin_ref, qg_ref, kg_ref, *o_refs, has_q, width):
    cos = cos_ref[...]
    sin = sin_ref[...]
    x = p_ref[0]
    col = 0
    outs = list(o_refs)
    if has_q:
        q_ref = outs.pop(0)
        for hd in range(width // LANES):
            sl = slice(col + hd * LANES, col + (hd + 1) * LANES)
            q_ref[0, :, hd * LANES:(hd + 1) * LANES] = _head_norm_rope(
                x[:, sl], qg_ref[...], cos, sin, LOG2E * DIFF_HEAD_DIM ** -0.5).astype(BF16)
        col += width
    k_ref, v_ref = outs
    for hd in range(width // LANES):
        sl = slice(col + hd * LANES, col + (hd + 1) * LANES)
        k_ref[0, :, hd * LANES:(hd + 1) * LANES] = _head_norm_rope(x[:, sl], kg_ref[...], cos, sin, 1.0).astype(BF16)
    col += width
    v_ref[0] = x[:, col:col + width].astype(BF16)


def qkv_prep(p, col_block, has_q, cos, sin_signed, q_g, k_g, tm):
    B, S, _ = p.shape
    width = DIFF_HEADS * 2 * DIFF_HEAD_DIM
    n_out = 3 if has_q else 2
    g2 = lambda g: jnp.concatenate([g, g]).reshape(1, LANES)
    outs = pl.pallas_call(
        functools.partial(_qkv_kernel, has_q=has_q, width=width),
        grid=(B, S // tm),
        in_specs=[pl.BlockSpec((1, tm, n_out * width), lambda b, i: (b, i, col_block)),
                  pl.BlockSpec((tm, LANES), lambda b, i: (i, 0)),
                  pl.BlockSpec((tm, LANES), lambda b, i: (i, 0)),
                  pl.BlockSpec((1, LANES), lambda b, i: (0, 0)),
                  pl.BlockSpec((1, LANES), lambda b, i: (0, 0))],
        out_specs=[pl.BlockSpec((1, tm, width), lambda b, i: (b, i, 0))] * n_out,
        out_shape=[jax.ShapeDtypeStruct((B, S, width), BF16)] * n_out,
        compiler_params=_params(("parallel", "parallel"), 32),
        name="qkv_prep",
    )(p, cos, sin_signed, g2(q_g), g2(k_g))
    return outs


def rope_tables(S, use_rope):
    if not use_rope:
        return jnp.ones((S, LANES), F32), jnp.zeros((S, LANES), F32)
    t = jnp.arange(S, dtype=I32)
    row = (t // GRID_W).astype(F32)[:, None]
    colp = (t % GRID_W).astype(F32)[:, None]
    nf = DIFF_HEAD_DIM // 4
    inv = ROPE_BASE ** (-jnp.arange(nf, dtype=F32) / nf)
    lane = jnp.arange(LANES)
    grp = (lane % DIFF_HEAD_DIM) // nf
    j = lane % nf
    pos = jnp.where((grp < 2)[None, :], row, colp)
    ang = pos * inv[j][None, :]
    sign = jnp.where((grp % 2 == 0)[None, :], -1.0, 1.0)
    return jnp.cos(ang), jnp.sin(ang) * sign


def _attn_kernel(lam_ref, q_ref, kc_ref, vc_ref, *rest, has_lat, out_scale):
    if has_lat:
        kl_ref, vl_ref, g_ref, o_ref = rest
    else:
        g_ref, o_ref = rest
    lam = lam_ref[0]
    q = q_ref[0]
    lane = lax.broadcasted_iota(I32, q.shape, 1)
    nt = (((1,), (1,)), ((), ()))
    maps = []
    for mp in range(2):
        keep = (lane < DIFF_HEAD_DIM) if mp == 0 else (lane >= DIFF_HEAD_DIM)
        qm = jnp.where(keep, q, jnp.zeros_like(q))
        sc = lax.dot_general(qm, kc_ref[0], nt, preferred_element_type=F32)
        mx = jnp.max(sc, axis=-1, keepdims=True)
        if has_lat:
            sl = lax.dot_general(qm, kl_ref[0], nt, preferred_element_type=F32)
            mx = jnp.maximum(mx, jnp.max(sl, axis=-1, keepdims=True))
        ec = jnp.exp2(sc - mx)
        den = jnp.sum(ec, axis=-1, keepdims=True)
        om = _dot(ec.astype(BF16), vc_ref[0])
        if has_lat:
            el = jnp.exp2(sl - mx)
            den = den + jnp.sum(el, axis=-1, keepdims=True)
            om = om + _dot(el.astype(BF16), vl_ref[0])
        maps.append(om * (1.0 / den))
    o = maps[0] - lam * maps[1]
    ms = jnp.mean(o * o, axis=-1, keepdims=True)
    o_ref[0] = ((o * lax.rsqrt(ms + NORM_EPS) * g_ref[...]) * out_scale).astype(BF16)


def diff_attention(lam, q, k_ctx, v_ctx, k_lat, v_lat, subln_g, out_scale, tq):
    B, Sq, W = q.shape
    H = W // LANES
    has_lat = k_lat is not None
    Sc = k_ctx.shape[1]
    head = lambda S: pl.BlockSpec((1, S, LANES), lambda b, h, i: (b, 0, h))
    in_specs = [pl.BlockSpec(memory_space=pltpu.SMEM),
                pl.BlockSpec((1, tq, LANES), lambda b, h, i: (b, i, h)), head(Sc), head(Sc)]
    args = [lam.reshape(1), q, k_ctx, v_ctx]
    if has_lat:
        in_specs += [head(k_lat.shape[1]), head(k_lat.shape[1])]
        args += [k_lat, v_lat]
    in_specs.append(pl.BlockSpec((1, LANES), lambda b, h, i: (0, 0)))
    args.append(subln_g.reshape(1, LANES))
    return pl.pallas_call(
        functools.partial(_attn_kernel, has_lat=has_lat, out_scale=out_scale),
        grid=(B, H, Sq // tq),
        in_specs=in_specs,
        out_specs=pl.BlockSpec((1, tq, LANES), lambda b, h, i: (b, i, h)),
        out_shape=jax.ShapeDtypeStruct((B, Sq, W), BF16),
        compiler_params=_params(("parallel", "parallel", "parallel"), 56),
        name="diff_attention",
    )(*args)


def _oproj_kernel(hy_ref, o_ref, w_ref, h_ref, gate_ref, out_ref, *, half):
    y = _dot(hy_ref[0].astype(BF16), w_ref[0:half, :]) + _dot(o_ref[0], w_ref[half:, :])
    out_ref[0] = h_ref[0] + gate_ref[0] * y


def out_proj(hy, o, w_out, h, gate, tm):
    B, S, D = h.shape
    half = hy.shape[2]
    return pl.pallas_call(
        functools.partial(_oproj_kernel, half=half),
        grid=(B, S // tm),
        in_specs=[pl.BlockSpec((1, tm, half), lambda b, i: (b, i, 0)),
                  pl.BlockSpec((1, tm, half), lambda b, i: (b, i, 0)),
                  pl.BlockSpec((2 * half, D), lambda b, i: (0, 0)),
                  pl.BlockSpec((1, tm, D), lambda b, i: (b, i, 0)),
                  pl.BlockSpec((1, 1, D), lambda b, i: (b, 0, 0))],
        out_specs=pl.BlockSpec((1, tm, D), lambda b, i: (b, i, 0)),
        out_shape=jax.ShapeDtypeStruct((B, S, D), F32),
        compiler_params=_params(("parallel", "parallel"), 40),
        name="out_proj",
    )(hy, o, w_out, h, gate)


def _shift_rows(x, d):
    return pltpu.roll(x, (-d) % x.shape[0], 0)


def _pool_kernel(hp_ref, hc_ref, hn_ref, g_ref, sc_ref, sh_ref, gate_ref, pw_ref, ps_ref, o_ref, *, T, L):
    i = pl.program_id(1)
    nt = pl.num_programs(1)
    g, sc, sh = g_ref[...], sc_ref[0], sh_ref[0]
    hc = hc_ref[0]
    a_c = _norm_mod(hc, g, sc, sh)
    a_p = jnp.where(i == 0, 0.0, _norm_mod(hp_ref[0], g, sc, sh))
    a_n = jnp.where(i == nt - 1, 0.0, _norm_mod(hn_ref[0], g, sc, sh))
    ext = jnp.concatenate([a_p, a_c, a_n], axis=0)
    tok = i * T + lax.broadcasted_iota(I32, (T, 1), 0)
    G = ext.shape[1] // len(POOL_WINDOWS)
    ys = []
    for gi, w in enumerate(POOL_WINDOWS):
        xg = ext[:, gi * G:(gi + 1) * G]
        s = _shift_rows(xg, -1) + xg
        step = 1
        while 2 * step < w:
            s = _shift_rows(s, -step) + _shift_rows(s, step)
            step *= 2
        cnt = (jnp.minimum(tok + w // 2, L) - jnp.maximum(tok - w // 2, 0)).astype(F32)
        p = s[POOL_HALO:POOL_HALO + T] / cnt - a_c[:, gi * G:(gi + 1) * G]
        ys.append(_dot(p.astype(BF16), pw_ref[gi]))
    y = jnp.concatenate(ys, axis=1) * ps_ref[...]
    o_ref[0] = hc + gate_ref[0] * y


def pool_mixer(h, g, scale, shift, gate, pool_w, pool_scale, T):
    B, S, D = h.shape
    nh = T // POOL_HALO
    last = S // POOL_HALO - 1
    mod = pl.BlockSpec((1, 1, D), lambda b, i: (b, 0, 0))
    return pl.pallas_call(
        functools.partial(_pool_kernel, T=T, L=S),
        grid=(B, S // T),
        in_specs=[pl.BlockSpec((1, POOL_HALO, D), lambda b, i: (b, jnp.maximum(i * nh - 1, 0), 0)),
                  pl.BlockSpec((1, T, D), lambda b, i: (b, i, 0)),
                  pl.BlockSpec((1, POOL_HALO, D), lambda b, i: (b, jnp.minimum((i + 1) * nh, last), 0)),
                  pl.BlockSpec((1, D), lambda b, i: (0, 0)), mod, mod, mod,
                  pl.BlockSpec(pool_w.shape, lambda b, i: (0, 0, 0)),
                  pl.BlockSpec((1, D), lambda b, i: (0, 0))],
        out_specs=pl.BlockSpec((1, T, D), lambda b, i: (b, i, 0)),
        out_shape=jax.ShapeDtypeStruct((B, S, D), F32),
        compiler_params=_params(("parallel", "parallel"), 48),
        name="pool_mixer",
    )(h, h, h, g.reshape(1, D), scale, shift, gate, pool_w, pool_scale.reshape(1, D))


def _ffn_prep_kernel(h_ref, g_ref, sc_ref, sh_ref, rw_ref, x_ref, aff_ref, *, n_exp):
    a = _norm_mod(h_ref[0], g_ref[...], sc_ref[0], sh_ref[0]).astype(BF16)
    x_ref[0] = a
    logits = _dot(a, rw_ref[...])
    lane = lax.broadcasted_iota(I32, logits.shape, 1)
    valid = lane < n_exp
    mx = jnp.max(jnp.where(valid, logits, -jnp.inf), axis=-1, keepdims=True)
    e = jnp.where(valid, jnp.exp(logits - mx), 0.0)
    aff_ref[0] = e / jnp.sum(e, axis=-1, keepdims=True)


def ffn_prep(h, g, scale, shift, router_w, tm):
    B, S, D = h.shape
    n_exp = router_w.shape[1]
    rw = jnp.pad(router_w, ((0, 0), (0, LANES - n_exp))).astype(BF16)
    mod = pl.BlockSpec((1, 1, D), lambda b, i: (b, 0, 0))
    return pl.pallas_call(
        functools.partial(_ffn_prep_kernel, n_exp=n_exp),
        grid=(B, S // tm),
        in_specs=[pl.BlockSpec((1, tm, D), lambda b, i: (b, i, 0)),
                  pl.BlockSpec((1, D), lambda b, i: (0, 0)), mod, mod,
                  pl.BlockSpec((D, LANES), lambda b, i: (0, 0))],
        out_specs=[pl.BlockSpec((1, tm, D), lambda b, i: (b, i, 0)),
                   pl.BlockSpec((1, tm, LANES), lambda b, i: (b, i, 0))],
        out_shape=[jax.ShapeDtypeStruct((B, S, D), BF16), jax.ShapeDtypeStruct((B, S, LANES), F32)],
        compiler_params=_params(("parallel", "parallel"), 32),
        name="ffn_prep",
    )(h, g.reshape(1, D), scale, shift, rw)


def _prefix_excl(m, tri, tb):
    S = m.shape[0]
    carry = jnp.zeros((1, m.shape[1]), F32)
    outs, carries = [], []
    for blk in range(S // tb):
        mb = m[blk * tb:(blk + 1) * tb]
        outs.append(_dot(tri, mb.astype(BF16)) + carry)
        carries.append(carry)
        carry = carry + jnp.sum(mb, axis=0, keepdims=True)
    carries.append(carry)
    return (jnp.concatenate(outs, axis=0) if len(outs) > 1 else outs[0]), jnp.concatenate(carries, axis=0)


def _select_kernel(aff_ref, tri_ref, pos_ref, post_ref, offs_ref, *, cap, tb, n_exp):
    bits = pltpu.bitcast(aff_ref[0], I32)

    def body(i, cur):
        cand = cur | (jnp.int32(1) << (30 - i))
        cnt = jnp.sum((bits >= cand).astype(F32), axis=0, keepdims=True)
        return jnp.where(cnt >= cap, cand, cur)

    thr = lax.fori_loop(0, 31, body, jnp.zeros((1, bits.shape[1]), I32))
    tri = tri_ref[...]
    gt = (bits > thr).astype(F32)
    eq = (bits == thr).astype(F32)
    need = cap - jnp.sum(gt, axis=0, keepdims=True)
    sel = gt + eq * (_prefix_excl(eq, tri, tb)[0] < need).astype(F32)
    slot, offs = _prefix_excl(sel, tri, tb)
    sp = jnp.where(sel > 0.0, slot, -1.0)
    pos_ref[0] = sp
    offs_ref[0] = offs
    for blk in range(sp.shape[0] // tb):
        post_ref[0, :, blk * tb:(blk + 1) * tb] = sp[blk * tb:(blk + 1) * tb].T[0:n_exp]


def select_tokens(aff, cap, n_exp):
    B, S, _ = aff.shape
    tb = min(S, 256)
    r = lax.broadcasted_iota(I32, (tb, tb), 0)
    c = lax.broadcasted_iota(I32, (tb, tb), 1)
    tri = (c < r).astype(BF16)
    nblk = S // tb
    pos, post, offs = pl.pallas_call(
        functools.partial(_select_kernel, cap=cap, tb=tb, n_exp=n_exp),
        grid=(B,),
        in_specs=[pl.BlockSpec((1, S, LANES), lambda b: (b, 0, 0)),
                  pl.BlockSpec((tb, tb), lambda b: (0, 0))],
        out_specs=[pl.BlockSpec((1, S, LANES), lambda b: (b, 0, 0)),
                   pl.BlockSpec((1, n_exp, S), lambda b: (b, 0, 0)),
                   pl.BlockSpec((1, nblk + 1, LANES), lambda b: (b, 0, 0))],
        out_shape=[jax.ShapeDtypeStruct((B, S, LANES), F32), jax.ShapeDtypeStruct((B, n_exp, S), F32),
                   jax.ShapeDtypeStruct((B, nblk + 1, LANES), F32)],
        compiler_params=_params(("parallel",), 48),
        name="select_tokens",
    )(aff, tri)
    offs = jnp.swapaxes(offs[:, :, :n_exp], 1, 2).astype(I32).reshape(-1)
    return pos, post.reshape(B, n_exp, nblk, tb), offs


def _gather_kernel(offs_ref, x_ref, post_ref, o_ref, acc_ref, *, slots, rc, tb, nblk, n_exp):
    b = pl.program_id(0)
    e = pl.program_id(1)
    base = (b * n_exp + e) * (nblk + 1)
    r = lax.broadcasted_iota(I32, (rc, tb), 0).astype(F32)
    for k in range(slots // rc):
        i0 = jnp.int32(0)
        i1 = jnp.int32(0)
        for i in range(nblk):
            i0 += (offs_ref[base + i + 1] <= k * rc).astype(I32)
            i1 += (offs_ref[base + i] < (k + 1) * rc).astype(I32)
        acc_ref[...] = jnp.zeros_like(acc_ref)

        def body(i, carry):
            slot = post_ref[0, pl.ds(e, 1), pl.ds(i, 1), :].reshape(1, tb)
            onehot = (r + float(k * rc) == slot).astype(BF16)
            acc_ref[...] += _dot(onehot, x_ref[0, pl.ds(pl.multiple_of(i * tb, tb), tb), :])
            return carry

        lax.fori_loop(i0, i1, body, 0)
        o_ref[0, 0, k * rc:(k + 1) * rc, :] = acc_ref[...].astype(BF16)


def gather_tokens(x, post, offs, slots):
    B, S, D = x.shape
    _, E, nblk, tb = post.shape
    rc = min(slots, MOE_ROW_CHUNK)
    return pl.pallas_call(
        functools.partial(_gather_kernel, slots=slots, rc=rc, tb=tb, nblk=nblk, n_exp=E),
        grid_spec=pltpu.PrefetchScalarGridSpec(
            num_scalar_prefetch=1,
            grid=(B, E),
            in_specs=[pl.BlockSpec((1, S, D), lambda b, e, offs: (b, 0, 0)),
                      pl.BlockSpec((1, E, nblk, tb), lambda b, e, offs: (b, 0, 0, 0))],
            out_specs=pl.BlockSpec((1, 1, slots, D), lambda b, e, offs: (e, b, 0, 0)),
            scratch_shapes=[pltpu.VMEM((rc, D), F32)]),
        out_shape=jax.ShapeDtypeStruct((E, B, slots, D), BF16),
        compiler_params=_params(("parallel", "arbitrary"), 48),
        name="gather_tokens",
    )(offs, x, post)


def _ffn_kernel(xs_ref, wg_ref, wu_ref, wd_ref, o_ref, acc_ref, *, nf):
    f = pl.program_id(2)
    x = xs_ref[0]
    a = _dot(x, wg_ref[...].astype(BF16))
    u = _dot(x, wu_ref[...].astype(BF16))
    hm = (a * jax.nn.sigmoid(a) * u).astype(BF16)
    part = _dot(hm, wd_ref[...].astype(BF16))

    @pl.when(f == 0)
    def _():
        acc_ref[...] = part

    @pl.when(f > 0)
    def _():
        acc_ref[...] += part

    @pl.when(f == nf - 1)
    def _():
        o_ref[0] = acc_ref[...].astype(BF16)


def expert_ffn(xs, w_gate, w_up, w_down, layer, tm, tf):
    E, M, D = xs.shape
    F = w_gate.shape[3]
    nf = F // tf
    return pl.pallas_call(
        functools.partial(_ffn_kernel, nf=nf),
        grid=(E, M // tm, nf),
        in_specs=[pl.BlockSpec((1, tm, D), lambda e, m, f: (e, m, 0)),
                  pl.BlockSpec((None, None, D, tf), lambda e, m, f: (layer, e, 0, f)),
                  pl.BlockSpec((None, None, D, tf), lambda e, m, f: (layer, e, 0, f)),
                  pl.BlockSpec((None, None, tf, D), lambda e, m, f: (layer, e, f, 0))],
        out_specs=pl.BlockSpec((1, tm, D), lambda e, m, f: (e, m, 0)),
        out_shape=jax.ShapeDtypeStruct((E, M, D), BF16),
        scratch_shapes=[pltpu.VMEM((tm, D), F32)],
        compiler_params=_params(("parallel", "parallel", "arbitrary"), 56),
        name="expert_ffn",
    )(xs, w_gate, w_up, w_down)


def _combine_kernel(offs_ref, y_ref, pos_ref, aff_ref, h_ref, gate_ref, o_ref, acc_ref, *, rc, tb, nblk, n_exp):
    b = pl.program_id(0)
    t = pl.program_id(1)
    e = pl.program_id(2)
    per_tile = pos_ref.shape[1] // tb
    base = (b * n_exp + e) * (nblk + 1) + t * per_tile

    @pl.when(e == 0)
    def _():
        acc_ref[...] = jnp.zeros_like(acc_ref)

    lane = lax.broadcasted_iota(I32, (tb, LANES), 1)
    mine = lane == e
    r = lax.broadcasted_iota(I32, (tb, rc), 1).astype(F32)
    for i in range(per_tile):
        rows = slice(i * tb, (i + 1) * tb)
        lo = offs_ref[base + i]
        hi = offs_ref[base + i + 1]
        k0 = lo // rc
        k1 = jnp.where(hi > lo, (hi + rc - 1) // rc, k0)
        slot = jnp.sum(jnp.where(mine, pos_ref[0, rows, :], 0.0), axis=-1, keepdims=True)
        g = jnp.sum(jnp.where(mine, aff_ref[0, rows, :], 0.0), axis=-1, keepdims=True)

        def body(k, carry):
            onehot = (slot - (k * rc).astype(F32) == r).astype(BF16)
            acc_ref[rows, :] += g * _dot(onehot, y_ref[0, 0, pl.ds(pl.multiple_of(k * rc, rc), rc), :])
            return carry

        lax.fori_loop(k0, k1, body, 0)

    @pl.when(e == n_exp - 1)
    def _():
        o_ref[0] = h_ref[0] + gate_ref[0] * acc_ref[...]


def combine_tokens(y, pos, aff, offs, h, gate, T, tb):
    E, B, slots, D = y.shape
    S = h.shape[1]
    rc = min(slots, MOE_ROW_CHUNK)
    return pl.pallas_call(
        functools.partial(_combine_kernel, rc=rc, tb=tb, nblk=S // tb, n_exp=E),
        grid_spec=pltpu.PrefetchScalarGridSpec(
            num_scalar_prefetch=1,
            grid=(B, S // T, E),
            in_specs=[pl.BlockSpec((1, 1, slots, D), lambda b, t, e, offs: (e, b, 0, 0)),
                      pl.BlockSpec((1, T, LANES), lambda b, t, e, offs: (b, t, 0)),
                      pl.BlockSpec((1, T, LANES), lambda b, t, e, offs: (b, t, 0)),
                      pl.BlockSpec((1, T, D), lambda b, t, e, offs: (b, t, 0)),
                      pl.BlockSpec((1, 1, D), lambda b, t, e, offs: (b, 0, 0))],
            out_specs=pl.BlockSpec((1, T, D), lambda b, t, e, offs: (b, t, 0)),
            scratch_shapes=[pltpu.VMEM((T, D), F32)]),
        out_shape=jax.ShapeDtypeStruct((B, S, D), F32),
        compiler_params=_params(("parallel", "parallel", "arbitrary"), 48),
        name="combine_tokens",
    )(offs, y, pos, aff, h, gate)


def moe_block(h, g, scale, shift, gate, router_w, w_gate, w_up, w_down, layer):
    B, S, D = h.shape
    E = router_w.shape[1]
    cap = EC_CAPACITY * S // E
    slots = max(cap, LANES)
    x, aff = ffn_prep(h, g, scale, shift, router_w, min(S, 512))
    pos, post, offs = select_tokens(aff, cap, E)
    xs = gather_tokens(x, post, offs, slots)
    M = B * slots
    y = expert_ffn(xs.reshape(E, M, D), w_gate, w_up, w_down, layer, min(M, 1024), 512)
    return combine_tokens(y.reshape(E, B, slots, D), pos, aff, offs, h, gate, min(S, 1024), post.shape[3])


def kernel(x, c, ctx, c_ctx, ada_w, ada_b, norm_mix_g, norm_ffn_g, w_in, w_out, hy_conv_w, hy_conv_b, hy_f_w1, hy_f_b1, hy_f_w2, hy_f_b2, hy_f_w3, hy_f_b3, hy_f_freq, hy_f_wout, hy_bias, q_norm_g, k_norm_g, diff_lambda, subln_g, pool_w, pool_scale, router_w, exp_w_gate, exp_w_up, exp_w_down):
    B, S, D = x.shape
    Lc = ctx.shape[1]
    depth = ada_w.shape[0]
    hy_width = hy_bias.shape[2]
    hy_proj = (HYENA_ORDER + 1) * hy_width
    qk_width = DIFF_HEADS * 2 * DIFF_HEAD_DIM
    last_attn = ((depth - 1) // 2) * 2

    s_all = jnp.concatenate([jax.nn.silu(c), jax.nn.silu(c_ctx)[None, :]], axis=0)
    rows = -(-s_all.shape[0] // SUBLANES) * SUBLANES
    mods = ada_modulation(jnp.pad(s_all, ((0, rows - s_all.shape[0]), (0, 0))), ada_w, ada_b)

    nblk_lat = max(1, S // HYENA_BLOCK)
    tables_lat = dft_tables(S // nblk_lat)
    tables_ctx = dft_tables(Lc)
    rope_lat = rope_tables(S, True)
    rope_ctx = rope_tables(Lc, False)

    h, hc = x, ctx
    for l in range(depth):
        m = [mods[l, :B, i * D:(i + 1) * D].reshape(B, 1, D) for i in range(6)]
        mc = [jnp.broadcast_to(mods[l, B, i * D:(i + 1) * D].reshape(1, 1, D), (B, 1, D)) for i in range(6)]
        ctx_full = l < last_attn
        if l % 2 == 0:
            e = l // 2
            lam_init = 0.8 - 0.6 * math.exp(-0.3 * l)
            lv = diff_lambda[e]
            lam = jnp.exp(jnp.sum(lv[0] * lv[1])) - jnp.exp(jnp.sum(lv[2] * lv[3])) + lam_init
            filt = (hy_f_w1[e], hy_f_b1[e], hy_f_w2[e], hy_f_b2[e], hy_f_w3[e], hy_f_b3[e], hy_f_freq[e], hy_f_wout[e])
            w_in_b = w_in[e].astype(BF16)
            w_out_b = w_out[e].astype(BF16)

            p = norm_mod_matmul(h, norm_mix_g[l], m[1], m[0], w_in_b, 512)
            gts = short_conv(p, hy_conv_w[e], hy_conv_b[e], 0, 2 * hy_width, 256, F32)
            hv = short_conv(p, hy_conv_w[e], hy_conv_b[e], 2 * hy_width, hy_width, 256, BF16)
            hy = hyena_operator(gts, hv, tables_lat, hyena_filter_taps(S, *filt, hy_width), hy_bias[e], nblk_lat, 256)
            q, k, v = qkv_prep(p, 1, True, *rope_lat, q_norm_g[e], k_norm_g[e], 512)

            if l <= last_attn:
                if ctx_full:
                    pc = norm_mod_matmul(hc, norm_mix_g[l], mc[1], mc[0], w_in_b, Lc)
                    qc, kc, vc = qkv_prep(pc, 1, True, *rope_ctx, q_norm_g[e], k_norm_g[e], Lc)
                else:
                    pc = norm_mod_matmul(hc, norm_mix_g[l], mc[1], mc[0], w_in_b[:, hy_proj + qk_width:], Lc)
                    kc, vc = qkv_prep(pc, 0, False, *rope_ctx, q_norm_g[e], k_norm_g[e], Lc)
            o = diff_attention(lam, q, kc, vc, k, v, subln_g[e], 1.0 - lam_init, min(S, 512))
            h = out_proj(hy, o, w_out_b, h, m[2], 512)
            if ctx_full:
                gtc = short_conv(pc, hy_conv_w[e], hy_conv_b[e], 0, 2 * hy_width, 256, F32)
                hvc = short_conv(pc, hy_conv_w[e], hy_conv_b[e], 2 * hy_width, hy_width, 256, BF16)
                hyc = hyena_operator(gtc, hvc, tables_ctx, hyena_filter_taps(Lc, *filt, hy_width), hy_bias[e], 1, Lc)
                oc = diff_attention(lam, qc, kc, vc, None, None, subln_g[e], 1.0 - lam_init, Lc)
                hc = out_proj(hyc, oc, w_out_b, hc, mc[2], Lc)
        else:
            o_idx = l // 2
            pw = pool_w[o_idx].astype(BF16)
            h = pool_mixer(h, norm_mix_g[l], m[1], m[0], m[2], pw, pool_scale[o_idx], 512)
            if ctx_full:
                hc = pool_mixer(hc, norm_mix_g[l], mc[1], mc[0], mc[2], pw, pool_scale[o_idx], Lc)
        h = moe_block(h, norm_ffn_g[l], m[4], m[3], m[5], router_w[l], exp_w_gate, exp_w_up, exp_w_down, l)
        if ctx_full:
            hc = moe_block(hc, norm_ffn_g[l], mc[4], mc[3], mc[5], router_w[l], exp_w_gate, exp_w_up, exp_w_down, l)
    return h
```

```python
import functools
import math

import jax
import jax.numpy as jnp
from jax import lax
from jax.experimental import pallas as pl
from jax.experimental.pallas import tpu as pltpu

F32 = jnp.float32
BF16 = jnp.bfloat16
I32 = jnp.int32

NORM_EPS = 1e-6
GRID_W = 64
HYENA_ORDER = 2
FILTER_EMB = 33
FILTER_FAST_DECAY = 0.3
FILTER_SLOW_DECAY = 1.5
FILTER_TARGET = 1e-2
DIFF_HEADS = 4
DIFF_HEAD_DIM = 64
ROPE_BASE = 10000.0
POOL_WINDOWS = (2, 4, 8, 16)
EC_CAPACITY = 2
LANES = 128
SUBLANES = 8
POOL_HALO = 8
LOG2E = 1.4426950408889634
ATTN_KEY_CHUNK = 512
MOE_ROW_CHUNK = 256
GATHER_WINDOW = 10
HYENA_BLOCK = 1024


def _params(sem, vmem_mb):
    return pltpu.CompilerParams(dimension_semantics=sem, vmem_limit_bytes=vmem_mb * 1024 * 1024)


def _dot(a, b):
    return jnp.dot(a, b, preferred_element_type=F32)


def _norm_mod(x, g, scale, shift):
    ms = jnp.mean(x * x, axis=-1, keepdims=True)
    return (x * lax.rsqrt(ms + NORM_EPS) * g) * (1.0 + scale) + shift


def _mm_kernel(a_ref, b_ref, o_ref, acc_ref, *, nk):
    k = pl.program_id(2)
    part = _dot(a_ref[...].astype(BF16), b_ref[...].astype(BF16))

    @pl.when(k == 0)
    def _():
        acc_ref[...] = part

    @pl.when(k > 0)
    def _():
        acc_ref[...] += part

    @pl.when(k == nk - 1)
    def _():
        o_ref[...] = acc_ref[...].astype(o_ref.dtype)


def matmul(a, b, tm, tn, tk):
    M, K = a.shape
    N = b.shape[1]
    nk = K // tk
    return pl.pallas_call(
        functools.partial(_mm_kernel, nk=nk),
        grid=(M // tm, N // tn, nk),
        in_specs=[pl.BlockSpec((tm, tk), lambda i, j, k: (i, k)),
                  pl.BlockSpec((tk, tn), lambda i, j, k: (k, j))],
        out_specs=pl.BlockSpec((tm, tn), lambda i, j, k: (i, j)),
        out_shape=jax.ShapeDtypeStruct((M, N), F32),
        scratch_shapes=[pltpu.VMEM((tm, tn), F32)],
        compiler_params=_params(("parallel", "parallel", "arbitrary"), 40),
        name="matmul",
    )(a, b)


def _ada_kernel(s_ref, w_ref, b_ref, o_ref):
    o_ref[...] = _dot(s_ref[...].astype(BF16), w_ref[...].astype(BF16)) + b_ref[...]


def ada_modulation(s, ada_w, ada_b):
    depth, D, N = ada_w.shape
    R = s.shape[0]
    tn = 1024
    return pl.pallas_call(
        _ada_kernel,
        grid=(depth, N // tn),
        in_specs=[pl.BlockSpec((R, D), lambda l, j: (0, 0)),
                  pl.BlockSpec((None, D, tn), lambda l, j: (l, 0, j)),
                  pl.BlockSpec((None, 1, tn), lambda l, j: (l, 0, j))],
        out_specs=pl.BlockSpec((None, R, tn), lambda l, j: (l, 0, j)),
        out_shape=jax.ShapeDtypeStruct((depth, R, N), F32),
        compiler_params=_params(("parallel", "parallel"), 32),
        name="ada_modulation",
    )(s, ada_w, ada_b.reshape(depth, 1, N))


def _nmm_kernel(h_ref, g_ref, sc_ref, sh_ref, w_ref, o_ref):
    a = _norm_mod(h_ref[0], g_ref[...], sc_ref[0], sh_ref[0])
    o_ref[0] = _dot(a.astype(BF16), w_ref[...])


def norm_mod_matmul(h, g, scale, shift, w, tm):
    B, S, D = h.shape
    N = w.shape[1]
    return pl.pallas_call(
        _nmm_kernel,
        grid=(B, S // tm),
        in_specs=[pl.BlockSpec((1, tm, D), lambda b, i: (b, i, 0)),
                  pl.BlockSpec((1, D), lambda b, i: (0, 0)),
                  pl.BlockSpec((1, 1, D), lambda b, i: (b, 0, 0)),
                  pl.BlockSpec((1, 1, D), lambda b, i: (b, 0, 0)),
                  pl.BlockSpec((D, N), lambda b, i: (0, 0))],
        out_specs=pl.BlockSpec((1, tm, N), lambda b, i: (b, i, 0)),
        out_shape=jax.ShapeDtypeStruct((B, S, N), F32),
        compiler_params=_params(("parallel", "parallel"), 48),
        name="norm_mod_matmul",
    )(h, g.reshape(1, D), scale, shift, w)


def _sconv_kernel(p_ref, w_ref, b_ref, o_ref):
    x = p_ref[0]
    S = x.shape[0]
    row = lax.broadcasted_iota(I32, x.shape, 0)
    xm = jnp.where(row == 0, 0.0, pltpu.roll(x, 1, 0))
    xp = jnp.where(row == S - 1, 0.0, pltpu.roll(x, S - 1, 0))
    w = w_ref[...]
    o_ref[0] = (xm * w[0:1] + x * w[1:2] + xp * w[2:3] + b_ref[...]).astype(o_ref.dtype)


def short_conv(p, conv_w, conv_b, col0, width, tc, out_dtype):
    B, S, _ = p.shape
    c0 = col0 // tc
    return pl.pallas_call(
        _sconv_kernel,
        grid=(B, width // tc),
        in_specs=[pl.BlockSpec((1, S, tc), lambda b, c: (b, 0, c + c0)),
                  pl.BlockSpec((3, tc), lambda b, c: (0, c + c0)),
                  pl.BlockSpec((1, tc), lambda b, c: (0, c + c0))],
        out_specs=pl.BlockSpec((1, S, tc), lambda b, c: (b, 0, c)),
        out_shape=jax.ShapeDtypeStruct((B, S, width), out_dtype),
        compiler_params=_params(("parallel", "parallel"), 48),
        name="short_conv",
    )(p, conv_w, conv_b.reshape(1, -1))


def _dft_fwd_kernel(c_ref, s_ref, u_ref, k_ref, y_ref, *, nblk):
    c = c_ref[...]
    s = s_ref[...]
    ure, uim = [], []
    for j in range(nblk):
        u = u_ref[0, j]
        ure.append(_dot(c, u))
        uim.append(-_dot(s, u))
    for i in range(nblk):
        yre = yim = None
        for j in range(nblk):
            d = i - j + nblk - 1
            kre = k_ref[d, 0]
            kim = k_ref[d, 1]
            tre = kre * ure[j] - kim * uim[j]
            tim = kre * uim[j] + kim * ure[j]
            yre = tre if yre is None else yre + tre
            yim = tim if yim is None else yim + tim
        y_ref[0, i, 0] = yre.astype(BF16)
        y_ref[0, i, 1] = yim.astype(BF16)


def dft_forward(cf, sf, u, kspec, tf):
    B, nblk, Lb, C = u.shape
    nd = 2 * nblk - 1
    return pl.pallas_call(
        functools.partial(_dft_fwd_kernel, nblk=nblk),
        grid=(Lb // tf, B),
        in_specs=[pl.BlockSpec((tf, Lb), lambda f, b: (f, 0)),
                  pl.BlockSpec((tf, Lb), lambda f, b: (f, 0)),
                  pl.BlockSpec((1, nblk, Lb, C), lambda f, b: (b, 0, 0, 0)),
                  pl.BlockSpec((nd, 2, tf, C), lambda f, b: (0, 0, f, 0))],
        out_specs=pl.BlockSpec((1, nblk, 2, tf, C), lambda f, b: (b, 0, 0, f, 0)),
        out_shape=jax.ShapeDtypeStruct((B, nblk, 2, Lb, C), BF16),
        compiler_params=_params(("parallel", "parallel"), 56),
        name="dft_forward",
    )(cf, sf, u, kspec)


def _dft_inv_kernel(ct_ref, st_ref, y_ref, gate_ref, o_ref, *, scale):
    acc = _dot(ct_ref[...], y_ref[0, 0, 0]) - _dot(st_ref[...], y_ref[0, 0, 1])
    o_ref[0, 0] = (gate_ref[0, 0] * (acc * scale)).astype(BF16)


def dft_inverse(ct, st, y, gates, gate_col):
    B, nblk, _, Lb, C = y.shape
    return pl.pallas_call(
        functools.partial(_dft_inv_kernel, scale=1.0 / Lb),
        grid=(B, nblk),
        in_specs=[pl.BlockSpec((Lb, Lb), lambda b, i: (0, 0)),
                  pl.BlockSpec((Lb, Lb), lambda b, i: (0, 0)),
                  pl.BlockSpec((1, 1, 2, Lb, C), lambda b, i: (b, i, 0, 0, 0)),
                  pl.BlockSpec((1, 1, Lb, C), lambda b, i: (b, i, 0, gate_col))],
        out_specs=pl.BlockSpec((1, 1, Lb, C), lambda b, i: (b, i, 0, 0)),
        out_shape=jax.ShapeDtypeStruct((B, nblk, Lb, C), BF16),
        compiler_params=_params(("parallel", "parallel"), 48),
        name="dft_inverse",
    )(ct, st, y, gates)


def dft_tables(L):
    n = 2 * L
    f = lax.broadcasted_iota(I32, (L, L), 0)
    t = lax.broadcasted_iota(I32, (L, L), 1)
    m = ((2 * f + 1) * t) % (2 * n)
    ang = m.astype(F32) * (math.pi / n)
    cf = jnp.cos(ang)
    sf = jnp.sin(ang)
    return cf.astype(BF16), sf.astype(BF16), cf.T.astype(BF16), sf.T.astype(BF16)


def hyena_filter_taps(L, w1, b1, w2, b2, w3, b3, freq, wout, width):
    hp = lax.Precision.HIGHEST
    t = jnp.linspace(0.0, 1.0, L, dtype=F32)[:, None]
    bands = (FILTER_EMB - 1) // 2
    w = 2.0 * math.pi * jnp.arange(L, dtype=F32)[:, None] / L
    f = jnp.linspace(1e-4, bands - 1, bands, dtype=F32)[None, :]
    z = jnp.concatenate([t, jnp.cos(f * w), -jnp.sin(f * w)], axis=-1)
    h = jnp.sin(freq * (jnp.dot(z, w1, precision=hp) + b1))
    h = jnp.sin(freq * (jnp.dot(h, w2, precision=hp) + b2))
    h = jnp.sin(freq * (jnp.dot(h, w3, precision=hp) + b3))
    h = jnp.dot(h, wout, precision=hp)
    max_decay = math.log(FILTER_TARGET) / FILTER_FAST_DECAY
    min_decay = math.log(FILTER_TARGET) / FILTER_SLOW_DECAY
    deltas = jnp.abs(jnp.linspace(min_decay, max_decay, width, dtype=F32))
    decay = jnp.exp(-t * deltas[None, :])
    return h.reshape(L, HYENA_ORDER, 2, width) * decay[:, None, None, :]


def filter_spectra(cf, sf, taps, bias, nblk):
    L, C = taps.shape[0], taps.shape[3]
    Lb = L // nblk
    nd = 2 * nblk - 1
    zero = jnp.zeros((1, C), F32)
    sums, diffs = [], []
    for o in range(HYENA_ORDER):
        h_fwd, h_bwd = taps[:, o, 0], taps[:, o, 1]
        kfull = jnp.concatenate([zero, h_bwd[:0:-1], (h_fwd[0] + h_bwd[0] + bias[o])[None, :], h_fwd[1:]], axis=0)
        for dlt in range(-(nblk - 1), nblk):
            base = L + dlt * Lb
            kp = kfull[base:base + Lb]
            km = jnp.concatenate([zero, kfull[base - Lb + 1:base][::-1]], axis=0)
            sums.append(kp + km)
            diffs.append(kp - km)
    tmm = min(Lb, 512)
    kre = matmul(cf, jnp.concatenate(sums, axis=1), tmm, tmm, tmm)
    kim = -matmul(sf, jnp.concatenate(diffs, axis=1), tmm, tmm, tmm)
    shape = (Lb, HYENA_ORDER, nd, C)
    return jnp.stack([kre.reshape(shape), kim.reshape(shape)], axis=0).transpose(2, 3, 0, 1, 4)


def hyena_operator(gates, v, tables, taps, bias, nblk, tf):
    cf, sf, ct, st = tables
    B, L, C = v.shape
    Lb = L // nblk
    kspec = filter_spectra(cf, sf, taps, bias, nblk)
    gates = gates.reshape(B, nblk, Lb, 2 * C)
    z = v.reshape(B, nblk, Lb, C)
    for o in range(HYENA_ORDER):
        y = dft_forward(cf, sf, z, kspec[o], tf)
        z = dft_inverse(ct, st, y, gates, o)
    return z.reshape(B, L, C)


def _head_norm_rope(x, g, cos, sin_signed, scale):
    lane = lax.broadcasted_iota(I32, x.shape, 1)
    lo = lane < DIFF_HEAD_DIM
    x2 = x * x
    s_lo = jnp.sum(jnp.where(lo, x2, 0.0), axis=-1, keepdims=True)
    s_hi = jnp.sum(jnp.where(lo, 0.0, x2), axis=-1, keepdims=True)
    ms = jnp.where(lo, s_lo, s_hi) * (1.0 / DIFF_HEAD_DIM)
    xn = x * lax.rsqrt(ms + NORM_EPS) * g
    first = (lane & 16) == 0
    partner = jnp.where(first, pltpu.roll(xn, LANES - 16, 1), pltpu.roll(xn, 16, 1))
    return (xn * cos + partner * sin_signed) * scale


def _qkv_kernel(p_ref, cos_ref, sin_ref, qg_ref, kg_ref, *o_refs, has_q, width):
    cos = cos_ref[...]
    sin = sin_ref[...]
    x = p_ref[0]
    col = 0
    outs = list(o_refs)
    if has_q:
        q_ref = outs.pop(0)
        for hd in range(width // LANES):
            sl = slice(col + hd * LANES, col + (hd + 1) * LANES)
            q_ref[0, :, hd * LANES:(hd + 1) * LANES] = _head_norm_rope(
                x[:, sl], qg_ref[...], cos, sin, LOG2E * DIFF_HEAD_DIM ** -0.5).astype(BF16)
        col += width
    k_ref, v_ref = outs
    for hd in range(width // LANES):
        sl = slice(col + hd * LANES, col + (hd + 1) * LANES)
        k_ref[0, :, hd * LANES:(hd + 1) * LANES] = _head_norm_rope(x[:, sl], kg_ref[...], cos, sin, 1.0).astype(BF16)
    col += width
    v_ref[0] = x[:, col:col + width].astype(BF16)


def qkv_prep(p, col_block, has_q, cos, sin_signed, q_g, k_g, tm):
    B, S, _ = p.shape
    width = DIFF_HEADS * 2 * DIFF_HEAD_DIM
    n_out = 3 if has_q else 2
    g2 = lambda g: jnp.concatenate([g, g]).reshape(1, LANES)
    outs = pl.pallas_call(
        functools.partial(_qkv_kernel, has_q=has_q, width=width),
        grid=(B, S // tm),
        in_specs=[pl.BlockSpec((1, tm, n_out * width), lambda b, i: (b, i, col_block)),
                  pl.BlockSpec((tm, LANES), lambda b, i: (i, 0)),
                  pl.BlockSpec((tm, LANES), lambda b, i: (i, 0)),
                  pl.BlockSpec((1, LANES), lambda b, i: (0, 0)),
                  pl.BlockSpec((1, LANES), lambda b, i: (0, 0))],
        out_specs=[pl.BlockSpec((1, tm, width), lambda b, i: (b, i, 0))] * n_out,
        out_shape=[jax.ShapeDtypeStruct((B, S, width), BF16)] * n_out,
        compiler_params=_params(("parallel", "parallel"), 32),
        name="qkv_prep",
    )(p, cos, sin_signed, g2(q_g), g2(k_g))
    return outs


def rope_tables(S, use_rope):
    if not use_rope:
        return jnp.ones((S, LANES), F32), jnp.zeros((S, LANES), F32)
    t = jnp.arange(S, dtype=I32)
    row = (t // GRID_W).astype(F32)[:, None]
    colp = (t % GRID_W).astype(F32)[:, None]
    nf = DIFF_HEAD_DIM // 4
    inv = ROPE_BASE ** (-jnp.arange(nf, dtype=F32) / nf)
    lane = jnp.arange(LANES)
    grp = (lane % DIFF_HEAD_DIM) // nf
    j = lane % nf
    pos = jnp.where((grp < 2)[None, :], row, colp)
    ang = pos * inv[j][None, :]
    sign = jnp.where((grp % 2 == 0)[None, :], -1.0, 1.0)
    return jnp.cos(ang), jnp.sin(ang) * sign


def _attn_kernel(lam_ref, q_ref, kc_ref, vc_ref, *rest, has_lat, out_scale, ck):
    if has_lat:
        kl_ref, vl_ref, g_ref, o_ref = rest
    else:
        g_ref, o_ref = rest
    lam = lam_ref[0]
    q = q_ref[0]
    lane = lax.broadcasted_iota(I32, q.shape, 1)
    nt = (((1,), (1,)), ((), ()))
    zero = jnp.zeros_like(q)
    qm = [jnp.where(lane < DIFF_HEAD_DIM, q, zero), jnp.where(lane >= DIFF_HEAD_DIM, q, zero)]
    chunks = [(kc_ref, vc_ref, 0, kc_ref.shape[1])]
    if has_lat:
        chunks += [(kl_ref, vl_ref, c * ck, ck) for c in range(kl_ref.shape[1] // ck)]

    def scores(ch):
        k = ch[0][0, ch[2]:ch[2] + ch[3], :]
        return [lax.dot_general(qm[mp], k, nt, preferred_element_type=F32) for mp in range(2)]

    m, den, acc = [None, None], [None, None], [None, None]
    s_next = scores(chunks[0])
    for ci, ch in enumerate(chunks):
        s_cur = s_next
        if ci + 1 < len(chunks):
            s_next = scores(chunks[ci + 1])
        v = ch[1][0, ch[2]:ch[2] + ch[3], :]
        for mp in range(2):
            s = s_cur[mp]
            mx = jnp.max(s, axis=-1, keepdims=True)
            if ci == 0:
                p = jnp.exp2(s - mx)
                m[mp], den[mp], acc[mp] = mx, jnp.sum(p, axis=-1, keepdims=True), _dot(p.astype(BF16), v)
            else:
                m_new = jnp.maximum(m[mp], mx)
                alpha = jnp.exp2(m[mp] - m_new)
                p = jnp.exp2(s - m_new)
                den[mp] = alpha * den[mp] + jnp.sum(p, axis=-1, keepdims=True)
                acc[mp] = alpha * acc[mp] + _dot(p.astype(BF16), v)
                m[mp] = m_new
    o = acc[0] * (1.0 / den[0]) - lam * (acc[1] * (1.0 / den[1]))
    ms = jnp.mean(o * o, axis=-1, keepdims=True)
    o_ref[0] = ((o * lax.rsqrt(ms + NORM_EPS) * g_ref[...]) * out_scale).astype(BF16)


def diff_attention(lam, q, k_ctx, v_ctx, k_lat, v_lat, subln_g, out_scale, tq):
    B, Sq, W = q.shape
    H = W // LANES
    has_lat = k_lat is not None
    Sc = k_ctx.shape[1]
    head = lambda S: pl.BlockSpec((1, S, LANES), lambda b, h, i: (b, 0, h))
    in_specs = [pl.BlockSpec(memory_space=pltpu.SMEM),
                pl.BlockSpec((1, tq, LANES), lambda b, h, i: (b, i, h)), head(Sc), head(Sc)]
    args = [lam.reshape(1), q, k_ctx, v_ctx]
    if has_lat:
        in_specs += [head(k_lat.shape[1]), head(k_lat.shape[1])]
        args += [k_lat, v_lat]
    in_specs.append(pl.BlockSpec((1, LANES), lambda b, h, i: (0, 0)))
    args.append(subln_g.reshape(1, LANES))
    return pl.pallas_call(
        functools.partial(_attn_kernel, has_lat=has_lat, out_scale=out_scale, ck=ATTN_KEY_CHUNK),
        grid=(B, H, Sq // tq),
        in_specs=in_specs,
        out_specs=pl.BlockSpec((1, tq, LANES), lambda b, h, i: (b, i, h)),
        out_shape=jax.ShapeDtypeStruct((B, Sq, W), BF16),
        compiler_params=_params(("parallel", "parallel", "parallel"), 56),
        name="diff_attention",
    )(*args)


def _oproj_kernel(hy_ref, o_ref, w_ref, h_ref, gate_ref, out_ref, *, half):
    y = _dot(hy_ref[0].astype(BF16), w_ref[0:half, :]) + _dot(o_ref[0], w_ref[half:, :])
    out_ref[0] = h_ref[0] + gate_ref[0] * y


def out_proj(hy, o, w_out, h, gate, tm):
    B, S, D = h.shape
    half = hy.shape[2]
    return pl.pallas_call(
        functools.partial(_oproj_kernel, half=half),
        grid=(B, S // tm),
        in_specs=[pl.BlockSpec((1, tm, half), lambda b, i: (b, i, 0)),
                  pl.BlockSpec((1, tm, half), lambda b, i: (b, i, 0)),
                  pl.BlockSpec((2 * half, D), lambda b, i: (0, 0)),
                  pl.BlockSpec((1, tm, D), lambda b, i: (b, i, 0)),
                  pl.BlockSpec((1, 1, D), lambda b, i: (b, 0, 0))],
        out_specs=pl.BlockSpec((1, tm, D), lambda b, i: (b, i, 0)),
        out_shape=jax.ShapeDtypeStruct((B, S, D), F32),
        compiler_params=_params(("parallel", "parallel"), 40),
        name="out_proj",
    )(hy, o, w_out, h, gate)


def _shift_rows(x, d):
    return pltpu.roll(x, (-d) % x.shape[0], 0)


def _pool_kernel(hp_ref, hc_ref, hn_ref, g_ref, sc_ref, sh_ref, gate_ref, pw_ref, ps_ref, o_ref, *, T, L):
    i = pl.program_id(1)
    nt = pl.num_programs(1)
    g, sc, sh = g_ref[...], sc_ref[0], sh_ref[0]
    hc = hc_ref[0]
    a_c = _norm_mod(hc, g, sc, sh)
    a_p = jnp.where(i == 0, 0.0, _norm_mod(hp_ref[0], g, sc, sh))
    a_n = jnp.where(i == nt - 1, 0.0, _norm_mod(hn_ref[0], g, sc, sh))
    ext = jnp.concatenate([a_p, a_c, a_n], axis=0)
    tok = i * T + lax.broadcasted_iota(I32, (T, 1), 0)
    G = ext.shape[1] // len(POOL_WINDOWS)
    ys = []
    for gi, w in enumerate(POOL_WINDOWS):
        xg = ext[:, gi * G:(gi + 1) * G]
        s = _shift_rows(xg, -1) + xg
        step = 1
        while 2 * step < w:
            s = _shift_rows(s, -step) + _shift_rows(s, step)
            step *= 2
        cnt = (jnp.minimum(tok + w // 2, L) - jnp.maximum(tok - w // 2, 0)).astype(F32)
        p = s[POOL_HALO:POOL_HALO + T] / cnt - a_c[:, gi * G:(gi + 1) * G]
        ys.append(_dot(p.astype(BF16), pw_ref[gi]))
    y = jnp.concatenate(ys, axis=1) * ps_ref[...]
    o_ref[0] = hc + gate_ref[0] * y


def pool_mixer(h, g, scale, shift, gate, pool_w, pool_scale, T):
    B, S, D = h.shape
    nh = T // POOL_HALO
    last = S // POOL_HALO - 1
    mod = pl.BlockSpec((1, 1, D), lambda b, i: (b, 0, 0))
    return pl.pallas_call(
        functools.partial(_pool_kernel, T=T, L=S),
        grid=(B, S // T),
        in_specs=[pl.BlockSpec((1, POOL_HALO, D), lambda b, i: (b, jnp.maximum(i * nh - 1, 0), 0)),
                  pl.BlockSpec((1, T, D), lambda b, i: (b, i, 0)),
                  pl.BlockSpec((1, POOL_HALO, D), lambda b, i: (b, jnp.minimum((i + 1) * nh, last), 0)),
                  pl.BlockSpec((1, D), lambda b, i: (0, 0)), mod, mod, mod,
                  pl.BlockSpec(pool_w.shape, lambda b, i: (0, 0, 0)),
                  pl.BlockSpec((1, D), lambda b, i: (0, 0))],
        out_specs=pl.BlockSpec((1, T, D), lambda b, i: (b, i, 0)),
        out_shape=jax.ShapeDtypeStruct((B, S, D), F32),
        compiler_params=_params(("parallel", "parallel"), 48),
        name="pool_mixer",
    )(h, h, h, g.reshape(1, D), scale, shift, gate, pool_w, pool_scale.reshape(1, D))


def _ffn_prep_kernel(h_ref, g_ref, sc_ref, sh_ref, rw_ref, x_ref, aff_ref, *, n_exp):
    a = _norm_mod(h_ref[0], g_ref[...], sc_ref[0], sh_ref[0]).astype(BF16)
    x_ref[0] = a
    logits = _dot(a, rw_ref[...])
    lane = lax.broadcasted_iota(I32, logits.shape, 1)
    valid = lane < n_exp
    mx = jnp.max(jnp.where(valid, logits, -jnp.inf), axis=-1, keepdims=True)
    e = jnp.where(valid, jnp.exp(logits - mx), 0.0)
    aff_ref[0] = e / jnp.sum(e, axis=-1, keepdims=True)


def ffn_prep(h, g, scale, shift, router_w, tm):
    B, S, D = h.shape
    n_exp = router_w.shape[1]
    rw = jnp.pad(router_w, ((0, 0), (0, LANES - n_exp))).astype(BF16)
    mod = pl.BlockSpec((1, 1, D), lambda b, i: (b, 0, 0))
    return pl.pallas_call(
        functools.partial(_ffn_prep_kernel, n_exp=n_exp),
        grid=(B, S // tm),
        in_specs=[pl.BlockSpec((1, tm, D), lambda b, i: (b, i, 0)),
                  pl.BlockSpec((1, D), lambda b, i: (0, 0)), mod, mod,
                  pl.BlockSpec((D, LANES), lambda b, i: (0, 0))],
        out_specs=[pl.BlockSpec((1, tm, D), lambda b, i: (b, i, 0)),
                   pl.BlockSpec((1, tm, LANES), lambda b, i: (b, i, 0))],
        out_shape=[jax.ShapeDtypeStruct((B, S, D), BF16), jax.ShapeDtypeStruct((B, S, LANES), F32)],
        compiler_params=_params(("parallel", "parallel"), 32),
        name="ffn_prep",
    )(h, g.reshape(1, D), scale, shift, rw)


def _prefix_excl(m, tri, tb):
    S = m.shape[0]
    carry = jnp.zeros((1, m.shape[1]), F32)
    outs, carries = [], []
    for blk in range(S // tb):
        mb = m[blk * tb:(blk + 1) * tb]
        outs.append(_dot(tri, mb.astype(BF16)) + carry)
        carries.append(carry)
        carry = carry + jnp.sum(mb, axis=0, keepdims=True)
    carries.append(carry)
    return (jnp.concatenate(outs, axis=0) if len(outs) > 1 else outs[0]), jnp.concatenate(carries, axis=0)


def _select_kernel(aff_ref, tri_ref, pos_ref, post_ref, offs_ref, *, cap, tb, n_exp):
    bits = pltpu.bitcast(aff_ref[0], I32)

    def body(i, cur):
        cand = cur | (jnp.int32(1) << (30 - i))
        cnt = jnp.sum((bits >= cand).astype(F32), axis=0, keepdims=True)
        return jnp.where(cnt >= cap, cand, cur)

    thr = lax.fori_loop(0, 31, body, jnp.zeros((1, bits.shape[1]), I32))
    tri = tri_ref[...]
    gt = (bits > thr).astype(F32)
    eq = (bits == thr).astype(F32)
    need = cap - jnp.sum(gt, axis=0, keepdims=True)
    sel = gt + eq * (_prefix_excl(eq, tri, tb)[0] < need).astype(F32)
    slot, offs = _prefix_excl(sel, tri, tb)
    sp = jnp.where(sel > 0.0, slot, -1.0)
    pos_ref[0] = sp
    offs_ref[0] = offs
    for blk in range(sp.shape[0] // tb):
        post_ref[0, :, blk * tb:(blk + 1) * tb] = sp[blk * tb:(blk + 1) * tb].T[0:n_exp]


def select_tokens(aff, cap, n_exp):
    B, S, _ = aff.shape
    tb = min(S, 256)
    r = lax.broadcasted_iota(I32, (tb, tb), 0)
    c = lax.broadcasted_iota(I32, (tb, tb), 1)
    tri = (c < r).astype(BF16)
    nblk = S // tb
    pos, post, offs = pl.pallas_call(
        functools.partial(_select_kernel, cap=cap, tb=tb, n_exp=n_exp),
        grid=(B,),
        in_specs=[pl.BlockSpec((1, S, LANES), lambda b: (b, 0, 0)),
                  pl.BlockSpec((tb, tb), lambda b: (0, 0))],
        out_specs=[pl.BlockSpec((1, S, LANES), lambda b: (b, 0, 0)),
                   pl.BlockSpec((1, n_exp, S), lambda b: (b, 0, 0)),
                   pl.BlockSpec((1, nblk + 1, LANES), lambda b: (b, 0, 0))],
        out_shape=[jax.ShapeDtypeStruct((B, S, LANES), F32), jax.ShapeDtypeStruct((B, n_exp, S), F32),
                   jax.ShapeDtypeStruct((B, nblk + 1, LANES), F32)],
        compiler_params=_params(("parallel",), 48),
        name="select_tokens",
    )(aff, tri)
    offs = jnp.swapaxes(offs[:, :, :n_exp], 1, 2).astype(I32).reshape(-1)
    return pos, post, offs, tb


def _gather_kernel(offs_ref, x_ref, post_ref, post4_ref, o_ref, *, slots, rc, tb, nblk, n_exp, win):
    b = pl.program_id(0)
    e = pl.program_id(1)
    S = x_ref.shape[1]
    base = (b * n_exp + e) * (nblk + 1)
    for k in range(slots // rc):
        rows = slice(k * rc, (k + 1) * rc)

        def full():
            r = lax.broadcasted_iota(I32, (rc, S), 0).astype(F32) + float(k * rc)
            onehot = (r == post_ref[0, pl.ds(e, 1), :]).astype(BF16)
            o_ref[0, 0, rows, :] = _dot(onehot, x_ref[0]).astype(BF16)

        if win >= S:
            full()
            continue
        i0 = jnp.int32(0)
        i1 = jnp.int32(0)
        for i in range(nblk):
            i0 += (offs_ref[base + i + 1] <= k * rc).astype(I32)
            i1 += (offs_ref[base + i] < (k + 1) * rc).astype(I32)
        narrow = (i1 - i0) * tb <= win
        wb = win // tb

        @pl.when(narrow)
        def _():
            j0 = jnp.minimum(i0, nblk - wb)
            r = lax.broadcasted_iota(I32, (rc, tb), 0).astype(F32) + float(k * rc)
            onehot = jnp.concatenate(
                [(r == post4_ref[0, pl.ds(e, 1), pl.ds(j0 + j, 1), :].reshape(1, tb)).astype(BF16) for j in range(wb)],
                axis=1)
            xw = x_ref[0, pl.ds(pl.multiple_of(j0 * tb, tb), win), :]
            o_ref[0, 0, rows, :] = _dot(onehot, xw).astype(BF16)

        pl.when(jnp.logical_not(narrow))(full)


def gather_tokens(x, post, offs, slots, tb):
    B, S, D = x.shape
    E = post.shape[1]
    nblk = S // tb
    rc = min(slots, MOE_ROW_CHUNK)
    win = min(S, GATHER_WINDOW * rc)
    return pl.pallas_call(
        functools.partial(_gather_kernel, slots=slots, rc=rc, tb=tb, nblk=nblk, n_exp=E, win=win),
        grid_spec=pltpu.PrefetchScalarGridSpec(
            num_scalar_prefetch=1,
            grid=(B, E),
            in_specs=[pl.BlockSpec((1, S, D), lambda b, e, offs: (b, 0, 0)),
                      pl.BlockSpec((1, E, S), lambda b, e, offs: (b, 0, 0)),
                      pl.BlockSpec((1, E, nblk, tb), lambda b, e, offs: (b, 0, 0, 0))],
            out_specs=pl.BlockSpec((1, 1, slots, D), lambda b, e, offs: (e, b, 0, 0))),
        out_shape=jax.ShapeDtypeStruct((E, B, slots, D), BF16),
        compiler_params=_params(("parallel", "arbitrary"), 48),
        name="gather_tokens",
    )(offs, x, post, post.reshape(B, E, nblk, tb))


def _ffn_kernel(xs_ref, wg_ref, wu_ref, wd_ref, o_ref, acc_ref, *, nf):
    f = pl.program_id(2)
    x = xs_ref[0]
    a = _dot(x, wg_ref[...].astype(BF16))
    u = _dot(x, wu_ref[...].astype(BF16))
    hm = (a * jax.nn.sigmoid(a) * u).astype(BF16)
    part = _dot(hm, wd_ref[...].astype(BF16))

    @pl.when(f == 0)
    def _():
        acc_ref[...] = part

    @pl.when(f > 0)
    def _():
        acc_ref[...] += part

    @pl.when(f == nf - 1)
    def _():
        o_ref[0] = acc_ref[...].astype(BF16)


def expert_ffn(xs, w_gate, w_up, w_down, layer, tm, tf):
    E, M, D = xs.shape
    F = w_gate.shape[3]
    nf = F // tf
    return pl.pallas_call(
        functools.partial(_ffn_kernel, nf=nf),
        grid=(E, M // tm, nf),
        in_specs=[pl.BlockSpec((1, tm, D), lambda e, m, f: (e, m, 0)),
                  pl.BlockSpec((None, None, D, tf), lambda e, m, f: (layer, e, 0, f)),
                  pl.BlockSpec((None, None, D, tf), lambda e, m, f: (layer, e, 0, f)),
                  pl.BlockSpec((None, None, tf, D), lambda e, m, f: (layer, e, f, 0))],
        out_specs=pl.BlockSpec((1, tm, D), lambda e, m, f: (e, m, 0)),
        out_shape=jax.ShapeDtypeStruct((E, M, D), BF16),
        scratch_shapes=[pltpu.VMEM((tm, D), F32)],
        compiler_params=_params(("parallel", "parallel", "arbitrary"), 56),
        name="expert_ffn",
    )(xs, w_gate, w_up, w_down)


def _combine_kernel(offs_ref, y_ref, pos_ref, aff_ref, h_ref, gate_ref, o_ref, acc_ref, *, tb, nblk, n_exp, win):
    b = pl.program_id(0)
    t = pl.program_id(1)
    T = pos_ref.shape[1]
    slots = y_ref.shape[2]
    per_tile = T // tb
    acc_ref[...] = jnp.zeros_like(acc_ref)
    lane = lax.broadcasted_iota(I32, (T, LANES), 1)

    def expert(e, carry):
        mine = lane == e
        slot = jnp.sum(jnp.where(mine, pos_ref[0], 0.0), axis=-1, keepdims=True)
        g = jnp.sum(jnp.where(mine, aff_ref[0], 0.0), axis=-1, keepdims=True)

        def full():
            r = lax.broadcasted_iota(I32, (T, slots), 1).astype(F32)
            acc_ref[...] += g * _dot((slot == r).astype(BF16), y_ref[e, 0])

        if win >= slots:
            full()
            return carry
        base = (b * n_exp + e) * (nblk + 1) + t * per_tile
        lo = offs_ref[base]
        hi = offs_ref[base + per_tile]
        a0 = jnp.minimum((lo // LANES) * LANES, slots - win)
        narrow = hi - a0 <= win

        @pl.when(narrow)
        def _():
            r = lax.broadcasted_iota(I32, (T, win), 1).astype(F32) + a0.astype(F32)
            yw = y_ref[e, 0, pl.ds(pl.multiple_of(a0, LANES), win), :]
            acc_ref[...] += g * _dot((slot == r).astype(BF16), yw)

        pl.when(jnp.logical_not(narrow))(full)
        return carry

    lax.fori_loop(0, n_exp, expert, 0)
    o_ref[0] = h_ref[0] + gate_ref[0] * acc_ref[...]


def combine_tokens(y, pos, aff, offs, h, gate, T, tb):
    E, B, slots, D = y.shape
    S = h.shape[1]
    win = min(slots, MOE_ROW_CHUNK)
    return pl.pallas_call(
        functools.partial(_combine_kernel, tb=tb, nblk=S // tb, n_exp=E, win=win),
        grid_spec=pltpu.PrefetchScalarGridSpec(
            num_scalar_prefetch=1,
            grid=(B, S // T),
            in_specs=[pl.BlockSpec((E, 1, slots, D), lambda b, t, offs: (0, b, 0, 0)),
                      pl.BlockSpec((1, T, LANES), lambda b, t, offs: (b, t, 0)),
                      pl.BlockSpec((1, T, LANES), lambda b, t, offs: (b, t, 0)),
                      pl.BlockSpec((1, T, D), lambda b, t, offs: (b, t, 0)),
                      pl.BlockSpec((1, 1, D), lambda b, t, offs: (b, 0, 0))],
            out_specs=pl.BlockSpec((1, T, D), lambda b, t, offs: (b, t, 0)),
            scratch_shapes=[pltpu.VMEM((T, D), F32)]),
        out_shape=jax.ShapeDtypeStruct((B, S, D), F32),
        compiler_params=_params(("parallel", "parallel"), 56),
        name="combine_tokens",
    )(offs, y, pos, aff, h, gate)


def moe_block(h, g, scale, shift, gate, router_w, w_gate, w_up, w_down, layer):
    B, S, D = h.shape
    E = router_w.shape[1]
    cap = EC_CAPACITY * S // E
    slots = max(cap, LANES)
    x, aff = ffn_prep(h, g, scale, shift, router_w, min(S, 512))
    pos, post, offs, tb = select_tokens(aff, cap, E)
    xs = gather_tokens(x, post, offs, slots, tb)
    M = B * slots
    y = expert_ffn(xs.reshape(E, M, D), w_gate, w_up, w_down, layer, min(M, 1024), 512)
    return combine_tokens(y.reshape(E, B, slots, D), pos, aff, offs, h, gate, min(S, 512), tb)


def kernel(x, c, ctx, c_ctx, ada_w, ada_b, norm_mix_g, norm_ffn_g, w_in, w_out, hy_conv_w, hy_conv_b, hy_f_w1, hy_f_b1, hy_f_w2, hy_f_b2, hy_f_w3, hy_f_b3, hy_f_freq, hy_f_wout, hy_bias, q_norm_g, k_norm_g, diff_lambda, subln_g, pool_w, pool_scale, router_w, exp_w_gate, exp_w_up, exp_w_down):
    B, S, D = x.shape
    Lc = ctx.shape[1]
    depth = ada_w.shape[0]
    hy_width = hy_bias.shape[2]
    hy_proj = (HYENA_ORDER + 1) * hy_width
    qk_width = DIFF_HEADS * 2 * DIFF_HEAD_DIM
    last_attn = ((depth - 1) // 2) * 2

    s_all = jnp.concatenate([jax.nn.silu(c), jax.nn.silu(c_ctx)[None, :]], axis=0)
    rows = -(-s_all.shape[0] // SUBLANES) * SUBLANES
    mods = ada_modulation(jnp.pad(s_all, ((0, rows - s_all.shape[0]), (0, 0))), ada_w, ada_b)

    nblk_lat = max(1, S // HYENA_BLOCK)
    tables_lat = dft_tables(S // nblk_lat)
    tables_ctx = dft_tables(Lc)
    rope_lat = rope_tables(S, True)
    rope_ctx = rope_tables(Lc, False)

    h, hc = x, ctx
    for l in range(depth):
        m = [mods[l, :B, i * D:(i + 1) * D].reshape(B, 1, D) for i in range(6)]
        mc = [jnp.broadcast_to(mods[l, B, i * D:(i + 1) * D].reshape(1, 1, D), (B, 1, D)) for i in range(6)]
        ctx_full = l < last_attn
        if l % 2 == 0:
            e = l // 2
            lam_init = 0.8 - 0.6 * math.exp(-0.3 * l)
            lv = diff_lambda[e]
            lam = jnp.exp(jnp.sum(lv[0] * lv[1])) - jnp.exp(jnp.sum(lv[2] * lv[3])) + lam_init
            filt = (hy_f_w1[e], hy_f_b1[e], hy_f_w2[e], hy_f_b2[e], hy_f_w3[e], hy_f_b3[e], hy_f_freq[e], hy_f_wout[e])
            w_in_b = w_in[e].astype(BF16)
            w_out_b = w_out[e].astype(BF16)

            p = norm_mod_matmul(h, norm_mix_g[l], m[1], m[0], w_in_b, 512)
            gts = short_conv(p, hy_conv_w[e], hy_conv_b[e], 0, 2 * hy_width, 256, F32)
            hv = short_conv(p, hy_conv_w[e], hy_conv_b[e], 2 * hy_width, hy_width, 256, BF16)
            hy = hyena_operator(gts, hv, tables_lat, hyena_filter_taps(S, *filt, hy_width), hy_bias[e], nblk_lat, 256)
            q, k, v = qkv_prep(p, 1, True, *rope_lat, q_norm_g[e], k_norm_g[e], 512)

            if l <= last_attn:
                if ctx_full:
                    pc = norm_mod_matmul(hc, norm_mix_g[l], mc[1], mc[0], w_in_b, Lc)
                    qc, kc, vc = qkv_prep(pc, 1, True, *rope_ctx, q_norm_g[e], k_norm_g[e], Lc)
                else:
                    pc = norm_mod_matmul(hc, norm_mix_g[l], mc[1], mc[0], w_in_b[:, hy_proj + qk_width:], Lc)
                    kc, vc = qkv_prep(pc, 0, False, *rope_ctx, q_norm_g[e], k_norm_g[e], Lc)
            o = diff_attention(lam, q, kc, vc, k, v, subln_g[e], 1.0 - lam_init, min(S, 512))
            h = out_proj(hy, o, w_out_b, h, m[2], 512)
            if ctx_full:
                gtc = short_conv(pc, hy_conv_w[e], hy_conv_b[e], 0, 2 * hy_width, 256, F32)
                hvc = short_conv(pc, hy_conv_w[e], hy_conv_b[e], 2 * hy_width, hy_width, 256, BF16)
                hyc = hyena_operator(gtc, hvc, tables_ctx, hyena_filter_taps(Lc, *filt, hy_width), hy_bias[e], 1, Lc)
                oc = diff_attention(lam, qc, kc, vc, None, None, subln_g[e], 1.0 - lam_init, Lc)
                hc = out_proj(hyc, oc, w_out_b, hc, mc[2], Lc)
        else:
            o_idx = l // 2
            pw = pool_w[o_idx].astype(BF16)
            h = pool_mixer(h, norm_mix_g[l], m[1], m[0], m[2], pw, pool_scale[o_idx], 512)
            if ctx_full:
                hc = pool_mixer(hc, norm_mix_g[l], mc[1], mc[0], mc[2], pw, pool_scale[o_idx], Lc)
        h = moe_block(h, norm_ffn_g[l], m[4], m[3], m[5], router_w[l], exp_w_gate, exp_w_up, exp_w_down, l)
        if ctx_full:
            hc = moe_block(hc, norm_ffn_g[l], mc[4], mc[3], mc[5], router_w[l], exp_w_gate, exp_w_up, exp_w_down, l)
    return h
```

```python
import functools
import math

import jax
import jax.numpy as jnp
from jax import lax
from jax.experimental import pallas as pl
from jax.experimental.pallas import tpu as pltpu

F32 = jnp.float32
BF16 = jnp.bfloat16
I32 = jnp.int32

NORM_EPS = 1e-6
GRID_W = 64
HYENA_ORDER = 2
FILTER_EMB = 33
FILTER_FAST_DECAY = 0.3
FILTER_SLOW_DECAY = 1.5
FILTER_TARGET = 1e-2
DIFF_HEADS = 4
DIFF_HEAD_DIM = 64
ROPE_BASE = 10000.0
POOL_WINDOWS = (2, 4, 8, 16)
EC_CAPACITY = 2
LANES = 128
SUBLANES = 8
POOL_HALO = 8
LOG2E = 1.4426950408889634
ATTN_KEY_CHUNK = 512
MOE_ROW_CHUNK = 256
GATHER_WINDOW = 10
COMBINE_WINDOW = 128
BF16_ROWS = 16
HYENA_BLOCK = 1024


def _params(sem, vmem_mb):
    return pltpu.CompilerParams(dimension_semantics=sem, vmem_limit_bytes=vmem_mb * 1024 * 1024)


def _dot(a, b):
    return jnp.dot(a, b, preferred_element_type=F32)


def _norm_mod(x, g, scale, shift):
    ms = jnp.mean(x * x, axis=-1, keepdims=True)
    return (x * lax.rsqrt(ms + NORM_EPS) * g) * (1.0 + scale) + shift


def _mm_kernel(a_ref, b_ref, o_ref, acc_ref, *, nk):
    k = pl.program_id(2)
    part = _dot(a_ref[...].astype(BF16), b_ref[...].astype(BF16))

    @pl.when(k == 0)
    def _():
        acc_ref[...] = part

    @pl.when(k > 0)
    def _():
        acc_ref[...] += part

    @pl.when(k == nk - 1)
    def _():
        o_ref[...] = acc_ref[...].astype(o_ref.dtype)


def matmul(a, b, tm, tn, tk):
    M, K = a.shape
    N = b.shape[1]
    nk = K // tk
    return pl.pallas_call(
        functools.partial(_mm_kernel, nk=nk),
        grid=(M // tm, N // tn, nk),
        in_specs=[pl.BlockSpec((tm, tk), lambda i, j, k: (i, k)),
                  pl.BlockSpec((tk, tn), lambda i, j, k: (k, j))],
        out_specs=pl.BlockSpec((tm, tn), lambda i, j, k: (i, j)),
        out_shape=jax.ShapeDtypeStruct((M, N), F32),
        scratch_shapes=[pltpu.VMEM((tm, tn), F32)],
        compiler_params=_params(("parallel", "parallel", "arbitrary"), 40),
        name="matmul",
    )(a, b)


def _ada_kernel(s_ref, w_ref, b_ref, o_ref):
    o_ref[...] = _dot(s_ref[...].astype(BF16), w_ref[...].astype(BF16)) + b_ref[...]


def ada_modulation(s, ada_w, ada_b):
    depth, D, N = ada_w.shape
    R = s.shape[0]
    tn = 1024
    return pl.pallas_call(
        _ada_kernel,
        grid=(depth, N // tn),
        in_specs=[pl.BlockSpec((R, D), lambda l, j: (0, 0)),
                  pl.BlockSpec((None, D, tn), lambda l, j: (l, 0, j)),
                  pl.BlockSpec((None, 1, tn), lambda l, j: (l, 0, j))],
        out_specs=pl.BlockSpec((None, R, tn), lambda l, j: (l, 0, j)),
        out_shape=jax.ShapeDtypeStruct((depth, R, N), F32),
        compiler_params=_params(("parallel", "parallel"), 32),
        name="ada_modulation",
    )(s, ada_w, ada_b.reshape(depth, 1, N))


def _nmm_kernel(h_ref, g_ref, sc_ref, sh_ref, w_ref, o_ref):
    a = _norm_mod(h_ref[0], g_ref[...], sc_ref[0], sh_ref[0])
    o_ref[0] = _dot(a.astype(BF16), w_ref[...])


def norm_mod_matmul(h, g, scale, shift, w, tm):
    B, S, D = h.shape
    N = w.shape[1]
    return pl.pallas_call(
        _nmm_kernel,
        grid=(B, S // tm),
        in_specs=[pl.BlockSpec((1, tm, D), lambda b, i: (b, i, 0)),
                  pl.BlockSpec((1, D), lambda b, i: (0, 0)),
                  pl.BlockSpec((1, 1, D), lambda b, i: (b, 0, 0)),
                  pl.BlockSpec((1, 1, D), lambda b, i: (b, 0, 0)),
                  pl.BlockSpec((D, N), lambda b, i: (0, 0))],
        out_specs=pl.BlockSpec((1, tm, N), lambda b, i: (b, i, 0)),
        out_shape=jax.ShapeDtypeStruct((B, S, N), F32),
        compiler_params=_params(("parallel", "parallel"), 48),
        name="norm_mod_matmul",
    )(h, g.reshape(1, D), scale, shift, w)


def _sconv_kernel(p_ref, w_ref, b_ref, o_ref):
    x = p_ref[0]
    S = x.shape[0]
    row = lax.broadcasted_iota(I32, x.shape, 0)
    xm = jnp.where(row == 0, 0.0, pltpu.roll(x, 1, 0))
    xp = jnp.where(row == S - 1, 0.0, pltpu.roll(x, S - 1, 0))
    w = w_ref[...]
    o_ref[0] = (xm * w[0:1] + x * w[1:2] + xp * w[2:3] + b_ref[...]).astype(o_ref.dtype)


def short_conv(p, conv_w, conv_b, col0, width, tc, out_dtype):
    B, S, _ = p.shape
    c0 = col0 // tc
    return pl.pallas_call(
        _sconv_kernel,
        grid=(B, width // tc),
        in_specs=[pl.BlockSpec((1, S, tc), lambda b, c: (b, 0, c + c0)),
                  pl.BlockSpec((3, tc), lambda b, c: (0, c + c0)),
                  pl.BlockSpec((1, tc), lambda b, c: (0, c + c0))],
        out_specs=pl.BlockSpec((1, S, tc), lambda b, c: (b, 0, c)),
        out_shape=jax.ShapeDtypeStruct((B, S, width), out_dtype),
        compiler_params=_params(("parallel", "parallel"), 48),
        name="short_conv",
    )(p, conv_w, conv_b.reshape(1, -1))


def _dft_fwd_kernel(c_ref, s_ref, u_ref, k_ref, y_ref, *, nblk):
    c = c_ref[...]
    s = s_ref[...]
    ure, uim = [], []
    for j in range(nblk):
        u = u_ref[0, j]
        ure.append(_dot(c, u))
        uim.append(-_dot(s, u))
    for i in range(nblk):
        yre = yim = None
        for j in range(nblk):
            d = i - j + nblk - 1
            kre = k_ref[d, 0]
            kim = k_ref[d, 1]
            tre = kre * ure[j] - kim * uim[j]
            tim = kre * uim[j] + kim * ure[j]
            yre = tre if yre is None else yre + tre
            yim = tim if yim is None else yim + tim
        y_ref[0, i, 0] = yre.astype(BF16)
        y_ref[0, i, 1] = yim.astype(BF16)


def dft_forward(cf, sf, u, kspec, tf):
    B, nblk, Lb, C = u.shape
    nd = 2 * nblk - 1
    return pl.pallas_call(
        functools.partial(_dft_fwd_kernel, nblk=nblk),
        grid=(Lb // tf, B),
        in_specs=[pl.BlockSpec((tf, Lb), lambda f, b: (f, 0)),
                  pl.BlockSpec((tf, Lb), lambda f, b: (f, 0)),
                  pl.BlockSpec((1, nblk, Lb, C), lambda f, b: (b, 0, 0, 0)),
                  pl.BlockSpec((nd, 2, tf, C), lambda f, b: (0, 0, f, 0))],
        out_specs=pl.BlockSpec((1, nblk, 2, tf, C), lambda f, b: (b, 0, 0, f, 0)),
        out_shape=jax.ShapeDtypeStruct((B, nblk, 2, Lb, C), BF16),
        compiler_params=_params(("parallel", "parallel"), 56),
        name="dft_forward",
    )(cf, sf, u, kspec)


def _dft_inv_kernel(ct_ref, st_ref, y_ref, gate_ref, o_ref, *, scale):
    acc = _dot(ct_ref[...], y_ref[0, 0, 0]) - _dot(st_ref[...], y_ref[0, 0, 1])
    o_ref[0, 0] = (gate_ref[0, 0] * (acc * scale)).astype(BF16)


def dft_inverse(ct, st, y, gates, gate_col):
    B, nblk, _, Lb, C = y.shape
    return pl.pallas_call(
        functools.partial(_dft_inv_kernel, scale=1.0 / Lb),
        grid=(B, nblk),
        in_specs=[pl.BlockSpec((Lb, Lb), lambda b, i: (0, 0)),
                  pl.BlockSpec((Lb, Lb), lambda b, i: (0, 0)),
                  pl.BlockSpec((1, 1, 2, Lb, C), lambda b, i: (b, i, 0, 0, 0)),
                  pl.BlockSpec((1, 1, Lb, C), lambda b, i: (b, i, 0, gate_col))],
        out_specs=pl.BlockSpec((1, 1, Lb, C), lambda b, i: (b, i, 0, 0)),
        out_shape=jax.ShapeDtypeStruct((B, nblk, Lb, C), BF16),
        compiler_params=_params(("parallel", "parallel"), 48),
        name="dft_inverse",
    )(ct, st, y, gates)


def dft_tables(L):
    n = 2 * L
    f = lax.broadcasted_iota(I32, (L, L), 0)
    t = lax.broadcasted_iota(I32, (L, L), 1)
    m = ((2 * f + 1) * t) % (2 * n)
    ang = m.astype(F32) * (math.pi / n)
    cf = jnp.cos(ang)
    sf = jnp.sin(ang)
    return cf.astype(BF16), sf.astype(BF16), cf.T.astype(BF16), sf.T.astype(BF16)


def hyena_filter_taps(L, w1, b1, w2, b2, w3, b3, freq, wout, width):
    hp = lax.Precision.HIGHEST
    t = jnp.linspace(0.0, 1.0, L, dtype=F32)[:, None]
    bands = (FILTER_EMB - 1) // 2
    w = 2.0 * math.pi * jnp.arange(L, dtype=F32)[:, None] / L
    f = jnp.linspace(1e-4, bands - 1, bands, dtype=F32)[None, :]
    z = jnp.concatenate([t, jnp.cos(f * w), -jnp.sin(f * w)], axis=-1)
    h = jnp.sin(freq * (jnp.dot(z, w1, precision=hp) + b1))
    h = jnp.sin(freq * (jnp.dot(h, w2, precision=hp) + b2))
    h = jnp.sin(freq * (jnp.dot(h, w3, precision=hp) + b3))
    h = jnp.dot(h, wout, precision=hp)
    max_decay = math.log(FILTER_TARGET) / FILTER_FAST_DECAY
    min_decay = math.log(FILTER_TARGET) / FILTER_SLOW_DECAY
    deltas = jnp.abs(jnp.linspace(min_decay, max_decay, width, dtype=F32))
    decay = jnp.exp(-t * deltas[None, :])
    return h.reshape(L, HYENA_ORDER, 2, width) * decay[:, None, None, :]


def filter_spectra(cf, sf, taps, bias, nblk):
    L, C = taps.shape[0], taps.shape[3]
    Lb = L // nblk
    nd = 2 * nblk - 1
    zero = jnp.zeros((1, C), F32)
    sums, diffs = [], []
    for o in range(HYENA_ORDER):
        h_fwd, h_bwd = taps[:, o, 0], taps[:, o, 1]
        k0 = (h_fwd[0] + h_bwd[0] + bias[o])[None, :]
        kfull = jnp.concatenate([zero, h_bwd[:0:-1], k0, h_fwd[1:]], axis=0)
        krev = jnp.concatenate([h_fwd[:0:-1], k0, h_bwd[1:], zero], axis=0)
        for dlt in range(-(nblk - 1), nblk):
            base = L + dlt * Lb
            kp = kfull[base:base + Lb]
            km = jnp.concatenate([zero, krev[2 * L - base:2 * L - base + Lb - 1]], axis=0)
            sums.append(kp + km)
            diffs.append(kp - km)
    tmm = min(Lb, 512)
    kre = matmul(cf, jnp.concatenate(sums, axis=1), tmm, tmm, tmm)
    kim = -matmul(sf, jnp.concatenate(diffs, axis=1), tmm, tmm, tmm)
    shape = (Lb, HYENA_ORDER, nd, C)
    return jnp.stack([kre.reshape(shape), kim.reshape(shape)], axis=0).transpose(2, 3, 0, 1, 4)


def hyena_operator(gates, v, tables, taps, bias, nblk, tf):
    cf, sf, ct, st = tables
    B, L, C = v.shape
    Lb = L // nblk
    kspec = filter_spectra(cf, sf, taps, bias, nblk)
    gates = gates.reshape(B, nblk, Lb, 2 * C)
    z = v.reshape(B, nblk, Lb, C)
    for o in range(HYENA_ORDER):
        y = dft_forward(cf, sf, z, kspec[o], tf)
        z = dft_inverse(ct, st, y, gates, o)
    return z.reshape(B, L, C)


def _head_norm_rope(x, g, cos, sin_signed, scale):
    lane = lax.broadcasted_iota(I32, x.shape, 1)
    lo = lane < DIFF_HEAD_DIM
    x2 = x * x
    s_lo = jnp.sum(jnp.where(lo, x2, 0.0), axis=-1, keepdims=True)
    s_hi = jnp.sum(jnp.where(lo, 0.0, x2), axis=-1, keepdims=True)
    ms = jnp.where(lo, s_lo, s_hi) * (1.0 / DIFF_HEAD_DIM)
    xn = x * lax.rsqrt(ms + NORM_EPS) * g
    first = (lane & 16) == 0
    partner = jnp.where(first, pltpu.roll(xn, LANES - 16, 1), pltpu.roll(xn, 16, 1))
    return (xn * cos + partner * sin_signed) * scale


def _qkv_kernel(p_ref, cos_ref, sin_ref, qg_ref, kg_ref, *o_refs, has_q, width):
    cos = cos_ref[...]
    sin = sin_ref[...]
    x = p_ref[0]
    col = 0
    outs = list(o_refs)
    if has_q:
        q_ref = outs.pop(0)
        for hd in range(width // LANES):
            sl = slice(col + hd * LANES, col + (hd + 1) * LANES)
            q_ref[0, :, hd * LANES:(hd + 1) * LANES] = _head_norm_rope(
                x[:, sl], qg_ref[...], cos, sin, LOG2E * DIFF_HEAD_DIM ** -0.5).astype(BF16)
        col += width
    k_ref, v_ref = outs
    for hd in range(width // LANES):
        sl = slice(col + hd * LANES, col + (hd + 1) * LANES)
        k_ref[0, :, hd * LANES:(hd + 1) * LANES] = _head_norm_rope(x[:, sl], kg_ref[...], cos, sin, 1.0).astype(BF16)
    col += width
    v_ref[0] = x[:, col:col + width].astype(BF16)


def qkv_prep(p, col_block, has_q, cos, sin_signed, q_g, k_g, tm):
    B, S, _ = p.shape
    width = DIFF_HEADS * 2 * DIFF_HEAD_DIM
    n_out = 3 if has_q else 2
    g2 = lambda g: jnp.concatenate([g, g]).reshape(1, LANES)
    outs = pl.pallas_call(
        functools.partial(_qkv_kernel, has_q=has_q, width=width),
        grid=(B, S // tm),
        in_specs=[pl.BlockSpec((1, tm, n_out * width), lambda b, i: (b, i, col_block)),
                  pl.BlockSpec((tm, LANES), lambda b, i: (i, 0)),
                  pl.BlockSpec((tm, LANES), lambda b, i: (i, 0)),
                  pl.BlockSpec((1, LANES), lambda b, i: (0, 0)),
                  pl.BlockSpec((1, LANES), lambda b, i: (0, 0))],
        out_specs=[pl.BlockSpec((1, tm, width), lambda b, i: (b, i, 0))] * n_out,
        out_shape=[jax.ShapeDtypeStruct((B, S, width), BF16)] * n_out,
        compiler_params=_params(("parallel", "parallel"), 32),
        name="qkv_prep",
    )(p, cos, sin_signed, g2(q_g), g2(k_g))
    return outs


def rope_tables(S, use_rope):
    if not use_rope:
        return jnp.ones((S, LANES), F32), jnp.zeros((S, LANES), F32)
    t = jnp.arange(S, dtype=I32)
    row = (t // GRID_W).astype(F32)[:, None]
    colp = (t % GRID_W).astype(F32)[:, None]
    nf = DIFF_HEAD_DIM // 4
    inv = ROPE_BASE ** (-jnp.arange(nf, dtype=F32) / nf)
    lane = jnp.arange(LANES)
    grp = (lane % DIFF_HEAD_DIM) // nf
    j = lane % nf
    pos = jnp.where((grp < 2)[None, :], row, colp)
    ang = pos * inv[j][None, :]
    sign = jnp.where((grp % 2 == 0)[None, :], -1.0, 1.0)
    return jnp.cos(ang), jnp.sin(ang) * sign


def _attn_kernel(lam_ref, q_ref, kc_ref, vc_ref, *rest, has_lat, out_scale, ck):
    if has_lat:
        kl_ref, vl_ref, g_ref, o_ref = rest
    else:
        g_ref, o_ref = rest
    lam = lam_ref[0]
    q = q_ref[0]
    lane = lax.broadcasted_iota(I32, q.shape, 1)
    nt = (((1,), (1,)), ((), ()))
    zero = jnp.zeros_like(q)
    qm = [jnp.where(lane < DIFF_HEAD_DIM, q, zero), jnp.where(lane >= DIFF_HEAD_DIM, q, zero)]
    chunks = [(kc_ref, vc_ref, 0, kc_ref.shape[1])]
    if has_lat:
        chunks += [(kl_ref, vl_ref, c * ck, ck) for c in range(kl_ref.shape[1] // ck)]

    def scores(ch):
        k = ch[0][0, ch[2]:ch[2] + ch[3], :]
        return [lax.dot_general(qm[mp], k, nt, preferred_element_type=F32) for mp in range(2)]

    m, den, acc = [None, None], [None, None], [None, None]
    s_next = scores(chunks[0])
    for ci, ch in enumerate(chunks):
        s_cur = s_next
        if ci + 1 < len(chunks):
            s_next = scores(chunks[ci + 1])
        v = ch[1][0, ch[2]:ch[2] + ch[3], :]
        for mp in range(2):
            s = s_cur[mp]
            mx = jnp.max(s, axis=-1, keepdims=True)
            if ci == 0:
                p = jnp.exp2(s - mx)
                m[mp], den[mp], acc[mp] = mx, jnp.sum(p, axis=-1, keepdims=True), _dot(p.astype(BF16), v)
            else:
                m_new = jnp.maximum(m[mp], mx)
                alpha = jnp.exp2(m[mp] - m_new)
                p = jnp.exp2(s - m_new)
                den[mp] = alpha * den[mp] + jnp.sum(p, axis=-1, keepdims=True)
                acc[mp] = alpha * acc[mp] + _dot(p.astype(BF16), v)
                m[mp] = m_new
    o = acc[0] * (1.0 / den[0]) - lam * (acc[1] * (1.0 / den[1]))
    ms = jnp.mean(o * o, axis=-1, keepdims=True)
    o_ref[0] = ((o * lax.rsqrt(ms + NORM_EPS) * g_ref[...]) * out_scale).astype(BF16)


def diff_attention(lam, q, k_ctx, v_ctx, k_lat, v_lat, subln_g, out_scale, tq):
    B, Sq, W = q.shape
    H = W // LANES
    has_lat = k_lat is not None
    Sc = k_ctx.shape[1]
    head = lambda S: pl.BlockSpec((1, S, LANES), lambda b, h, i: (b, 0, h))
    in_specs = [pl.BlockSpec(memory_space=pltpu.SMEM),
                pl.BlockSpec((1, tq, LANES), lambda b, h, i: (b, i, h)), head(Sc), head(Sc)]
    args = [lam.reshape(1), q, k_ctx, v_ctx]
    if has_lat:
        in_specs += [head(k_lat.shape[1]), head(k_lat.shape[1])]
        args += [k_lat, v_lat]
    in_specs.append(pl.BlockSpec((1, LANES), lambda b, h, i: (0, 0)))
    args.append(subln_g.reshape(1, LANES))
    return pl.pallas_call(
        functools.partial(_attn_kernel, has_lat=has_lat, out_scale=out_scale, ck=ATTN_KEY_CHUNK),
        grid=(B, H, Sq // tq),
        in_specs=in_specs,
        out_specs=pl.BlockSpec((1, tq, LANES), lambda b, h, i: (b, i, h)),
        out_shape=jax.ShapeDtypeStruct((B, Sq, W), BF16),
        compiler_params=_params(("parallel", "parallel", "parallel"), 56),
        name="diff_attention",
    )(*args)


def _oproj_kernel(hy_ref, o_ref, w_ref, h_ref, gate_ref, out_ref, *, half):
    y = _dot(hy_ref[0].astype(BF16), w_ref[0:half, :]) + _dot(o_ref[0], w_ref[half:, :])
    out_ref[0] = h_ref[0] + gate_ref[0] * y


def out_proj(hy, o, w_out, h, gate, tm):
    B, S, D = h.shape
    half = hy.shape[2]
    return pl.pallas_call(
        functools.partial(_oproj_kernel, half=half),
        grid=(B, S // tm),
        in_specs=[pl.BlockSpec((1, tm, half), lambda b, i: (b, i, 0)),
                  pl.BlockSpec((1, tm, half), lambda b, i: (b, i, 0)),
                  pl.BlockSpec((2 * half, D), lambda b, i: (0, 0)),
                  pl.BlockSpec((1, tm, D), lambda b, i: (b, i, 0)),
                  pl.BlockSpec((1, 1, D), lambda b, i: (b, 0, 0))],
        out_specs=pl.BlockSpec((1, tm, D), lambda b, i: (b, i, 0)),
        out_shape=jax.ShapeDtypeStruct((B, S, D), F32),
        compiler_params=_params(("parallel", "parallel"), 40),
        name="out_proj",
    )(hy, o, w_out, h, gate)


def _shift_rows(x, d):
    return pltpu.roll(x, (-d) % x.shape[0], 0)


def _pool_kernel(hp_ref, hc_ref, hn_ref, g_ref, sc_ref, sh_ref, gate_ref, pw_ref, ps_ref, o_ref, *, T, L):
    i = pl.program_id(1)
    nt = pl.num_programs(1)
    g, sc, sh = g_ref[...], sc_ref[0], sh_ref[0]
    hc = hc_ref[0]
    a_c = _norm_mod(hc, g, sc, sh)
    a_p = jnp.where(i == 0, 0.0, _norm_mod(hp_ref[0], g, sc, sh))
    a_n = jnp.where(i == nt - 1, 0.0, _norm_mod(hn_ref[0], g, sc, sh))
    ext = jnp.concatenate([a_p, a_c, a_n], axis=0)
    tok = i * T + lax.broadcasted_iota(I32, (T, 1), 0)
    G = ext.shape[1] // len(POOL_WINDOWS)
    ys = []
    for gi, w in enumerate(POOL_WINDOWS):
        xg = ext[:, gi * G:(gi + 1) * G]
        s = _shift_rows(xg, -1) + xg
        step = 1
        while 2 * step < w:
            s = _shift_rows(s, -step) + _shift_rows(s, step)
            step *= 2
        cnt = (jnp.minimum(tok + w // 2, L) - jnp.maximum(tok - w // 2, 0)).astype(F32)
        p = s[POOL_HALO:POOL_HALO + T] / cnt - a_c[:, gi * G:(gi + 1) * G]
        ys.append(_dot(p.astype(BF16), pw_ref[gi]))
    y = jnp.concatenate(ys, axis=1) * ps_ref[...]
    o_ref[0] = hc + gate_ref[0] * y


def pool_mixer(h, g, scale, shift, gate, pool_w, pool_scale, T):
    B, S, D = h.shape
    nh = T // POOL_HALO
    last = S // POOL_HALO - 1
    mod = pl.BlockSpec((1, 1, D), lambda b, i: (b, 0, 0))
    return pl.pallas_call(
        functools.partial(_pool_kernel, T=T, L=S),
        grid=(B, S // T),
        in_specs=[pl.BlockSpec((1, POOL_HALO, D), lambda b, i: (b, jnp.maximum(i * nh - 1, 0), 0)),
                  pl.BlockSpec((1, T, D), lambda b, i: (b, i, 0)),
                  pl.BlockSpec((1, POOL_HALO, D), lambda b, i: (b, jnp.minimum((i + 1) * nh, last), 0)),
                  pl.BlockSpec((1, D), lambda b, i: (0, 0)), mod, mod, mod,
                  pl.BlockSpec(pool_w.shape, lambda b, i: (0, 0, 0)),
                  pl.BlockSpec((1, D), lambda b, i: (0, 0))],
        out_specs=pl.BlockSpec((1, T, D), lambda b, i: (b, i, 0)),
        out_shape=jax.ShapeDtypeStruct((B, S, D), F32),
        compiler_params=_params(("parallel", "parallel"), 48),
        name="pool_mixer",
    )(h, h, h, g.reshape(1, D), scale, shift, gate, pool_w, pool_scale.reshape(1, D))


def _ffn_prep_kernel(h_ref, g_ref, sc_ref, sh_ref, rw_ref, x_ref, aff_ref, afft_ref, *, n_exp):
    a = _norm_mod(h_ref[0], g_ref[...], sc_ref[0], sh_ref[0]).astype(BF16)
    x_ref[0] = a
    logits = _dot(a, rw_ref[...])
    lane = lax.broadcasted_iota(I32, logits.shape, 1)
    valid = lane < n_exp
    mx = jnp.max(jnp.where(valid, logits, -jnp.inf), axis=-1, keepdims=True)
    e = jnp.where(valid, jnp.exp(logits - mx), 0.0)
    aff = e / jnp.sum(e, axis=-1, keepdims=True)
    aff_ref[0] = aff
    afft_ref[0] = aff.T[0:n_exp]


def ffn_prep(h, g, scale, shift, router_w, tm):
    B, S, D = h.shape
    n_exp = router_w.shape[1]
    rw = jnp.pad(router_w, ((0, 0), (0, LANES - n_exp))).astype(BF16)
    mod = pl.BlockSpec((1, 1, D), lambda b, i: (b, 0, 0))
    return pl.pallas_call(
        functools.partial(_ffn_prep_kernel, n_exp=n_exp),
        grid=(B, S // tm),
        in_specs=[pl.BlockSpec((1, tm, D), lambda b, i: (b, i, 0)),
                  pl.BlockSpec((1, D), lambda b, i: (0, 0)), mod, mod,
                  pl.BlockSpec((D, LANES), lambda b, i: (0, 0))],
        out_specs=[pl.BlockSpec((1, tm, D), lambda b, i: (b, i, 0)),
                   pl.BlockSpec((1, tm, LANES), lambda b, i: (b, i, 0)),
                   pl.BlockSpec((1, n_exp, tm), lambda b, i: (b, 0, i))],
        out_shape=[jax.ShapeDtypeStruct((B, S, D), BF16), jax.ShapeDtypeStruct((B, S, LANES), F32),
                   jax.ShapeDtypeStruct((B, n_exp, S), F32)],
        compiler_params=_params(("parallel", "parallel"), 32),
        name="ffn_prep",
    )(h, g.reshape(1, D), scale, shift, rw)


def _prefix_excl(m, tri, tb):
    S = m.shape[0]
    carry = jnp.zeros((1, m.shape[1]), F32)
    outs, carries = [], []
    for blk in range(S // tb):
        mb = m[blk * tb:(blk + 1) * tb]
        outs.append(_dot(tri, mb.astype(BF16)) + carry)
        carries.append(carry)
        carry = carry + jnp.sum(mb, axis=0, keepdims=True)
    carries.append(carry)
    return (jnp.concatenate(outs, axis=0) if len(outs) > 1 else outs[0]), jnp.concatenate(carries, axis=0)


def _select_kernel(aff_ref, afft_ref, tri_ref, pos_ref, post_ref, offs_ref, *, cap, tb, n_exp):
    bits = pltpu.bitcast(afft_ref[0], I32)

    def body(i, cur):
        cand = cur | (jnp.int32(1) << (30 - i))
        cnt = jnp.sum((bits >= cand).astype(F32), axis=1, keepdims=True)
        return jnp.where(cnt >= cap, cand, cur)

    thr_bits = lax.fori_loop(0, 31, body, jnp.zeros((n_exp, 1), I32))
    thr_col = pltpu.bitcast(jnp.broadcast_to(thr_bits, (n_exp, LANES)), F32)
    sub = lax.broadcasted_iota(I32, (n_exp, LANES), 0)
    lane = lax.broadcasted_iota(I32, (n_exp, LANES), 1)
    thr = jnp.sum(jnp.where(sub == lane, thr_col, 0.0), axis=0, keepdims=True)
    aff = aff_ref[0]
    tri = tri_ref[...]
    gt = (aff > thr).astype(F32)
    eq = (aff == thr).astype(F32)
    need = cap - jnp.sum(gt, axis=0, keepdims=True)
    sel = gt + eq * (_prefix_excl(eq, tri, tb)[0] < need).astype(F32)
    slot, offs = _prefix_excl(sel, tri, tb)
    sp = jnp.where(sel > 0.0, slot, -1.0)
    pos_ref[0] = sp
    offs_ref[0] = offs
    for blk in range(sp.shape[0] // tb):
        post_ref[0, :, blk * tb:(blk + 1) * tb] = sp[blk * tb:(blk + 1) * tb].T[0:n_exp]


def select_tokens(aff, afft, cap):
    B, S, _ = aff.shape
    n_exp = afft.shape[1]
    tb = min(S, 256)
    r = lax.broadcasted_iota(I32, (tb, tb), 0)
    c = lax.broadcasted_iota(I32, (tb, tb), 1)
    tri = (c < r).astype(BF16)
    nblk = S // tb
    pos, post, offs = pl.pallas_call(
        functools.partial(_select_kernel, cap=cap, tb=tb, n_exp=n_exp),
        grid=(B,),
        in_specs=[pl.BlockSpec((1, S, LANES), lambda b: (b, 0, 0)),
                  pl.BlockSpec((1, n_exp, S), lambda b: (b, 0, 0)),
                  pl.BlockSpec((tb, tb), lambda b: (0, 0))],
        out_specs=[pl.BlockSpec((1, S, LANES), lambda b: (b, 0, 0)),
                   pl.BlockSpec((1, n_exp, S), lambda b: (b, 0, 0)),
                   pl.BlockSpec((1, nblk + 1, LANES), lambda b: (b, 0, 0))],
        out_shape=[jax.ShapeDtypeStruct((B, S, LANES), F32), jax.ShapeDtypeStruct((B, n_exp, S), F32),
                   jax.ShapeDtypeStruct((B, nblk + 1, LANES), F32)],
        compiler_params=_params(("parallel",), 48),
        name="select_tokens",
    )(aff, afft, tri)
    offs = jnp.swapaxes(offs[:, :, :n_exp], 1, 2).astype(I32).reshape(-1)
    return pos, post, offs, tb


def _gather_kernel(offs_ref, x_ref, post_ref, post4_ref, afft_ref, afft4_ref, o_ref, g_ref, *, slots, rc, tb, nblk, n_exp, win):
    b = pl.program_id(0)
    e = pl.program_id(1)
    S = x_ref.shape[1]
    base = (b * n_exp + e) * (nblk + 1)
    for k in range(slots // rc):
        rows = slice(k * rc, (k + 1) * rc)

        def full():
            r = lax.broadcasted_iota(I32, (rc, S), 0).astype(F32) + float(k * rc)
            hit = r == post_ref[0, pl.ds(e, 1), :]
            o_ref[0, 0, rows, :] = _dot(hit.astype(BF16), x_ref[0]).astype(BF16)
            g = jnp.sum(jnp.where(hit, afft_ref[0, pl.ds(e, 1), :], 0.0), axis=1, keepdims=True)
            g_ref[0, 0, rows, :] = jnp.broadcast_to(g, (rc, LANES))

        if win >= S:
            full()
            continue
        i0 = jnp.int32(0)
        i1 = jnp.int32(0)
        for i in range(nblk):
            i0 += (offs_ref[base + i + 1] <= k * rc).astype(I32)
            i1 += (offs_ref[base + i] < (k + 1) * rc).astype(I32)
        narrow = (i1 - i0) * tb <= win
        wb = win // tb

        @pl.when(narrow)
        def _():
            j0 = jnp.minimum(i0, nblk - wb)
            r = lax.broadcasted_iota(I32, (rc, tb), 0).astype(F32) + float(k * rc)
            pieces = []
            gacc = jnp.zeros((rc, tb), F32)
            for j in range(wb):
                hit = r == post4_ref[0, pl.ds(e, 1), pl.ds(j0 + j, 1), :].reshape(1, tb)
                pieces.append(hit.astype(BF16))
                gacc = gacc + jnp.where(hit, afft4_ref[0, pl.ds(e, 1), pl.ds(j0 + j, 1), :].reshape(1, tb), 0.0)
            xw = x_ref[0, pl.ds(pl.multiple_of(j0 * tb, tb), win), :]
            o_ref[0, 0, rows, :] = _dot(jnp.concatenate(pieces, axis=1), xw).astype(BF16)
            g_ref[0, 0, rows, :] = jnp.broadcast_to(jnp.sum(gacc, axis=1, keepdims=True), (rc, LANES))

        pl.when(jnp.logical_not(narrow))(full)


def gather_tokens(x, post, afft, offs, slots, tb):
    B, S, D = x.shape
    E = post.shape[1]
    nblk = S // tb
    rc = min(slots, MOE_ROW_CHUNK)
    win = min(S, GATHER_WINDOW * rc)
    row = pl.BlockSpec((1, E, S), lambda b, e, offs: (b, 0, 0))
    row4 = pl.BlockSpec((1, E, nblk, tb), lambda b, e, offs: (b, 0, 0, 0))
    return pl.pallas_call(
        functools.partial(_gather_kernel, slots=slots, rc=rc, tb=tb, nblk=nblk, n_exp=E, win=win),
        grid_spec=pltpu.PrefetchScalarGridSpec(
            num_scalar_prefetch=1,
            grid=(B, E),
            in_specs=[pl.BlockSpec((1, S, D), lambda b, e, offs: (b, 0, 0)), row, row4, row, row4],
            out_specs=[pl.BlockSpec((1, 1, slots, D), lambda b, e, offs: (e, b, 0, 0)),
                       pl.BlockSpec((1, 1, slots, LANES), lambda b, e, offs: (e, b, 0, 0))]),
        out_shape=[jax.ShapeDtypeStruct((E, B, slots, D), BF16), jax.ShapeDtypeStruct((E, B, slots, LANES), F32)],
        compiler_params=_params(("parallel", "arbitrary"), 48),
        name="gather_tokens",
    )(offs, x, post, post.reshape(B, E, nblk, tb), afft, afft.reshape(B, E, nblk, tb))


def _ffn_kernel(xs_ref, g_ref, wg_ref, wu_ref, wd_ref, o_ref, acc_ref, *, nf):
    f = pl.program_id(2)
    x = xs_ref[0]
    a = _dot(x, wg_ref[...].astype(BF16))
    u = _dot(x, wu_ref[...].astype(BF16))
    hm = (a * jax.nn.sigmoid(a) * u).astype(BF16)
    part = _dot(hm, wd_ref[...].astype(BF16))

    @pl.when(f == 0)
    def _():
        acc_ref[...] = part

    @pl.when(f > 0)
    def _():
        acc_ref[...] += part

    @pl.when(f == nf - 1)
    def _():
        o_ref[0] = (acc_ref[...] * g_ref[0, :, 0:1]).astype(BF16)


def expert_ffn(xs, gs, w_gate, w_up, w_down, layer, tm, tf):
    E, M, D = xs.shape
    F = w_gate.shape[3]
    nf = F // tf
    return pl.pallas_call(
        functools.partial(_ffn_kernel, nf=nf),
        grid=(E, M // tm, nf),
        in_specs=[pl.BlockSpec((1, tm, D), lambda e, m, f: (e, m, 0)),
                  pl.BlockSpec((1, tm, LANES), lambda e, m, f: (e, m, 0)),
                  pl.BlockSpec((None, None, D, tf), lambda e, m, f: (layer, e, 0, f)),
                  pl.BlockSpec((None, None, D, tf), lambda e, m, f: (layer, e, 0, f)),
                  pl.BlockSpec((None, None, tf, D), lambda e, m, f: (layer, e, f, 0))],
        out_specs=pl.BlockSpec((1, tm, D), lambda e, m, f: (e, m, 0)),
        out_shape=jax.ShapeDtypeStruct((E, M, D), BF16),
        scratch_shapes=[pltpu.VMEM((tm, D), F32)],
        compiler_params=_params(("parallel", "parallel", "arbitrary"), 56),
        name="expert_ffn",
    )(xs, gs, w_gate, w_up, w_down)


def _combine_kernel(offs_ref, y_ref, pos_ref, h_ref, gate_ref, o_ref, ybuf_ref, *, nblk, n_exp, win):
    b = pl.program_id(0)
    t = pl.program_id(1)
    T = pos_ref.shape[1]
    slots = y_ref.shape[2]

    def full():
        r = lax.broadcasted_iota(I32, (T, slots), 1).astype(F32)
        acc = None
        for e in range(n_exp):
            part = _dot((pos_ref[0, :, e:e + 1] == r).astype(BF16), y_ref[e, 0])
            acc = part if acc is None else acc + part
        o_ref[0] = h_ref[0] + gate_ref[0] * acc

    if win >= slots:
        full()
        return
    starts = []
    narrow = None
    for e in range(n_exp):
        base = (b * n_exp + e) * (nblk + 1) + t
        lo = offs_ref[base]
        hi = offs_ref[base + 1]
        a0 = jnp.minimum((lo // BF16_ROWS) * BF16_ROWS, slots - win)
        ok = hi - a0 <= win
        narrow = ok if narrow is None else jnp.logical_and(narrow, ok)
        starts.append(a0)

    @pl.when(narrow)
    def _():
        r = lax.broadcasted_iota(I32, (T, win), 1).astype(F32)
        pieces = []
        for e in range(n_exp):
            a0 = starts[e]
            ybuf_ref[e * win:(e + 1) * win, :] = y_ref[e, 0, pl.ds(pl.multiple_of(a0, BF16_ROWS), win), :]
            pieces.append((pos_ref[0, :, e:e + 1] - a0.astype(F32) == r).astype(BF16))
        o_ref[0] = h_ref[0] + gate_ref[0] * _dot(jnp.concatenate(pieces, axis=1), ybuf_ref[...])

    pl.when(jnp.logical_not(narrow))(full)


def combine_tokens(y, pos, offs, h, gate, T):
    E, B, slots, D = y.shape
    S = h.shape[1]
    win = min(slots, COMBINE_WINDOW)
    return pl.pallas_call(
        functools.partial(_combine_kernel, nblk=S // T, n_exp=E, win=win),
        grid_spec=pltpu.PrefetchScalarGridSpec(
            num_scalar_prefetch=1,
            grid=(B, S // T),
            in_specs=[pl.BlockSpec((E, 1, slots, D), lambda b, t, offs: (0, b, 0, 0)),
                      pl.BlockSpec((1, T, LANES), lambda b, t, offs: (b, t, 0)),
                      pl.BlockSpec((1, T, D), lambda b, t, offs: (b, t, 0)),
                      pl.BlockSpec((1, 1, D), lambda b, t, offs: (b, 0, 0))],
            out_specs=pl.BlockSpec((1, T, D), lambda b, t, offs: (b, t, 0)),
            scratch_shapes=[pltpu.VMEM((E * win, D), BF16)]),
        out_shape=jax.ShapeDtypeStruct((B, S, D), F32),
        compiler_params=_params(("parallel", "parallel"), 56),
        name="combine_tokens",
    )(offs, y, pos, h, gate)


def moe_block(h, g, scale, shift, gate, router_w, w_gate, w_up, w_down, layer):
    B, S, D = h.shape
    E = router_w.shape[1]
    cap = EC_CAPACITY * S // E
    slots = max(cap, LANES)
    x, aff, afft = ffn_prep(h, g, scale, shift, router_w, min(S, 512))
    pos, post, offs, tb = select_tokens(aff, afft, cap)
    xs, gs = gather_tokens(x, post, afft, offs, slots, tb)
    M = B * slots
    y = expert_ffn(xs.reshape(E, M, D), gs.reshape(E, M, LANES), w_gate, w_up, w_down, layer, min(M, 1024), 512)
    return combine_tokens(y.reshape(E, B, slots, D), pos, offs, h, gate, tb)


def kernel(x, c, ctx, c_ctx, ada_w, ada_b, norm_mix_g, norm_ffn_g, w_in, w_out, hy_conv_w, hy_conv_b, hy_f_w1, hy_f_b1, hy_f_w2, hy_f_b2, hy_f_w3, hy_f_b3, hy_f_freq, hy_f_wout, hy_bias, q_norm_g, k_norm_g, diff_lambda, subln_g, pool_w, pool_scale, router_w, exp_w_gate, exp_w_up, exp_w_down):
    B, S, D = x.shape
    Lc = ctx.shape[1]
    depth = ada_w.shape[0]
    hy_width = hy_bias.shape[2]
    hy_proj = (HYENA_ORDER + 1) * hy_width
    qk_width = DIFF_HEADS * 2 * DIFF_HEAD_DIM
    last_attn = ((depth - 1) // 2) * 2

    s_all = jnp.concatenate([jax.nn.silu(c), jax.nn.silu(c_ctx)[None, :]], axis=0)
    rows = -(-s_all.shape[0] // SUBLANES) * SUBLANES
    mods = ada_modulation(jnp.pad(s_all, ((0, rows - s_all.shape[0]), (0, 0))), ada_w, ada_b)

    nblk_lat = max(1, S // HYENA_BLOCK)
    tables_lat = dft_tables(S // nblk_lat)
    tables_ctx = dft_tables(Lc)
    rope_lat = rope_tables(S, True)
    rope_ctx = rope_tables(Lc, False)

    h, hc = x, ctx
    for l in range(depth):
        m = [mods[l, :B, i * D:(i + 1) * D].reshape(B, 1, D) for i in range(6)]
        mc = [jnp.broadcast_to(mods[l, B, i * D:(i + 1) * D].reshape(1, 1, D), (B, 1, D)) for i in range(6)]
        ctx_full = l < last_attn
        if l % 2 == 0:
            e = l // 2
            lam_init = 0.8 - 0.6 * math.exp(-0.3 * l)
            lv = diff_lambda[e]
            lam = jnp.exp(jnp.sum(lv[0] * lv[1])) - jnp.exp(jnp.sum(lv[2] * lv[3])) + lam_init
            filt = (hy_f_w1[e], hy_f_b1[e], hy_f_w2[e], hy_f_b2[e], hy_f_w3[e], hy_f_b3[e], hy_f_freq[e], hy_f_wout[e])
            w_in_b = w_in[e].astype(BF16)
            w_out_b = w_out[e].astype(BF16)

            p = norm_mod_matmul(h, norm_mix_g[l], m[1], m[0], w_in_b, 512)
            gts = short_conv(p, hy_conv_w[e], hy_conv_b[e], 0, 2 * hy_width, 256, F32)
            hv = short_conv(p, hy_conv_w[e], hy_conv_b[e], 2 * hy_width, hy_width, 256, BF16)
            hy = hyena_operator(gts, hv, tables_lat, hyena_filter_taps(S, *filt, hy_width), hy_bias[e], nblk_lat, 256)
            q, k, v = qkv_prep(p, 1, True, *rope_lat, q_norm_g[e], k_norm_g[e], 512)

            if l <= last_attn:
                if ctx_full:
                    pc = norm_mod_matmul(hc, norm_mix_g[l], mc[1], mc[0], w_in_b, Lc)
                    qc, kc, vc = qkv_prep(pc, 1, True, *rope_ctx, q_norm_g[e], k_norm_g[e], Lc)
                else:
                    pc = norm_mod_matmul(hc, norm_mix_g[l], mc[1], mc[0], w_in_b[:, hy_proj + qk_width:], Lc)
                    kc, vc = qkv_prep(pc, 0, False, *rope_ctx, q_norm_g[e], k_norm_g[e], Lc)
            o = diff_attention(lam, q, kc, vc, k, v, subln_g[e], 1.0 - lam_init, min(S, 512))
            h = out_proj(hy, o, w_out_b, h, m[2], 512)
            if ctx_full:
                gtc = short_conv(pc, hy_conv_w[e], hy_conv_b[e], 0, 2 * hy_width, 256, F32)
                hvc = short_conv(pc, hy_conv_w[e], hy_conv_b[e], 2 * hy_width, hy_width, 256, BF16)
                hyc = hyena_operator(gtc, hvc, tables_ctx, hyena_filter_taps(Lc, *filt, hy_width), hy_bias[e], 1, Lc)
                oc = diff_attention(lam, qc, kc, vc, None, None, subln_g[e], 1.0 - lam_init, Lc)
                hc = out_proj(hyc, oc, w_out_b, hc, mc[2], Lc)
        else:
            o_idx = l // 2
            pw = pool_w[o_idx].astype(BF16)
            h = pool_mixer(h, norm_mix_g[l], m[1], m[0], m[2], pw, pool_scale[o_idx], 512)
            if ctx_full:
                hc = pool_mixer(hc, norm_mix_g[l], mc[1], mc[0], mc[2], pw, pool_scale[o_idx], Lc)
        h = moe_block(h, norm_ffn_g[l], m[4], m[3], m[5], router_w[l], exp_w_gate, exp_w_up, exp_w_down, l)
        if ctx_full:
            hc = moe_block(hc, norm_ffn_g[l], mc[4], mc[3], mc[5], router_w[l], exp_w_gate, exp_w_up, exp_w_down, l)
    return h
```

```python
import functools
import math

import jax
import jax.numpy as jnp
from jax import lax
from jax.experimental import pallas as pl
from jax.experimental.pallas import tpu as pltpu

F32 = jnp.float32
BF16 = jnp.bfloat16
I32 = jnp.int32

NORM_EPS = 1e-6
GRID_W = 64
HYENA_ORDER = 2
FILTER_EMB = 33
FILTER_FAST_DECAY = 0.3
FILTER_SLOW_DECAY = 1.5
FILTER_TARGET = 1e-2
DIFF_HEADS = 4
DIFF_HEAD_DIM = 64
ROPE_BASE = 10000.0
POOL_WINDOWS = (2, 4, 8, 16)
EC_CAPACITY = 2
LANES = 128
SUBLANES = 8
POOL_HALO = 8
LOG2E = 1.4426950408889634
ATTN_KEY_CHUNK = 1024
MOE_ROW_CHUNK = 256
GATHER_WINDOW = 10
COMBINE_WINDOW = 128
BF16_ROWS = 16
HYENA_BLOCK = 1024


def _params(sem, vmem_mb):
    return pltpu.CompilerParams(dimension_semantics=sem, vmem_limit_bytes=vmem_mb * 1024 * 1024)


def _dot(a, b):
    return jnp.dot(a, b, preferred_element_type=F32)


def _norm_mod(x, g, scale, shift):
    ms = jnp.mean(x * x, axis=-1, keepdims=True)
    return (x * lax.rsqrt(ms + NORM_EPS) * g) * (1.0 + scale) + shift


def _mm_kernel(a_ref, b_ref, o_ref, acc_ref, *, nk):
    k = pl.program_id(2)
    part = _dot(a_ref[...].astype(BF16), b_ref[...].astype(BF16))

    @pl.when(k == 0)
    def _():
        acc_ref[...] = part

    @pl.when(k > 0)
    def _():
        acc_ref[...] += part

    @pl.when(k == nk - 1)
    def _():
        o_ref[...] = acc_ref[...].astype(o_ref.dtype)


def matmul(a, b, tm, tn, tk):
    M, K = a.shape
    N = b.shape[1]
    nk = K // tk
    return pl.pallas_call(
        functools.partial(_mm_kernel, nk=nk),
        grid=(M // tm, N // tn, nk),
        in_specs=[pl.BlockSpec((tm, tk), lambda i, j, k: (i, k)),
                  pl.BlockSpec((tk, tn), lambda i, j, k: (k, j))],
        out_specs=pl.BlockSpec((tm, tn), lambda i, j, k: (i, j)),
        out_shape=jax.ShapeDtypeStruct((M, N), F32),
        scratch_shapes=[pltpu.VMEM((tm, tn), F32)],
        compiler_params=_params(("parallel", "parallel", "arbitrary"), 40),
        name="matmul",
    )(a, b)


def _ada_kernel(s_ref, w_ref, b_ref, o_ref):
    o_ref[...] = _dot(s_ref[...].astype(BF16), w_ref[...].astype(BF16)) + b_ref[...]


def ada_modulation(s, ada_w, ada_b):
    depth, D, N = ada_w.shape
    R = s.shape[0]
    tn = 1024
    return pl.pallas_call(
        _ada_kernel,
        grid=(depth, N // tn),
        in_specs=[pl.BlockSpec((R, D), lambda l, j: (0, 0)),
                  pl.BlockSpec((None, D, tn), lambda l, j: (l, 0, j)),
                  pl.BlockSpec((None, 1, tn), lambda l, j: (l, 0, j))],
        out_specs=pl.BlockSpec((None, R, tn), lambda l, j: (l, 0, j)),
        out_shape=jax.ShapeDtypeStruct((depth, R, N), F32),
        compiler_params=_params(("parallel", "parallel"), 32),
        name="ada_modulation",
    )(s, ada_w, ada_b.reshape(depth, 1, N))


def _nmm_kernel(h_ref, g_ref, sc_ref, sh_ref, w_ref, o_ref):
    a = _norm_mod(h_ref[0], g_ref[...], sc_ref[0], sh_ref[0])
    o_ref[0] = _dot(a.astype(BF16), w_ref[...])


def norm_mod_matmul(h, g, scale, shift, w, tm):
    B, S, D = h.shape
    N = w.shape[1]
    return pl.pallas_call(
        _nmm_kernel,
        grid=(B, S // tm),
        in_specs=[pl.BlockSpec((1, tm, D), lambda b, i: (b, i, 0)),
                  pl.BlockSpec((1, D), lambda b, i: (0, 0)),
                  pl.BlockSpec((1, 1, D), lambda b, i: (b, 0, 0)),
                  pl.BlockSpec((1, 1, D), lambda b, i: (b, 0, 0)),
                  pl.BlockSpec((D, N), lambda b, i: (0, 0))],
        out_specs=pl.BlockSpec((1, tm, N), lambda b, i: (b, i, 0)),
        out_shape=jax.ShapeDtypeStruct((B, S, N), F32),
        compiler_params=_params(("parallel", "parallel"), 48),
        name="norm_mod_matmul",
    )(h, g.reshape(1, D), scale, shift, w)


def _sconv_kernel(p_ref, w_ref, b_ref, o_ref):
    x = p_ref[0]
    S = x.shape[0]
    row = lax.broadcasted_iota(I32, x.shape, 0)
    xm = jnp.where(row == 0, 0.0, pltpu.roll(x, 1, 0))
    xp = jnp.where(row == S - 1, 0.0, pltpu.roll(x, S - 1, 0))
    w = w_ref[...]
    o_ref[0] = (xm * w[0:1] + x * w[1:2] + xp * w[2:3] + b_ref[...]).astype(o_ref.dtype)


def short_conv(p, conv_w, conv_b, col0, width, tc, out_dtype):
    B, S, _ = p.shape
    c0 = col0 // tc
    return pl.pallas_call(
        _sconv_kernel,
        grid=(B, width // tc),
        in_specs=[pl.BlockSpec((1, S, tc), lambda b, c: (b, 0, c + c0)),
                  pl.BlockSpec((3, tc), lambda b, c: (0, c + c0)),
                  pl.BlockSpec((1, tc), lambda b, c: (0, c + c0))],
        out_specs=pl.BlockSpec((1, S, tc), lambda b, c: (b, 0, c)),
        out_shape=jax.ShapeDtypeStruct((B, S, width), out_dtype),
        compiler_params=_params(("parallel", "parallel"), 48),
        name="short_conv",
    )(p, conv_w, conv_b.reshape(1, -1))


def _dft_fwd_kernel(c_ref, s_ref, u_ref, kre_ref, kim_ref, y_ref, *, nblk):
    c = c_ref[...]
    s = s_ref[...]
    C = u_ref.shape[3]
    ure, uim = [], []
    for j in range(nblk):
        u = u_ref[0, j]
        ure.append(_dot(c, u))
        uim.append(-_dot(s, u))
    for i in range(nblk):
        yre = yim = None
        for j in range(nblk):
            d = i - j + nblk - 1
            kre = kre_ref[:, d * C:(d + 1) * C]
            kim = kim_ref[:, d * C:(d + 1) * C]
            tre = kre * ure[j] - kim * uim[j]
            tim = kre * uim[j] + kim * ure[j]
            yre = tre if yre is None else yre + tre
            yim = tim if yim is None else yim + tim
        y_ref[0, i, 0] = yre.astype(BF16)
        y_ref[0, i, 1] = yim.astype(BF16)


def dft_forward(cf, sf, u, kre, kim, order, tf):
    B, nblk, Lb, C = u.shape
    nd = 2 * nblk - 1
    return pl.pallas_call(
        functools.partial(_dft_fwd_kernel, nblk=nblk),
        grid=(Lb // tf, B),
        in_specs=[pl.BlockSpec((tf, Lb), lambda f, b: (f, 0)),
                  pl.BlockSpec((tf, Lb), lambda f, b: (f, 0)),
                  pl.BlockSpec((1, nblk, Lb, C), lambda f, b: (b, 0, 0, 0)),
                  pl.BlockSpec((tf, nd * C), lambda f, b: (f, order)),
                  pl.BlockSpec((tf, nd * C), lambda f, b: (f, order))],
        out_specs=pl.BlockSpec((1, nblk, 2, tf, C), lambda f, b: (b, 0, 0, f, 0)),
        out_shape=jax.ShapeDtypeStruct((B, nblk, 2, Lb, C), BF16),
        compiler_params=_params(("parallel", "parallel"), 56),
        name="dft_forward",
    )(cf, sf, u, kre, kim)


def _dft_inv_kernel(ct_ref, st_ref, y_ref, gate_ref, o_ref, *, scale):
    acc = _dot(ct_ref[...], y_ref[0, 0, 0]) - _dot(st_ref[...], y_ref[0, 0, 1])
    o_ref[0, 0] = (gate_ref[0, 0] * (acc * scale)).astype(BF16)


def dft_inverse(ct, st, y, gates, gate_col):
    B, nblk, _, Lb, C = y.shape
    return pl.pallas_call(
        functools.partial(_dft_inv_kernel, scale=1.0 / Lb),
        grid=(B, nblk),
        in_specs=[pl.BlockSpec((Lb, Lb), lambda b, i: (0, 0)),
                  pl.BlockSpec((Lb, Lb), lambda b, i: (0, 0)),
                  pl.BlockSpec((1, 1, 2, Lb, C), lambda b, i: (b, i, 0, 0, 0)),
                  pl.BlockSpec((1, 1, Lb, C), lambda b, i: (b, i, 0, gate_col))],
        out_specs=pl.BlockSpec((1, 1, Lb, C), lambda b, i: (b, i, 0, 0)),
        out_shape=jax.ShapeDtypeStruct((B, nblk, Lb, C), BF16),
        compiler_params=_params(("parallel", "parallel"), 48),
        name="dft_inverse",
    )(ct, st, y, gates)


def dft_tables(L):
    n = 2 * L
    f = lax.broadcasted_iota(I32, (L, L), 0)
    t = lax.broadcasted_iota(I32, (L, L), 1)
    m = ((2 * f + 1) * t) % (2 * n)
    ang = m.astype(F32) * (math.pi / n)
    cf = jnp.cos(ang)
    sf = jnp.sin(ang)
    return cf.astype(BF16), sf.astype(BF16), cf.T.astype(BF16), sf.T.astype(BF16)


def hyena_filter_taps(L, w1, b1, w2, b2, w3, b3, freq, wout, width):
    hp = lax.Precision.HIGHEST
    t = jnp.linspace(0.0, 1.0, L, dtype=F32)[:, None]
    bands = (FILTER_EMB - 1) // 2
    w = 2.0 * math.pi * jnp.arange(L, dtype=F32)[:, None] / L
    f = jnp.linspace(1e-4, bands - 1, bands, dtype=F32)[None, :]
    z = jnp.concatenate([t, jnp.cos(f * w), -jnp.sin(f * w)], axis=-1)
    h = jnp.sin(freq * (jnp.dot(z, w1, precision=hp) + b1))
    h = jnp.sin(freq * (jnp.dot(h, w2, precision=hp) + b2))
    h = jnp.sin(freq * (jnp.dot(h, w3, precision=hp) + b3))
    max_decay = math.log(FILTER_TARGET) / FILTER_FAST_DECAY
    min_decay = math.log(FILTER_TARGET) / FILTER_SLOW_DECAY
    deltas = jnp.abs(jnp.linspace(min_decay, max_decay, width, dtype=F32))

    def taps(hh, tt):
        return jnp.dot(hh, wout, precision=hp).reshape(L, HYENA_ORDER, 2, width) * jnp.exp(-tt * deltas[None, :])[:, None, None, :]

    return taps(h, t), taps(h[::-1], t[::-1])


def filter_spectra(cf, sf, taps, taps_rev, bias, nblk):
    L, C = taps.shape[0], taps.shape[3]
    Lb = L // nblk
    zero = jnp.zeros((1, C), F32)
    sums, diffs = [], []
    for o in range(HYENA_ORDER):
        h_fwd, h_bwd = taps[:, o, 0], taps[:, o, 1]
        h_fwd_rev, h_bwd_rev = taps_rev[:, o, 0], taps_rev[:, o, 1]
        k0 = (h_fwd[0] + h_bwd[0] + bias[o])[None, :]
        kfull = jnp.concatenate([zero, h_bwd_rev[:L - 1], k0, h_fwd[1:]], axis=0)
        krev = jnp.concatenate([h_fwd_rev[:L - 1], k0, h_bwd[1:], zero], axis=0)
        for dlt in range(-(nblk - 1), nblk):
            base = L + dlt * Lb
            kp = kfull[base:base + Lb]
            km = jnp.concatenate([zero, krev[2 * L - base:2 * L - base + Lb - 1]], axis=0)
            sums.append(kp + km)
            diffs.append(km - kp)
    tmm = min(Lb, 512)
    kre = matmul(cf, jnp.concatenate(sums, axis=1), tmm, tmm, tmm)
    kim = matmul(sf, jnp.concatenate(diffs, axis=1), tmm, tmm, tmm)
    return kre, kim


def hyena_operator(gates, v, tables, taps, taps_rev, bias, nblk, tf):
    cf, sf, ct, st = tables
    B, L, C = v.shape
    Lb = L // nblk
    kre, kim = filter_spectra(cf, sf, taps, taps_rev, bias, nblk)
    gates = gates.reshape(B, nblk, Lb, 2 * C)
    z = v.reshape(B, nblk, Lb, C)
    for o in range(HYENA_ORDER):
        y = dft_forward(cf, sf, z, kre, kim, o, tf)
        z = dft_inverse(ct, st, y, gates, o)
    return z.reshape(B, L, C)


def _head_norm_rope(x, g, cos, sin_signed, scale):
    lane = lax.broadcasted_iota(I32, x.shape, 1)
    lo = lane < DIFF_HEAD_DIM
    x2 = x * x
    s_lo = jnp.sum(jnp.where(lo, x2, 0.0), axis=-1, keepdims=True)
    s_hi = jnp.sum(jnp.where(lo, 0.0, x2), axis=-1, keepdims=True)
    ms = jnp.where(lo, s_lo, s_hi) * (1.0 / DIFF_HEAD_DIM)
    xn = x * lax.rsqrt(ms + NORM_EPS) * g
    first = (lane & 16) == 0
    partner = jnp.where(first, pltpu.roll(xn, LANES - 16, 1), pltpu.roll(xn, 16, 1))
    return (xn * cos + partner * sin_signed) * scale


def _qkv_kernel(p_ref, cos_ref, sin_ref, qg_ref, kg_ref, *o_refs, has_q, width):
    cos = cos_ref[...]
    sin = sin_ref[...]
    x = p_ref[0]
    col = 0
    outs = list(o_refs)
    if has_q:
        q_ref = outs.pop(0)
        for hd in range(width // LANES):
            sl = slice(col + hd * LANES, col + (hd + 1) * LANES)
            q_ref[0, :, hd * LANES:(hd + 1) * LANES] = _head_norm_rope(
                x[:, sl], qg_ref[...], cos, sin, LOG2E * DIFF_HEAD_DIM ** -0.5).astype(BF16)
        col += width
    k_ref, v_ref = outs
    for hd in range(width // LANES):
        sl = slice(col + hd * LANES, col + (hd + 1) * LANES)
        k_ref[0, :, hd * LANES:(hd + 1) * LANES] = _head_norm_rope(x[:, sl], kg_ref[...], cos, sin, 1.0).astype(BF16)
    col += width
    v_ref[0] = x[:, col:col + width].astype(BF16)


def qkv_prep(p, col_block, has_q, cos, sin_signed, q_g, k_g, tm):
    B, S, _ = p.shape
    width = DIFF_HEADS * 2 * DIFF_HEAD_DIM
    n_out = 3 if has_q else 2
    g2 = lambda g: jnp.concatenate([g, g]).reshape(1, LANES)
    outs = pl.pallas_call(
        functools.partial(_qkv_kernel, has_q=has_q, width=width),
        grid=(B, S // tm),
        in_specs=[pl.BlockSpec((1, tm, n_out * width), lambda b, i: (b, i, col_block)),
                  pl.BlockSpec((tm, LANES), lambda b, i: (i, 0)),
                  pl.BlockSpec((tm, LANES), lambda b, i: (i, 0)),
                  pl.BlockSpec((1, LANES), lambda b, i: (0, 0)),
                  pl.BlockSpec((1, LANES), lambda b, i: (0, 0))],
        out_specs=[pl.BlockSpec((1, tm, width), lambda b, i: (b, i, 0))] * n_out,
        out_shape=[jax.ShapeDtypeStruct((B, S, width), BF16)] * n_out,
        compiler_params=_params(("parallel", "parallel"), 32),
        name="qkv_prep",
    )(p, cos, sin_signed, g2(q_g), g2(k_g))
    return outs


def rope_tables(S, use_rope):
    if not use_rope:
        return jnp.ones((S, LANES), F32), jnp.zeros((S, LANES), F32)
    t = jnp.arange(S, dtype=I32)
    row = (t // GRID_W).astype(F32)[:, None]
    colp = (t % GRID_W).astype(F32)[:, None]
    nf = DIFF_HEAD_DIM // 4
    inv = ROPE_BASE ** (-jnp.arange(nf, dtype=F32) / nf)
    lane = jnp.arange(LANES)
    grp = (lane % DIFF_HEAD_DIM) // nf
    j = lane % nf
    pos = jnp.where((grp < 2)[None, :], row, colp)
    ang = pos * inv[j][None, :]
    sign = jnp.where((grp % 2 == 0)[None, :], -1.0, 1.0)
    return jnp.cos(ang), jnp.sin(ang) * sign


def _attn_kernel(lam_ref, q_ref, kc_ref, vc_ref, *rest, has_lat, out_scale, ck):
    if has_lat:
        kl_ref, vl_ref, g_ref, o_ref = rest
    else:
        g_ref, o_ref = rest
    lam = lam_ref[0]
    q = q_ref[0]
    lane = lax.broadcasted_iota(I32, q.shape, 1)
    nt = (((1,), (1,)), ((), ()))
    zero = jnp.zeros_like(q)
    qm = [jnp.where(lane < DIFF_HEAD_DIM, q, zero), jnp.where(lane >= DIFF_HEAD_DIM, q, zero)]
    chunks = [(kc_ref, vc_ref, 0, kc_ref.shape[1])]
    if has_lat:
        chunks += [(kl_ref, vl_ref, c * ck, ck) for c in range(kl_ref.shape[1] // ck)]

    def scores(ch):
        k = ch[0][0, ch[2]:ch[2] + ch[3], :]
        return [lax.dot_general(qm[mp], k, nt, preferred_element_type=F32) for mp in range(2)]

    m, acc = [None, None], [None, None]
    s_next = scores(chunks[0])
    for ci, ch in enumerate(chunks):
        s_cur = s_next
        if ci + 1 < len(chunks):
            s_next = scores(chunks[ci + 1])
        v = ch[1][0, ch[2]:ch[2] + ch[3], :]
        v1 = jnp.concatenate([v, jnp.ones_like(v)], axis=1)
        for mp in range(2):
            s = s_cur[mp]
            mx = jnp.max(s, axis=-1, keepdims=True)
            m_new = mx if ci == 0 else jnp.maximum(m[mp], mx)
            pv = _dot(jnp.exp2(s - m_new).astype(BF16), v1)
            acc[mp] = pv if ci == 0 else jnp.exp2(m[mp] - m_new) * acc[mp] + pv
            m[mp] = m_new
    o = acc[0][:, :LANES] / acc[0][:, LANES:] - lam * (acc[1][:, :LANES] / acc[1][:, LANES:])
    ms = jnp.mean(o * o, axis=-1, keepdims=True)
    o_ref[0] = ((o * lax.rsqrt(ms + NORM_EPS) * g_ref[...]) * out_scale).astype(BF16)


def diff_attention(lam, q, k_ctx, v_ctx, k_lat, v_lat, subln_g, out_scale, tq):
    B, Sq, W = q.shape
    H = W // LANES
    has_lat = k_lat is not None
    Sc = k_ctx.shape[1]
    head = lambda S: pl.BlockSpec((1, S, LANES), lambda b, h, i: (b, 0, h))
    in_specs = [pl.BlockSpec(memory_space=pltpu.SMEM),
                pl.BlockSpec((1, tq, LANES), lambda b, h, i: (b, i, h)), head(Sc), head(Sc)]
    args = [lam.reshape(1), q, k_ctx, v_ctx]
    if has_lat:
        in_specs += [head(k_lat.shape[1]), head(k_lat.shape[1])]
        args += [k_lat, v_lat]
    in_specs.append(pl.BlockSpec((1, LANES), lambda b, h, i: (0, 0)))
    args.append(subln_g.reshape(1, LANES))
    return pl.pallas_call(
        functools.partial(_attn_kernel, has_lat=has_lat, out_scale=out_scale,
                          ck=min(ATTN_KEY_CHUNK, k_lat.shape[1]) if has_lat else 0),
        grid=(B, H, Sq // tq),
        in_specs=in_specs,
        out_specs=pl.BlockSpec((1, tq, LANES), lambda b, h, i: (b, i, h)),
        out_shape=jax.ShapeDtypeStruct((B, Sq, W), BF16),
        compiler_params=_params(("parallel", "parallel", "parallel"), 56),
        name="diff_attention",
    )(*args)


def _oproj_kernel(hy_ref, o_ref, w_ref, h_ref, gate_ref, out_ref, *, half):
    y = _dot(hy_ref[0].astype(BF16), w_ref[0:half, :]) + _dot(o_ref[0], w_ref[half:, :])
    out_ref[0] = h_ref[0] + gate_ref[0] * y


def out_proj(hy, o, w_out, h, gate, tm):
    B, S, D = h.shape
    half = hy.shape[2]
    return pl.pallas_call(
        functools.partial(_oproj_kernel, half=half),
        grid=(B, S // tm),
        in_specs=[pl.BlockSpec((1, tm, half), lambda b, i: (b, i, 0)),
                  pl.BlockSpec((1, tm, half), lambda b, i: (b, i, 0)),
                  pl.BlockSpec((2 * half, D), lambda b, i: (0, 0)),
                  pl.BlockSpec((1, tm, D), lambda b, i: (b, i, 0)),
                  pl.BlockSpec((1, 1, D), lambda b, i: (b, 0, 0))],
        out_specs=pl.BlockSpec((1, tm, D), lambda b, i: (b, i, 0)),
        out_shape=jax.ShapeDtypeStruct((B, S, D), F32),
        compiler_params=_params(("parallel", "parallel"), 40),
        name="out_proj",
    )(hy, o, w_out, h, gate)


def _shift_rows(x, d):
    return pltpu.roll(x, (-d) % x.shape[0], 0)


def _pool_kernel(hp_ref, hc_ref, hn_ref, g_ref, sc_ref, sh_ref, gate_ref, pw_ref, ps_ref, o_ref, *, T, L):
    i = pl.program_id(1)
    nt = pl.num_programs(1)
    g, sc, sh = g_ref[...], sc_ref[0], sh_ref[0]
    hc = hc_ref[0]
    a_c = _norm_mod(hc, g, sc, sh)
    a_p = jnp.where(i == 0, 0.0, _norm_mod(hp_ref[0], g, sc, sh))
    a_n = jnp.where(i == nt - 1, 0.0, _norm_mod(hn_ref[0], g, sc, sh))
    ext = jnp.concatenate([a_p, a_c, a_n], axis=0)
    tok = i * T + lax.broadcasted_iota(I32, (T, 1), 0)
    G = ext.shape[1] // len(POOL_WINDOWS)
    ys = []
    for gi, w in enumerate(POOL_WINDOWS):
        xg = ext[:, gi * G:(gi + 1) * G]
        s = _shift_rows(xg, -1) + xg
        step = 1
        while 2 * step < w:
            s = _shift_rows(s, -step) + _shift_rows(s, step)
            step *= 2
        cnt = (jnp.minimum(tok + w // 2, L) - jnp.maximum(tok - w // 2, 0)).astype(F32)
        p = s[POOL_HALO:POOL_HALO + T] / cnt - a_c[:, gi * G:(gi + 1) * G]
        ys.append(_dot(p.astype(BF16), pw_ref[gi]))
    y = jnp.concatenate(ys, axis=1) * ps_ref[...]
    o_ref[0] = hc + gate_ref[0] * y


def pool_mixer(h, g, scale, shift, gate, pool_w, pool_scale, T):
    B, S, D = h.shape
    nh = T // POOL_HALO
    last = S // POOL_HALO - 1
    mod = pl.BlockSpec((1, 1, D), lambda b, i: (b, 0, 0))
    return pl.pallas_call(
        functools.partial(_pool_kernel, T=T, L=S),
        grid=(B, S // T),
        in_specs=[pl.BlockSpec((1, POOL_HALO, D), lambda b, i: (b, jnp.maximum(i * nh - 1, 0), 0)),
                  pl.BlockSpec((1, T, D), lambda b, i: (b, i, 0)),
                  pl.BlockSpec((1, POOL_HALO, D), lambda b, i: (b, jnp.minimum((i + 1) * nh, last), 0)),
                  pl.BlockSpec((1, D), lambda b, i: (0, 0)), mod, mod, mod,
                  pl.BlockSpec(pool_w.shape, lambda b, i: (0, 0, 0)),
                  pl.BlockSpec((1, D), lambda b, i: (0, 0))],
        out_specs=pl.BlockSpec((1, T, D), lambda b, i: (b, i, 0)),
        out_shape=jax.ShapeDtypeStruct((B, S, D), F32),
        compiler_params=_params(("parallel", "parallel"), 48),
        name="pool_mixer",
    )(h, h, h, g.reshape(1, D), scale, shift, gate, pool_w, pool_scale.reshape(1, D))


def _ffn_prep_kernel(h_ref, g_ref, sc_ref, sh_ref, rw_ref, x_ref, aff_ref, afft_ref, *, n_exp):
    a = _norm_mod(h_ref[0], g_ref[...], sc_ref[0], sh_ref[0]).astype(BF16)
    x_ref[0] = a
    logits = _dot(a, rw_ref[...])
    lane = lax.broadcasted_iota(I32, logits.shape, 1)
    valid = lane < n_exp
    mx = jnp.max(jnp.where(valid, logits, -jnp.inf), axis=-1, keepdims=True)
    e = jnp.where(valid, jnp.exp(logits - mx), 0.0)
    aff = e / jnp.sum(e, axis=-1, keepdims=True)
    aff_ref[0] = aff
    afft_ref[0] = aff.T[0:n_exp]


def ffn_prep(h, g, scale, shift, router_w, tm):
    B, S, D = h.shape
    n_exp = router_w.shape[1]
    rw = jnp.pad(router_w, ((0, 0), (0, LANES - n_exp))).astype(BF16)
    mod = pl.BlockSpec((1, 1, D), lambda b, i: (b, 0, 0))
    return pl.pallas_call(
        functools.partial(_ffn_prep_kernel, n_exp=n_exp),
        grid=(B, S // tm),
        in_specs=[pl.BlockSpec((1, tm, D), lambda b, i: (b, i, 0)),
                  pl.BlockSpec((1, D), lambda b, i: (0, 0)), mod, mod,
                  pl.BlockSpec((D, LANES), lambda b, i: (0, 0))],
        out_specs=[pl.BlockSpec((1, tm, D), lambda b, i: (b, i, 0)),
                   pl.BlockSpec((1, tm, LANES), lambda b, i: (b, i, 0)),
                   pl.BlockSpec((1, n_exp, tm), lambda b, i: (b, 0, i))],
        out_shape=[jax.ShapeDtypeStruct((B, S, D), BF16), jax.ShapeDtypeStruct((B, S, LANES), F32),
                   jax.ShapeDtypeStruct((B, n_exp, S), F32)],
        compiler_params=_params(("parallel", "parallel"), 32),
        name="ffn_prep",
    )(h, g.reshape(1, D), scale, shift, rw)


def _prefix_excl(m, tri, tb):
    S = m.shape[0]
    carry = jnp.zeros((1, m.shape[1]), F32)
    outs, carries = [], []
    for blk in range(S // tb):
        mb = m[blk * tb:(blk + 1) * tb]
        outs.append(_dot(tri, mb.astype(BF16)) + carry)
        carries.append(carry)
        carry = carry + jnp.sum(mb, axis=0, keepdims=True)
    carries.append(carry)
    return (jnp.concatenate(outs, axis=0) if len(outs) > 1 else outs[0]), jnp.concatenate(carries, axis=0)


def _select_kernel(aff_ref, afft_ref, tri_ref, pos_ref, post_ref, offs_ref, *, cap, tb, n_exp):
    bits = pltpu.bitcast(afft_ref[0], I32)

    def body(i, cur):
        cand = cur | (jnp.int32(1) << (30 - i))
        cnt = jnp.sum((bits >= cand).astype(F32), axis=1, keepdims=True)
        return jnp.where(cnt >= cap, cand, cur)

    thr_bits = lax.fori_loop(0, 31, body, jnp.zeros((n_exp, 1), I32))
    thr_col = pltpu.bitcast(jnp.broadcast_to(thr_bits, (n_exp, LANES)), F32)
    sub = lax.broadcasted_iota(I32, (n_exp, LANES), 0)
    lane = lax.broadcasted_iota(I32, (n_exp, LANES), 1)
    thr = jnp.sum(jnp.where(sub == lane, thr_col, 0.0), axis=0, keepdims=True)
    aff = aff_ref[0]
    tri = tri_ref[...]
    gt = (aff > thr).astype(F32)
    eq = (aff == thr).astype(F32)
    need = cap - jnp.sum(gt, axis=0, keepdims=True)
    sel = gt + eq * (_prefix_excl(eq, tri, tb)[0] < need).astype(F32)
    slot, offs = _prefix_excl(sel, tri, tb)
    sp = jnp.where(sel > 0.0, slot, -1.0)
    pos_ref[0] = sp
    offs_ref[0] = offs
    for blk in range(sp.shape[0] // tb):
        post_ref[0, :, blk * tb:(blk + 1) * tb] = sp[blk * tb:(blk + 1) * tb].T[0:n_exp]


def select_tokens(aff, afft, cap):
    B, S, _ = aff.shape
    n_exp = afft.shape[1]
    tb = min(S, 256)
    r = lax.broadcasted_iota(I32, (tb, tb), 0)
    c = lax.broadcasted_iota(I32, (tb, tb), 1)
    tri = (c < r).astype(BF16)
    nblk = S // tb
    pos, post, offs = pl.pallas_call(
        functools.partial(_select_kernel, cap=cap, tb=tb, n_exp=n_exp),
        grid=(B,),
        in_specs=[pl.BlockSpec((1, S, LANES), lambda b: (b, 0, 0)),
                  pl.BlockSpec((1, n_exp, S), lambda b: (b, 0, 0)),
                  pl.BlockSpec((tb, tb), lambda b: (0, 0))],
        out_specs=[pl.BlockSpec((1, S, LANES), lambda b: (b, 0, 0)),
                   pl.BlockSpec((1, n_exp, S), lambda b: (b, 0, 0)),
                   pl.BlockSpec((1, nblk + 1, LANES), lambda b: (b, 0, 0))],
        out_shape=[jax.ShapeDtypeStruct((B, S, LANES), F32), jax.ShapeDtypeStruct((B, n_exp, S), F32),
                   jax.ShapeDtypeStruct((B, nblk + 1, LANES), F32)],
        compiler_params=_params(("parallel",), 48),
        name="select_tokens",
    )(aff, afft, tri)
    offs = jnp.swapaxes(offs[:, :, :n_exp], 1, 2).astype(I32).reshape(-1)
    return pos, post, offs, tb


def _gather_kernel(offs_ref, x_ref, post_ref, post4_ref, afft_ref, afft4_ref, o_ref, g_ref, *, slots, rc, tb, nblk, n_exp, win):
    b = pl.program_id(0)
    e = pl.program_id(1)
    S = x_ref.shape[1]
    base = (b * n_exp + e) * (nblk + 1)
    for k in range(slots // rc):
        rows = slice(k * rc, (k + 1) * rc)

        def full():
            r = lax.broadcasted_iota(I32, (rc, S), 0).astype(F32) + float(k * rc)
            hit = r == post_ref[0, pl.ds(e, 1), :]
            o_ref[0, 0, rows, :] = _dot(hit.astype(BF16), x_ref[0]).astype(BF16)
            g = jnp.sum(jnp.where(hit, afft_ref[0, pl.ds(e, 1), :], 0.0), axis=1, keepdims=True)
            g_ref[0, 0, rows, :] = jnp.broadcast_to(g, (rc, LANES))

        if win >= S:
            full()
            continue
        i0 = jnp.int32(0)
        i1 = jnp.int32(0)
        for i in range(nblk):
            i0 += (offs_ref[base + i + 1] <= k * rc).astype(I32)
            i1 += (offs_ref[base + i] < (k + 1) * rc).astype(I32)
        narrow = (i1 - i0) * tb <= win
        wb = win // tb

        @pl.when(narrow)
        def _():
            j0 = jnp.minimum(i0, nblk - wb)
            r = lax.broadcasted_iota(I32, (rc, tb), 0).astype(F32) + float(k * rc)
            pieces = []
            gacc = jnp.zeros((rc, tb), F32)
            for j in range(wb):
                hit = r == post4_ref[0, pl.ds(e, 1), pl.ds(j0 + j, 1), :].reshape(1, tb)
                pieces.append(hit.astype(BF16))
                gacc = gacc + jnp.where(hit, afft4_ref[0, pl.ds(e, 1), pl.ds(j0 + j, 1), :].reshape(1, tb), 0.0)
            xw = x_ref[0, pl.ds(pl.multiple_of(j0 * tb, tb), win), :]
            o_ref[0, 0, rows, :] = _dot(jnp.concatenate(pieces, axis=1), xw).astype(BF16)
            g_ref[0, 0, rows, :] = jnp.broadcast_to(jnp.sum(gacc, axis=1, keepdims=True), (rc, LANES))

        pl.when(jnp.logical_not(narrow))(full)


def gather_tokens(x, post, afft, offs, slots, tb):
    B, S, D = x.shape
    E = post.shape[1]
    nblk = S // tb
    rc = min(slots, MOE_ROW_CHUNK)
    win = min(S, GATHER_WINDOW * rc)
    row = pl.BlockSpec((1, E, S), lambda b, e, offs: (b, 0, 0))
    row4 = pl.BlockSpec((1, E, nblk, tb), lambda b, e, offs: (b, 0, 0, 0))
    return pl.pallas_call(
        functools.partial(_gather_kernel, slots=slots, rc=rc, tb=tb, nblk=nblk, n_exp=E, win=win),
        grid_spec=pltpu.PrefetchScalarGridSpec(
            num_scalar_prefetch=1,
            grid=(B, E),
            in_specs=[pl.BlockSpec((1, S, D), lambda b, e, offs: (b, 0, 0)), row, row4, row, row4],
            out_specs=[pl.BlockSpec((1, 1, slots, D), lambda b, e, offs: (e, b, 0, 0)),
                       pl.BlockSpec((1, 1, slots, LANES), lambda b, e, offs: (e, b, 0, 0))]),
        out_shape=[jax.ShapeDtypeStruct((E, B, slots, D), BF16), jax.ShapeDtypeStruct((E, B, slots, LANES), F32)],
        compiler_params=_params(("parallel", "arbitrary"), 48),
        name="gather_tokens",
    )(offs, x, post, post.reshape(B, E, nblk, tb), afft, afft.reshape(B, E, nblk, tb))


def _ffn_kernel(xs_ref, g_ref, wg_ref, wu_ref, wd_ref, o_ref, acc_ref, *, nf):
    f = pl.program_id(2)
    x = xs_ref[0]
    a = _dot(x, wg_ref[...].astype(BF16))
    u = _dot(x, wu_ref[...].astype(BF16))
    hm = (a * jax.nn.sigmoid(a) * u).astype(BF16)
    part = _dot(hm, wd_ref[...].astype(BF16))

    @pl.when(f == 0)
    def _():
        acc_ref[...] = part

    @pl.when(f > 0)
    def _():
        acc_ref[...] += part

    @pl.when(f == nf - 1)
    def _():
        o_ref[0] = (acc_ref[...] * g_ref[0, :, 0:1]).astype(BF16)


def expert_ffn(xs, gs, w_gate, w_up, w_down, layer, tm, tf):
    E, M, D = xs.shape
    F = w_gate.shape[3]
    nf = F // tf
    return pl.pallas_call(
        functools.partial(_ffn_kernel, nf=nf),
        grid=(E, M // tm, nf),
        in_specs=[pl.BlockSpec((1, tm, D), lambda e, m, f: (e, m, 0)),
                  pl.BlockSpec((1, tm, LANES), lambda e, m, f: (e, m, 0)),
                  pl.BlockSpec((None, None, D, tf), lambda e, m, f: (layer, e, 0, f)),
                  pl.BlockSpec((None, None, D, tf), lambda e, m, f: (layer, e, 0, f)),
                  pl.BlockSpec((None, None, tf, D), lambda e, m, f: (layer, e, f, 0))],
        out_specs=pl.BlockSpec((1, tm, D), lambda e, m, f: (e, m, 0)),
        out_shape=jax.ShapeDtypeStruct((E, M, D), BF16),
        scratch_shapes=[pltpu.VMEM((tm, D), F32)],
        compiler_params=_params(("parallel", "parallel", "arbitrary"), 56),
        name="expert_ffn",
    )(xs, gs, w_gate, w_up, w_down)


def _combine_kernel(offs_ref, y_ref, pos_ref, h_ref, gate_ref, o_ref, ybuf_ref, *, nblk, n_exp, win):
    b = pl.program_id(0)
    t = pl.program_id(1)
    T = pos_ref.shape[1]
    slots = y_ref.shape[2]

    def full():
        if slots < LANES:
            K = n_exp * slots
            col = lax.broadcasted_iota(I32, (LANES, K), 1).astype(F32)
            owner = jnp.floor((col + 0.5) * (1.0 / slots))
            rep = owner == lax.broadcasted_iota(I32, (LANES, K), 0).astype(F32)
            slot_rep = _dot(pos_ref[0].astype(BF16), rep.astype(BF16))
            want = (col - owner * slots)[0:1]
            acc = _dot((slot_rep == want).astype(BF16), y_ref[:, 0].reshape(K, y_ref.shape[3]))
        else:
            r = lax.broadcasted_iota(I32, (T, slots), 1).astype(F32)
            acc = None
            for e in range(n_exp):
                part = _dot((pos_ref[0, :, e:e + 1] == r).astype(BF16), y_ref[e, 0])
                acc = part if acc is None else acc + part
        o_ref[0] = h_ref[0] + gate_ref[0] * acc

    if win >= slots:
        full()
        return
    starts = []
    narrow = None
    for e in range(n_exp):
        base = (b * n_exp + e) * (nblk + 1) + t
        lo = offs_ref[base]
        hi = offs_ref[base + 1]
        a0 = jnp.minimum((lo // BF16_ROWS) * BF16_ROWS, slots - win)
        ok = hi - a0 <= win
        narrow = ok if narrow is None else jnp.logical_and(narrow, ok)
        starts.append(a0)

    @pl.when(narrow)
    def _():
        r = lax.broadcasted_iota(I32, (T, win), 1).astype(F32)
        pieces = []
        for e in range(n_exp):
            a0 = starts[e]
            ybuf_ref[e * win:(e + 1) * win, :] = y_ref[e, 0, pl.ds(pl.multiple_of(a0, BF16_ROWS), win), :]
            pieces.append((pos_ref[0, :, e:e + 1] - a0.astype(F32) == r).astype(BF16))
        o_ref[0] = h_ref[0] + gate_ref[0] * _dot(jnp.concatenate(pieces, axis=1), ybuf_ref[...])

    pl.when(jnp.logical_not(narrow))(full)


def combine_tokens(y, pos, offs, h, gate, T):
    E, B, slots, D = y.shape
    S = h.shape[1]
    win = min(slots, COMBINE_WINDOW)
    return pl.pallas_call(
        functools.partial(_combine_kernel, nblk=S // T, n_exp=E, win=win),
        grid_spec=pltpu.PrefetchScalarGridSpec(
            num_scalar_prefetch=1,
            grid=(B, S // T),
            in_specs=[pl.BlockSpec((E, 1, slots, D), lambda b, t, offs: (0, b, 0, 0)),
                      pl.BlockSpec((1, T, LANES), lambda b, t, offs: (b, t, 0)),
                      pl.BlockSpec((1, T, D), lambda b, t, offs: (b, t, 0)),
                      pl.BlockSpec((1, 1, D), lambda b, t, offs: (b, 0, 0))],
            out_specs=pl.BlockSpec((1, T, D), lambda b, t, offs: (b, t, 0)),
            scratch_shapes=[pltpu.VMEM((E * win, D), BF16)]),
        out_shape=jax.ShapeDtypeStruct((B, S, D), F32),
        compiler_params=_params(("parallel", "parallel"), 56),
        name="combine_tokens",
    )(offs, y, pos, h, gate)


def moe_block(h, g, scale, shift, gate, router_w, w_gate, w_up, w_down, layer):
    B, S, D = h.shape
    E = router_w.shape[1]
    cap = EC_CAPACITY * S // E
    slots = cap
    x, aff, afft = ffn_prep(h, g, scale, shift, router_w, min(S, 512))
    pos, post, offs, tb = select_tokens(aff, afft, cap)
    xs, gs = gather_tokens(x, post, afft, offs, slots, tb)
    M = B * slots
    y = expert_ffn(xs.reshape(E, M, D), gs.reshape(E, M, LANES), w_gate, w_up, w_down, layer, min(M, 1024), 512)
    return combine_tokens(y.reshape(E, B, slots, D), pos, offs, h, gate, tb)


def kernel(x, c, ctx, c_ctx, ada_w, ada_b, norm_mix_g, norm_ffn_g, w_in, w_out, hy_conv_w, hy_conv_b, hy_f_w1, hy_f_b1, hy_f_w2, hy_f_b2, hy_f_w3, hy_f_b3, hy_f_freq, hy_f_wout, hy_bias, q_norm_g, k_norm_g, diff_lambda, subln_g, pool_w, pool_scale, router_w, exp_w_gate, exp_w_up, exp_w_down):
    B, S, D = x.shape
    Lc = ctx.shape[1]
    depth = ada_w.shape[0]
    hy_width = hy_bias.shape[2]
    hy_proj = (HYENA_ORDER + 1) * hy_width
    qk_width = DIFF_HEADS * 2 * DIFF_HEAD_DIM
    last_attn = ((depth - 1) // 2) * 2

    s_all = jnp.concatenate([jax.nn.silu(c), jax.nn.silu(c_ctx)[None, :]], axis=0)
    rows = -(-s_all.shape[0] // SUBLANES) * SUBLANES
    mods = ada_modulation(jnp.pad(s_all, ((0, rows - s_all.shape[0]), (0, 0))), ada_w, ada_b)

    nblk_lat = max(1, S // HYENA_BLOCK)
    tables_lat = dft_tables(S // nblk_lat)
    tables_ctx = dft_tables(Lc)
    rope_lat = rope_tables(S, True)
    rope_ctx = rope_tables(Lc, False)

    h, hc = x, ctx
    for l in range(depth):
        m = [mods[l, :B, i * D:(i + 1) * D].reshape(B, 1, D) for i in range(6)]
        mc = [jnp.broadcast_to(mods[l, B, i * D:(i + 1) * D].reshape(1, 1, D), (B, 1, D)) for i in range(6)]
        ctx_full = l < last_attn
        if l % 2 == 0:
            e = l // 2
            lam_init = 0.8 - 0.6 * math.exp(-0.3 * l)
            lv = diff_lambda[e]
            lam = jnp.exp(jnp.sum(lv[0] * lv[1])) - jnp.exp(jnp.sum(lv[2] * lv[3])) + lam_init
            filt = (hy_f_w1[e], hy_f_b1[e], hy_f_w2[e], hy_f_b2[e], hy_f_w3[e], hy_f_b3[e], hy_f_freq[e], hy_f_wout[e])
            w_in_b = w_in[e].astype(BF16)
            w_out_b = w_out[e].astype(BF16)

            p = norm_mod_matmul(h, norm_mix_g[l], m[1], m[0], w_in_b, 512)
            gts = short_conv(p, hy_conv_w[e], hy_conv_b[e], 0, 2 * hy_width, 256, F32)
            hv = short_conv(p, hy_conv_w[e], hy_conv_b[e], 2 * hy_width, hy_width, 256, BF16)
            hy = hyena_operator(gts, hv, tables_lat, *hyena_filter_taps(S, *filt, hy_width), hy_bias[e], nblk_lat, 256)
            q, k, v = qkv_prep(p, 1, True, *rope_lat, q_norm_g[e], k_norm_g[e], 512)

            if l <= last_attn:
                if ctx_full:
                    pc = norm_mod_matmul(hc, norm_mix_g[l], mc[1], mc[0], w_in_b, Lc)
                    qc, kc, vc = qkv_prep(pc, 1, True, *rope_ctx, q_norm_g[e], k_norm_g[e], Lc)
                else:
                    pc = norm_mod_matmul(hc, norm_mix_g[l], mc[1], mc[0], w_in_b[:, hy_proj + qk_width:], Lc)
                    kc, vc = qkv_prep(pc, 0, False, *rope_ctx, q_norm_g[e], k_norm_g[e], Lc)
            o = diff_attention(lam, q, kc, vc, k, v, subln_g[e], 1.0 - lam_init, min(S, 512))
            h = out_proj(hy, o, w_out_b, h, m[2], 512)
            if ctx_full:
                gtc = short_conv(pc, hy_conv_w[e], hy_conv_b[e], 0, 2 * hy_width, 256, F32)
                hvc = short_conv(pc, hy_conv_w[e], hy_conv_b[e], 2 * hy_width, hy_width, 256, BF16)
                hyc = hyena_operator(gtc, hvc, tables_ctx, *hyena_filter_taps(Lc, *filt, hy_width), hy_bias[e], 1, Lc)
                oc = diff_attention(lam, qc, kc, vc, None, None, subln_g[e], 1.0 - lam_init, Lc)
                hc = out_proj(hyc, oc, w_out_b, hc, mc[2], Lc)
        else:
            o_idx = l // 2
            pw = pool_w[o_idx].astype(BF16)
            h = pool_mixer(h, norm_mix_g[l], m[1], m[0], m[2], pw, pool_scale[o_idx], 512)
            if ctx_full:
                hc = pool_mixer(hc, norm_mix_g[l], mc[1], mc[0], mc[2], pw, pool_scale[o_idx], Lc)
        h = moe_block(h, norm_ffn_g[l], m[4], m[3], m[5], router_w[l], exp_w_gate, exp_w_up, exp_w_down, l)
        if ctx_full:
            hc = moe_block(hc, norm_ffn_g[l], mc[4], mc[3], mc[5], router_w[l], exp_w_gate, exp_w_up, exp_w_down, l)
    return h
```

```python
import functools
import math

import jax
import jax.numpy as jnp
from jax import lax
from jax.experimental import pallas as pl
from jax.experimental.pallas import tpu as pltpu

F32 = jnp.float32
BF16 = jnp.bfloat16
I32 = jnp.int32

NORM_EPS = 1e-6
GRID_W = 64
HYENA_ORDER = 2
FILTER_EMB = 33
FILTER_FAST_DECAY = 0.3
FILTER_SLOW_DECAY = 1.5
FILTER_TARGET = 1e-2
DIFF_HEADS = 4
DIFF_HEAD_DIM = 64
ROPE_BASE = 10000.0
POOL_WINDOWS = (2, 4, 8, 16)
EC_CAPACITY = 2
LANES = 128
SUBLANES = 8
POOL_HALO = 8
LOG2E = 1.4426950408889634
ATTN_KEY_CHUNK = 1024
GATHER_WINDOW = 80
COMBINE_WINDOW = 128
BF16_ROWS = 16
HYENA_BLOCK = 1024


def _params(sem, vmem_mb):
    return pltpu.CompilerParams(dimension_semantics=sem, vmem_limit_bytes=vmem_mb * 1024 * 1024)


def _dot(a, b):
    return jnp.dot(a, b, preferred_element_type=F32)


def _norm_mod(x, g, scale, shift):
    ms = jnp.mean(x * x, axis=-1, keepdims=True)
    return (x * lax.rsqrt(ms + NORM_EPS) * g) * (1.0 + scale) + shift


def _mm_kernel(a_ref, b_ref, o_ref, acc_ref, *, nk):
    k = pl.program_id(2)
    part = _dot(a_ref[...].astype(BF16), b_ref[...].astype(BF16))

    @pl.when(k == 0)
    def _():
        acc_ref[...] = part

    @pl.when(k > 0)
    def _():
        acc_ref[...] += part

    @pl.when(k == nk - 1)
    def _():
        o_ref[...] = acc_ref[...].astype(o_ref.dtype)


def matmul(a, b, tm, tn, tk):
    M, K = a.shape
    N = b.shape[1]
    nk = K // tk
    return pl.pallas_call(
        functools.partial(_mm_kernel, nk=nk),
        grid=(M // tm, N // tn, nk),
        in_specs=[pl.BlockSpec((tm, tk), lambda i, j, k: (i, k)),
                  pl.BlockSpec((tk, tn), lambda i, j, k: (k, j))],
        out_specs=pl.BlockSpec((tm, tn), lambda i, j, k: (i, j)),
        out_shape=jax.ShapeDtypeStruct((M, N), F32),
        scratch_shapes=[pltpu.VMEM((tm, tn), F32)],
        compiler_params=_params(("parallel", "parallel", "arbitrary"), 40),
        name="matmul",
    )(a, b)


def _ada_kernel(s_ref, w_ref, b_ref, o_ref):
    o_ref[...] = _dot(s_ref[...].astype(BF16), w_ref[...].astype(BF16)) + b_ref[...]


def ada_modulation(s, ada_w, ada_b):
    depth, D, N = ada_w.shape
    R = s.shape[0]
    tn = 1024
    return pl.pallas_call(
        _ada_kernel,
        grid=(depth, N // tn),
        in_specs=[pl.BlockSpec((R, D), lambda l, j: (0, 0)),
                  pl.BlockSpec((None, D, tn), lambda l, j: (l, 0, j)),
                  pl.BlockSpec((None, 1, tn), lambda l, j: (l, 0, j))],
        out_specs=pl.BlockSpec((None, R, tn), lambda l, j: (l, 0, j)),
        out_shape=jax.ShapeDtypeStruct((depth, R, N), F32),
        compiler_params=_params(("parallel", "parallel"), 32),
        name="ada_modulation",
    )(s, ada_w, ada_b.reshape(depth, 1, N))


def _nmm_kernel(h_ref, g_ref, sc_ref, sh_ref, w_ref, o_ref):
    a = _norm_mod(h_ref[0], g_ref[...], sc_ref[0], sh_ref[0])
    o_ref[0] = _dot(a.astype(BF16), w_ref[...])


def norm_mod_matmul(h, g, scale, shift, w, tm):
    B, S, D = h.shape
    N = w.shape[1]
    return pl.pallas_call(
        _nmm_kernel,
        grid=(B, S // tm),
        in_specs=[pl.BlockSpec((1, tm, D), lambda b, i: (b, i, 0)),
                  pl.BlockSpec((1, D), lambda b, i: (0, 0)),
                  pl.BlockSpec((1, 1, D), lambda b, i: (b, 0, 0)),
                  pl.BlockSpec((1, 1, D), lambda b, i: (b, 0, 0)),
                  pl.BlockSpec((D, N), lambda b, i: (0, 0))],
        out_specs=pl.BlockSpec((1, tm, N), lambda b, i: (b, i, 0)),
        out_shape=jax.ShapeDtypeStruct((B, S, N), F32),
        compiler_params=_params(("parallel", "parallel"), 48),
        name="norm_mod_matmul",
    )(h, g.reshape(1, D), scale, shift, w)


def _sconv_kernel(p_ref, w_ref, b_ref, o_ref):
    x = p_ref[0]
    S = x.shape[0]
    row = lax.broadcasted_iota(I32, x.shape, 0)
    xm = jnp.where(row == 0, 0.0, pltpu.roll(x, 1, 0))
    xp = jnp.where(row == S - 1, 0.0, pltpu.roll(x, S - 1, 0))
    w = w_ref[...]
    o_ref[0] = (xm * w[0:1] + x * w[1:2] + xp * w[2:3] + b_ref[...]).astype(o_ref.dtype)


def short_conv(p, conv_w, conv_b, col0, width, tc, out_dtype):
    B, S, _ = p.shape
    c0 = col0 // tc
    return pl.pallas_call(
        _sconv_kernel,
        grid=(B, width // tc),
        in_specs=[pl.BlockSpec((1, S, tc), lambda b, c: (b, 0, c + c0)),
                  pl.BlockSpec((3, tc), lambda b, c: (0, c + c0)),
                  pl.BlockSpec((1, tc), lambda b, c: (0, c + c0))],
        out_specs=pl.BlockSpec((1, S, tc), lambda b, c: (b, 0, c)),
        out_shape=jax.ShapeDtypeStruct((B, S, width), out_dtype),
        compiler_params=_params(("parallel", "parallel"), 48),
        name="short_conv",
    )(p, conv_w, conv_b.reshape(1, -1))


def _dft_fwd_kernel(c_ref, s_ref, u_ref, kre_ref, kim_ref, y_ref, *, nblk):
    c = c_ref[...]
    s = s_ref[...]
    C = u_ref.shape[3]
    ure, uim = [], []
    for j in range(nblk):
        u = u_ref[0, j]
        ure.append(_dot(c, u))
        uim.append(-_dot(s, u))
    for i in range(nblk):
        yre = yim = None
        for j in range(nblk):
            d = i - j + nblk - 1
            kre = kre_ref[:, d * C:(d + 1) * C]
            kim = kim_ref[:, d * C:(d + 1) * C]
            tre = kre * ure[j] - kim * uim[j]
            tim = kre * uim[j] + kim * ure[j]
            yre = tre if yre is None else yre + tre
            yim = tim if yim is None else yim + tim
        y_ref[0, i, 0] = yre.astype(BF16)
        y_ref[0, i, 1] = yim.astype(BF16)


def dft_forward(cf, sf, u, kre, kim, order, tf):
    B, nblk, Lb, C = u.shape
    nd = 2 * nblk - 1
    return pl.pallas_call(
        functools.partial(_dft_fwd_kernel, nblk=nblk),
        grid=(Lb // tf, B),
        in_specs=[pl.BlockSpec((tf, Lb), lambda f, b: (f, 0)),
                  pl.BlockSpec((tf, Lb), lambda f, b: (f, 0)),
                  pl.BlockSpec((1, nblk, Lb, C), lambda f, b: (b, 0, 0, 0)),
                  pl.BlockSpec((tf, nd * C), lambda f, b: (f, order)),
                  pl.BlockSpec((tf, nd * C), lambda f, b: (f, order))],
        out_specs=pl.BlockSpec((1, nblk, 2, tf, C), lambda f, b: (b, 0, 0, f, 0)),
        out_shape=jax.ShapeDtypeStruct((B, nblk, 2, Lb, C), BF16),
        compiler_params=_params(("parallel", "parallel"), 56),
        name="dft_forward",
    )(cf, sf, u, kre, kim)


def _dft_inv_kernel(ct_ref, st_ref, y_ref, gate_ref, o_ref, *, scale):
    acc = _dot(ct_ref[...], y_ref[0, 0, 0]) - _dot(st_ref[...], y_ref[0, 0, 1])
    o_ref[0, 0] = (gate_ref[0, 0] * (acc * scale)).astype(BF16)


def dft_inverse(ct, st, y, gates, gate_col):
    B, nblk, _, Lb, C = y.shape
    return pl.pallas_call(
        functools.partial(_dft_inv_kernel, scale=1.0 / Lb),
        grid=(B, nblk),
        in_specs=[pl.BlockSpec((Lb, Lb), lambda b, i: (0, 0)),
                  pl.BlockSpec((Lb, Lb), lambda b, i: (0, 0)),
                  pl.BlockSpec((1, 1, 2, Lb, C), lambda b, i: (b, i, 0, 0, 0)),
                  pl.BlockSpec((1, 1, Lb, C), lambda b, i: (b, i, 0, gate_col))],
        out_specs=pl.BlockSpec((1, 1, Lb, C), lambda b, i: (b, i, 0, 0)),
        out_shape=jax.ShapeDtypeStruct((B, nblk, Lb, C), BF16),
        compiler_params=_params(("parallel", "parallel"), 48),
        name="dft_inverse",
    )(ct, st, y, gates)


def dft_tables(L):
    n = 2 * L
    f = lax.broadcasted_iota(I32, (L, L), 0)
    t = lax.broadcasted_iota(I32, (L, L), 1)
    m = ((2 * f + 1) * t) % (2 * n)
    ang = m.astype(F32) * (math.pi / n)
    cf = jnp.cos(ang)
    sf = jnp.sin(ang)
    return cf.astype(BF16), sf.astype(BF16), cf.T.astype(BF16), sf.T.astype(BF16)


def hyena_filter_taps(L, w1, b1, w2, b2, w3, b3, freq, wout, width):
    hp = lax.Precision.HIGHEST
    t = jnp.linspace(0.0, 1.0, L, dtype=F32)[:, None]
    bands = (FILTER_EMB - 1) // 2
    w = 2.0 * math.pi * jnp.arange(L, dtype=F32)[:, None] / L
    f = jnp.linspace(1e-4, bands - 1, bands, dtype=F32)[None, :]
    z = jnp.concatenate([t, jnp.cos(f * w), -jnp.sin(f * w)], axis=-1)
    h = jnp.sin(freq * (jnp.dot(z, w1, precision=hp) + b1))
    h = jnp.sin(freq * (jnp.dot(h, w2, precision=hp) + b2))
    h = jnp.sin(freq * (jnp.dot(h, w3, precision=hp) + b3))
    max_decay = math.log(FILTER_TARGET) / FILTER_FAST_DECAY
    min_decay = math.log(FILTER_TARGET) / FILTER_SLOW_DECAY
    deltas = jnp.abs(jnp.linspace(min_decay, max_decay, width, dtype=F32))

    def taps(hh, tt):
        return jnp.dot(hh, wout, precision=hp).reshape(L, HYENA_ORDER, 2, width) * jnp.exp(-tt * deltas[None, :])[:, None, None, :]

    return taps(h, t), taps(h[::-1], t[::-1])


def filter_spectra(cf, sf, taps, taps_rev, bias, nblk):
    L, C = taps.shape[0], taps.shape[3]
    Lb = L // nblk
    zero = jnp.zeros((1, C), F32)
    sums, diffs = [], []
    for o in range(HYENA_ORDER):
        h_fwd, h_bwd = taps[:, o, 0], taps[:, o, 1]
        h_fwd_rev, h_bwd_rev = taps_rev[:, o, 0], taps_rev[:, o, 1]
        k0 = (h_fwd[0] + h_bwd[0] + bias[o])[None, :]
        kfull = jnp.concatenate([zero, h_bwd_rev[:L - 1], k0, h_fwd[1:]], axis=0)
        krev = jnp.concatenate([h_fwd_rev[:L - 1], k0, h_bwd[1:], zero], axis=0)
        for dlt in range(-(nblk - 1), nblk):
            base = L + dlt * Lb
            kp = kfull[base:base + Lb]
            km = jnp.concatenate([zero, krev[2 * L - base:2 * L - base + Lb - 1]], axis=0)
            sums.append(kp + km)
            diffs.append(km - kp)
    tmm = min(Lb, 512)
    kre = matmul(cf, jnp.concatenate(sums, axis=1), tmm, tmm, tmm)
    kim = matmul(sf, jnp.concatenate(diffs, axis=1), tmm, tmm, tmm)
    return kre, kim


def hyena_operator(gates, v, tables, taps, taps_rev, bias, nblk, tf):
    cf, sf, ct, st = tables
    B, L, C = v.shape
    Lb = L // nblk
    kre, kim = filter_spectra(cf, sf, taps, taps_rev, bias, nblk)
    gates = gates.reshape(B, nblk, Lb, 2 * C)
    z = v.reshape(B, nblk, Lb, C)
    for o in range(HYENA_ORDER):
        y = dft_forward(cf, sf, z, kre, kim, o, tf)
        z = dft_inverse(ct, st, y, gates, o)
    return z.reshape(B, L, C)


def _head_norm_rope(x, g, cos, sin_signed, scale):
    lane = lax.broadcasted_iota(I32, x.shape, 1)
    lo = lane < DIFF_HEAD_DIM
    x2 = x * x
    s_lo = jnp.sum(jnp.where(lo, x2, 0.0), axis=-1, keepdims=True)
    s_hi = jnp.sum(jnp.where(lo, 0.0, x2), axis=-1, keepdims=True)
    ms = jnp.where(lo, s_lo, s_hi) * (1.0 / DIFF_HEAD_DIM)
    xn = x * lax.rsqrt(ms + NORM_EPS) * g
    first = (lane & 16) == 0
    partner = jnp.where(first, pltpu.roll(xn, LANES - 16, 1), pltpu.roll(xn, 16, 1))
    return (xn * cos + partner * sin_signed) * scale


def _qkv_kernel(p_ref, cos_ref, sin_ref, qg_ref, kg_ref, *o_refs, has_q, width):
    cos = cos_ref[...]
    sin = sin_ref[...]
    x = p_ref[0]
    col = 0
    outs = list(o_refs)
    if has_q:
        q_ref = outs.pop(0)
        for hd in range(width // LANES):
            sl = slice(col + hd * LANES, col + (hd + 1) * LANES)
            q_ref[0, :, hd * LANES:(hd + 1) * LANES] = _head_norm_rope(
                x[:, sl], qg_ref[...], cos, sin, LOG2E * DIFF_HEAD_DIM ** -0.5).astype(BF16)
        col += width
    k_ref, v_ref = outs
    for hd in range(width // LANES):
        sl = slice(col + hd * LANES, col + (hd + 1) * LANES)
        k_ref[0, :, hd * LANES:(hd + 1) * LANES] = _head_norm_rope(x[:, sl], kg_ref[...], cos, sin, 1.0).astype(BF16)
    col += width
    v_ref[0] = x[:, col:col + width].astype(BF16)


def qkv_prep(p, col_block, has_q, cos, sin_signed, q_g, k_g, tm):
    B, S, _ = p.shape
    width = DIFF_HEADS * 2 * DIFF_HEAD_DIM
    n_out = 3 if has_q else 2
    g2 = lambda g: jnp.concatenate([g, g]).reshape(1, LANES)
    outs = pl.pallas_call(
        functools.partial(_qkv_kernel, has_q=has_q, width=width),
        grid=(B, S // tm),
        in_specs=[pl.BlockSpec((1, tm, n_out * width), lambda b, i: (b, i, col_block)),
                  pl.BlockSpec((tm, LANES), lambda b, i: (i, 0)),
                  pl.BlockSpec((tm, LANES), lambda b, i: (i, 0)),
                  pl.BlockSpec((1, LANES), lambda b, i: (0, 0)),
                  pl.BlockSpec((1, LANES), lambda b, i: (0, 0))],
        out_specs=[pl.BlockSpec((1, tm, width), lambda b, i: (b, i, 0))] * n_out,
        out_shape=[jax.ShapeDtypeStruct((B, S, width), BF16)] * n_out,
        compiler_params=_params(("parallel", "parallel"), 32),
        name="qkv_prep",
    )(p, cos, sin_signed, g2(q_g), g2(k_g))
    return outs


def rope_tables(S, use_rope):
    if not use_rope:
        return jnp.ones((S, LANES), F32), jnp.zeros((S, LANES), F32)
    t = jnp.arange(S, dtype=I32)
    row = (t // GRID_W).astype(F32)[:, None]
    colp = (t % GRID_W).astype(F32)[:, None]
    nf = DIFF_HEAD_DIM // 4
    inv = ROPE_BASE ** (-jnp.arange(nf, dtype=F32) / nf)
    lane = jnp.arange(LANES)
    grp = (lane % DIFF_HEAD_DIM) // nf
    j = lane % nf
    pos = jnp.where((grp < 2)[None, :], row, colp)
    ang = pos * inv[j][None, :]
    sign = jnp.where((grp % 2 == 0)[None, :], -1.0, 1.0)
    return jnp.cos(ang), jnp.sin(ang) * sign


def _attn_kernel(lam_ref, q_ref, kc_ref, vc_ref, *rest, has_lat, out_scale, ck):
    if has_lat:
        kl_ref, vl_ref, g_ref, o_ref = rest
    else:
        g_ref, o_ref = rest
    lam = lam_ref[0]
    q = q_ref[0]
    lane = lax.broadcasted_iota(I32, q.shape, 1)
    nt = (((1,), (1,)), ((), ()))
    zero = jnp.zeros_like(q)
    qm = [jnp.where(lane < DIFF_HEAD_DIM, q, zero), jnp.where(lane >= DIFF_HEAD_DIM, q, zero)]
    chunks = [(kc_ref, vc_ref, 0, kc_ref.shape[1])]
    if has_lat:
        chunks += [(kl_ref, vl_ref, c * ck, ck) for c in range(kl_ref.shape[1] // ck)]

    def scores(ch):
        k = ch[0][0, ch[2]:ch[2] + ch[3], :]
        return [lax.dot_general(qm[mp], k, nt, preferred_element_type=F32) for mp in range(2)]

    m, acc = [None, None], [None, None]
    s_next = scores(chunks[0])
    for ci, ch in enumerate(chunks):
        s_cur = s_next
        if ci + 1 < len(chunks):
            s_next = scores(chunks[ci + 1])
        v = ch[1][0, ch[2]:ch[2] + ch[3], :]
        v1 = jnp.concatenate([v, jnp.ones_like(v)], axis=1)
        for mp in range(2):
            s = s_cur[mp]
            mx = jnp.max(s, axis=-1, keepdims=True)
            m_new = mx if ci == 0 else jnp.maximum(m[mp], mx)
            pv = _dot(jnp.exp2(s - m_new).astype(BF16), v1)
            acc[mp] = pv if ci == 0 else jnp.exp2(m[mp] - m_new) * acc[mp] + pv
            m[mp] = m_new
    o = acc[0][:, :LANES] / acc[0][:, LANES:] - lam * (acc[1][:, :LANES] / acc[1][:, LANES:])
    ms = jnp.mean(o * o, axis=-1, keepdims=True)
    o_ref[0] = ((o * lax.rsqrt(ms + NORM_EPS) * g_ref[...]) * out_scale).astype(BF16)


def diff_attention(lam, q, k_ctx, v_ctx, k_lat, v_lat, subln_g, out_scale, tq):
    B, Sq, W = q.shape
    H = W // LANES
    has_lat = k_lat is not None
    Sc = k_ctx.shape[1]
    head = lambda S: pl.BlockSpec((1, S, LANES), lambda b, h, i: (b, 0, h))
    in_specs = [pl.BlockSpec(memory_space=pltpu.SMEM),
                pl.BlockSpec((1, tq, LANES), lambda b, h, i: (b, i, h)), head(Sc), head(Sc)]
    args = [lam.reshape(1), q, k_ctx, v_ctx]
    if has_lat:
        in_specs += [head(k_lat.shape[1]), head(k_lat.shape[1])]
        args += [k_lat, v_lat]
    in_specs.append(pl.BlockSpec((1, LANES), lambda b, h, i: (0, 0)))
    args.append(subln_g.reshape(1, LANES))
    return pl.pallas_call(
        functools.partial(_attn_kernel, has_lat=has_lat, out_scale=out_scale,
                          ck=min(ATTN_KEY_CHUNK, k_lat.shape[1]) if has_lat else 0),
        grid=(B, H, Sq // tq),
        in_specs=in_specs,
        out_specs=pl.BlockSpec((1, tq, LANES), lambda b, h, i: (b, i, h)),
        out_shape=jax.ShapeDtypeStruct((B, Sq, W), BF16),
        compiler_params=_params(("parallel", "parallel", "parallel"), 56),
        name="diff_attention",
    )(*args)


def _oproj_kernel(hy_ref, o_ref, w_ref, h_ref, gate_ref, out_ref, *, half):
    y = _dot(hy_ref[0].astype(BF16), w_ref[0:half, :]) + _dot(o_ref[0], w_ref[half:, :])
    out_ref[0] = h_ref[0] + gate_ref[0] * y


def out_proj(hy, o, w_out, h, gate, tm):
    B, S, D = h.shape
    half = hy.shape[2]
    return pl.pallas_call(
        functools.partial(_oproj_kernel, half=half),
        grid=(B, S // tm),
        in_specs=[pl.BlockSpec((1, tm, half), lambda b, i: (b, i, 0)),
                  pl.BlockSpec((1, tm, half), lambda b, i: (b, i, 0)),
                  pl.BlockSpec((2 * half, D), lambda b, i: (0, 0)),
                  pl.BlockSpec((1, tm, D), lambda b, i: (b, i, 0)),
                  pl.BlockSpec((1, 1, D), lambda b, i: (b, 0, 0))],
        out_specs=pl.BlockSpec((1, tm, D), lambda b, i: (b, i, 0)),
        out_shape=jax.ShapeDtypeStruct((B, S, D), F32),
        compiler_params=_params(("parallel", "parallel"), 40),
        name="out_proj",
    )(hy, o, w_out, h, gate)


def _shift_rows(x, d):
    return pltpu.roll(x, (-d) % x.shape[0], 0)


def _pool_kernel(hp_ref, hc_ref, hn_ref, g_ref, sc_ref, sh_ref, gate_ref, pw_ref, ps_ref, o_ref, *, T, L):
    i = pl.program_id(1)
    nt = pl.num_programs(1)
    g, sc, sh = g_ref[...], sc_ref[0], sh_ref[0]
    hc = hc_ref[0]
    a_c = _norm_mod(hc, g, sc, sh)
    a_p = jnp.where(i == 0, 0.0, _norm_mod(hp_ref[0], g, sc, sh))
    a_n = jnp.where(i == nt - 1, 0.0, _norm_mod(hn_ref[0], g, sc, sh))
    ext = jnp.concatenate([a_p, a_c, a_n], axis=0)
    tok = i * T + lax.broadcasted_iota(I32, (T, 1), 0)
    G = ext.shape[1] // len(POOL_WINDOWS)
    ys = []
    for gi, w in enumerate(POOL_WINDOWS):
        xg = ext[:, gi * G:(gi + 1) * G]
        s = _shift_rows(xg, -1) + xg
        step = 1
        while 2 * step < w:
            s = _shift_rows(s, -step) + _shift_rows(s, step)
            step *= 2
        cnt = (jnp.minimum(tok + w // 2, L) - jnp.maximum(tok - w // 2, 0)).astype(F32)
        p = s[POOL_HALO:POOL_HALO + T] / cnt - a_c[:, gi * G:(gi + 1) * G]
        ys.append(_dot(p.astype(BF16), pw_ref[gi]))
    y = jnp.concatenate(ys, axis=1) * ps_ref[...]
    o_ref[0] = hc + gate_ref[0] * y


def pool_mixer(h, g, scale, shift, gate, pool_w, pool_scale, T):
    B, S, D = h.shape
    nh = T // POOL_HALO
    last = S // POOL_HALO - 1
    mod = pl.BlockSpec((1, 1, D), lambda b, i: (b, 0, 0))
    return pl.pallas_call(
        functools.partial(_pool_kernel, T=T, L=S),
        grid=(B, S // T),
        in_specs=[pl.BlockSpec((1, POOL_HALO, D), lambda b, i: (b, jnp.maximum(i * nh - 1, 0), 0)),
                  pl.BlockSpec((1, T, D), lambda b, i: (b, i, 0)),
                  pl.BlockSpec((1, POOL_HALO, D), lambda b, i: (b, jnp.minimum((i + 1) * nh, last), 0)),
                  pl.BlockSpec((1, D), lambda b, i: (0, 0)), mod, mod, mod,
                  pl.BlockSpec(pool_w.shape, lambda b, i: (0, 0, 0)),
                  pl.BlockSpec((1, D), lambda b, i: (0, 0))],
        out_specs=pl.BlockSpec((1, T, D), lambda b, i: (b, i, 0)),
        out_shape=jax.ShapeDtypeStruct((B, S, D), F32),
        compiler_params=_params(("parallel", "parallel"), 48),
        name="pool_mixer",
    )(h, h, h, g.reshape(1, D), scale, shift, gate, pool_w, pool_scale.reshape(1, D))


def _ffn_prep_kernel(h_ref, g_ref, sc_ref, sh_ref, rw_ref, x_ref, aff_ref, afft_ref, *, n_exp):
    a = _norm_mod(h_ref[0], g_ref[...], sc_ref[0], sh_ref[0]).astype(BF16)
    x_ref[0] = a
    logits = _dot(a, rw_ref[...])
    lane = lax.broadcasted_iota(I32, logits.shape, 1)
    valid = lane < n_exp
    mx = jnp.max(jnp.where(valid, logits, -jnp.inf), axis=-1, keepdims=True)
    e = jnp.where(valid, jnp.exp(logits - mx), 0.0)
    aff = e / jnp.sum(e, axis=-1, keepdims=True)
    aff_ref[0] = aff
    afft_ref[0] = aff.T[0:n_exp]


def ffn_prep(h, g, scale, shift, router_w, tm):
    B, S, D = h.shape
    n_exp = router_w.shape[1]
    rw = jnp.pad(router_w, ((0, 0), (0, LANES - n_exp))).astype(BF16)
    mod = pl.BlockSpec((1, 1, D), lambda b, i: (b, 0, 0))
    return pl.pallas_call(
        functools.partial(_ffn_prep_kernel, n_exp=n_exp),
        grid=(B, S // tm),
        in_specs=[pl.BlockSpec((1, tm, D), lambda b, i: (b, i, 0)),
                  pl.BlockSpec((1, D), lambda b, i: (0, 0)), mod, mod,
                  pl.BlockSpec((D, LANES), lambda b, i: (0, 0))],
        out_specs=[pl.BlockSpec((1, tm, D), lambda b, i: (b, i, 0)),
                   pl.BlockSpec((1, tm, LANES), lambda b, i: (b, i, 0)),
                   pl.BlockSpec((1, n_exp, tm), lambda b, i: (b, 0, i))],
        out_shape=[jax.ShapeDtypeStruct((B, S, D), BF16), jax.ShapeDtypeStruct((B, S, LANES), F32),
                   jax.ShapeDtypeStruct((B, n_exp, S), F32)],
        compiler_params=_params(("parallel", "parallel"), 32),
        name="ffn_prep",
    )(h, g.reshape(1, D), scale, shift, rw)


def _prefix_excl(m, tri, tb):
    S = m.shape[0]
    carry = jnp.zeros((1, m.shape[1]), F32)
    outs, carries = [], []
    for blk in range(S // tb):
        mb = m[blk * tb:(blk + 1) * tb]
        outs.append(_dot(tri, mb.astype(BF16)) + carry)
        carries.append(carry)
        carry = carry + jnp.sum(mb, axis=0, keepdims=True)
    carries.append(carry)
    return (jnp.concatenate(outs, axis=0) if len(outs) > 1 else outs[0]), jnp.concatenate(carries, axis=0)


def _select_kernel(aff_ref, afft_ref, tri_ref, pos_ref, post_ref, offs_ref, *, cap, tb, n_exp):
    bits = pltpu.bitcast(afft_ref[0], I32)

    def body(i, cur):
        cand = cur | (jnp.int32(1) << (30 - i))
        cnt = jnp.sum((bits >= cand).astype(F32), axis=1, keepdims=True)
        return jnp.where(cnt >= cap, cand, cur)

    thr_bits = lax.fori_loop(0, 31, body, jnp.zeros((n_exp, 1), I32))
    thr_col = pltpu.bitcast(jnp.broadcast_to(thr_bits, (n_exp, LANES)), F32)
    sub = lax.broadcasted_iota(I32, (n_exp, LANES), 0)
    lane = lax.broadcasted_iota(I32, (n_exp, LANES), 1)
    thr = jnp.sum(jnp.where(sub == lane, thr_col, 0.0), axis=0, keepdims=True)
    aff = aff_ref[0]
    tri = tri_ref[...]
    gt = (aff > thr).astype(F32)
    eq = (aff == thr).astype(F32)
    need = cap - jnp.sum(gt, axis=0, keepdims=True)
    sel = gt + eq * (_prefix_excl(eq, tri, tb)[0] < need).astype(F32)
    slot, offs = _prefix_excl(sel, tri, tb)
    sp = jnp.where(sel > 0.0, slot, -1.0)
    pos_ref[0] = sp
    offs_ref[0] = offs
    for blk in range(sp.shape[0] // tb):
        post_ref[0, :, blk * tb:(blk + 1) * tb] = sp[blk * tb:(blk + 1) * tb].T[0:n_exp]


def select_tokens(aff, afft, cap):
    B, S, _ = aff.shape
    n_exp = afft.shape[1]
    tb = min(S, 256)
    r = lax.broadcasted_iota(I32, (tb, tb), 0)
    c = lax.broadcasted_iota(I32, (tb, tb), 1)
    tri = (c < r).astype(BF16)
    nblk = S // tb
    pos, post, offs = pl.pallas_call(
        functools.partial(_select_kernel, cap=cap, tb=tb, n_exp=n_exp),
        grid=(B,),
        in_specs=[pl.BlockSpec((1, S, LANES), lambda b: (b, 0, 0)),
                  pl.BlockSpec((1, n_exp, S), lambda b: (b, 0, 0)),
                  pl.BlockSpec((tb, tb), lambda b: (0, 0))],
        out_specs=[pl.BlockSpec((1, S, LANES), lambda b: (b, 0, 0)),
                   pl.BlockSpec((1, n_exp, S), lambda b: (b, 0, 0)),
                   pl.BlockSpec((1, nblk + 1, LANES), lambda b: (b, 0, 0))],
        out_shape=[jax.ShapeDtypeStruct((B, S, LANES), F32), jax.ShapeDtypeStruct((B, n_exp, S), F32),
                   jax.ShapeDtypeStruct((B, nblk + 1, LANES), F32)],
        compiler_params=_params(("parallel",), 48),
        name="select_tokens",
    )(aff, afft, tri)
    offs = jnp.swapaxes(offs[:, :, :n_exp], 1, 2).astype(I32).reshape(-1)
    return pos, post, offs, tb


def _gather_kernel(offs_ref, x_ref, post_ref, afft_ref, o_ref, g_ref, *, nblk, n_exp, win):
    b = pl.program_id(0)
    t = pl.program_id(1)
    slots = o_ref.shape[2]

    @pl.when(t == 0)
    def _():
        o_ref[...] = jnp.zeros_like(o_ref)
        g_ref[...] = jnp.zeros_like(g_ref)

    x = x_ref[0]
    T = x.shape[0]

    def full():
        r = lax.broadcasted_iota(I32, (slots, T), 0).astype(F32)
        for e in range(n_exp):
            hit = r == post_ref[0, e:e + 1, :]
            o_ref[e, 0] = (o_ref[e, 0].astype(F32) + _dot(hit.astype(BF16), x)).astype(BF16)
            g = jnp.sum(jnp.where(hit, afft_ref[0, e:e + 1, :], 0.0), axis=1, keepdims=True)
            g_ref[e, 0] = g_ref[e, 0] + jnp.broadcast_to(g, (slots, LANES))

    if win >= slots:
        full()
        return
    starts = []
    narrow = None
    for e in range(n_exp):
        base = (b * n_exp + e) * (nblk + 1) + t
        lo = offs_ref[base]
        hi = offs_ref[base + 1]
        a0 = pl.multiple_of(jnp.minimum((lo // BF16_ROWS) * BF16_ROWS, slots - win), BF16_ROWS)
        ok = hi - a0 <= win
        narrow = ok if narrow is None else jnp.logical_and(narrow, ok)
        starts.append(a0)

    @pl.when(narrow)
    def _():
        r = lax.broadcasted_iota(I32, (win, T), 0).astype(F32)
        hits = [r + starts[e].astype(F32) == post_ref[0, e:e + 1, :] for e in range(n_exp)]
        rows = _dot(jnp.concatenate([h.astype(BF16) for h in hits], axis=0), x)
        old = [o_ref[e, 0, pl.ds(starts[e], win), :] for e in range(n_exp)]
        old_g = [g_ref[e, 0, pl.ds(starts[e], win), :] for e in range(n_exp)]
        for e in range(n_exp):
            o_ref[e, 0, pl.ds(starts[e], win), :] = (old[e].astype(F32) + rows[e * win:(e + 1) * win]).astype(BF16)
            g = jnp.sum(jnp.where(hits[e], afft_ref[0, e:e + 1, :], 0.0), axis=1, keepdims=True)
            g_ref[e, 0, pl.ds(starts[e], win), :] = old_g[e] + jnp.broadcast_to(g, (win, LANES))

    pl.when(jnp.logical_not(narrow))(full)


def gather_tokens(x, post, afft, offs, slots, T):
    B, S, D = x.shape
    E = post.shape[1]
    row = pl.BlockSpec((1, E, T), lambda b, t, offs: (b, 0, t))
    return pl.pallas_call(
        functools.partial(_gather_kernel, nblk=S // T, n_exp=E, win=min(slots, GATHER_WINDOW)),
        grid_spec=pltpu.PrefetchScalarGridSpec(
            num_scalar_prefetch=1,
            grid=(B, S // T),
            in_specs=[pl.BlockSpec((1, T, D), lambda b, t, offs: (b, t, 0)), row, row],
            out_specs=[pl.BlockSpec((E, 1, slots, D), lambda b, t, offs: (0, b, 0, 0)),
                       pl.BlockSpec((E, 1, slots, LANES), lambda b, t, offs: (0, b, 0, 0))]),
        out_shape=[jax.ShapeDtypeStruct((E, B, slots, D), BF16), jax.ShapeDtypeStruct((E, B, slots, LANES), F32)],
        compiler_params=_params(("parallel", "arbitrary"), 56),
        name="gather_tokens",
    )(offs, x, post, afft)


def _ffn_kernel(xs_ref, g_ref, wg_ref, wu_ref, wd_ref, o_ref, acc_ref, *, nf):
    f = pl.program_id(2)
    x = xs_ref[0]
    a = _dot(x, wg_ref[...].astype(BF16))
    u = _dot(x, wu_ref[...].astype(BF16))
    hm = (a * jax.nn.sigmoid(a) * u).astype(BF16)
    part = _dot(hm, wd_ref[...].astype(BF16))

    @pl.when(f == 0)
    def _():
        acc_ref[...] = part

    @pl.when(f > 0)
    def _():
        acc_ref[...] += part

    @pl.when(f == nf - 1)
    def _():
        o_ref[0] = (acc_ref[...] * g_ref[0, :, 0:1]).astype(BF16)


def expert_ffn(xs, gs, w_gate, w_up, w_down, layer, tm, tf):
    E, M, D = xs.shape
    F = w_gate.shape[3]
    nf = F // tf
    return pl.pallas_call(
        functools.partial(_ffn_kernel, nf=nf),
        grid=(E, M // tm, nf),
        in_specs=[pl.BlockSpec((1, tm, D), lambda e, m, f: (e, m, 0)),
                  pl.BlockSpec((1, tm, LANES), lambda e, m, f: (e, m, 0)),
                  pl.BlockSpec((None, None, D, tf), lambda e, m, f: (layer, e, 0, f)),
                  pl.BlockSpec((None, None, D, tf), lambda e, m, f: (layer, e, 0, f)),
                  pl.BlockSpec((None, None, tf, D), lambda e, m, f: (layer, e, f, 0))],
        out_specs=pl.BlockSpec((1, tm, D), lambda e, m, f: (e, m, 0)),
        out_shape=jax.ShapeDtypeStruct((E, M, D), BF16),
        scratch_shapes=[pltpu.VMEM((tm, D), F32)],
        compiler_params=_params(("parallel", "parallel", "arbitrary"), 56),
        name="expert_ffn",
    )(xs, gs, w_gate, w_up, w_down)


def _combine_kernel(offs_ref, y_ref, pos_ref, h_ref, gate_ref, o_ref, ybuf_ref, *, nblk, n_exp, win):
    b = pl.program_id(0)
    t = pl.program_id(1)
    T = pos_ref.shape[1]
    slots = y_ref.shape[2]

    def full():
        if slots < LANES:
            K = n_exp * slots
            col = lax.broadcasted_iota(I32, (LANES, K), 1).astype(F32)
            owner = jnp.floor((col + 0.5) * (1.0 / slots))
            rep = owner == lax.broadcasted_iota(I32, (LANES, K), 0).astype(F32)
            slot_rep = _dot(pos_ref[0].astype(BF16), rep.astype(BF16))
            want = (col - owner * slots)[0:1]
            acc = _dot((slot_rep == want).astype(BF16), y_ref[:, 0].reshape(K, y_ref.shape[3]))
        else:
            r = lax.broadcasted_iota(I32, (T, slots), 1).astype(F32)
            acc = None
            for e in range(n_exp):
                part = _dot((pos_ref[0, :, e:e + 1] == r).astype(BF16), y_ref[e, 0])
                acc = part if acc is None else acc + part
        o_ref[0] = h_ref[0] + gate_ref[0] * acc

    if win >= slots:
        full()
        return
    starts = []
    narrow = None
    for e in range(n_exp):
        base = (b * n_exp + e) * (nblk + 1) + t
        lo = offs_ref[base]
        hi = offs_ref[base + 1]
        a0 = jnp.minimum((lo // BF16_ROWS) * BF16_ROWS, slots - win)
        ok = hi - a0 <= win
        narrow = ok if narrow is None else jnp.logical_and(narrow, ok)
        starts.append(a0)

    @pl.when(narrow)
    def _():
        r = lax.broadcasted_iota(I32, (T, win), 1).astype(F32)
        pieces = []
        for e in range(n_exp):
            a0 = starts[e]
            ybuf_ref[e * win:(e + 1) * win, :] = y_ref[e, 0, pl.ds(pl.multiple_of(a0, BF16_ROWS), win), :]
            pieces.append((pos_ref[0, :, e:e + 1] - a0.astype(F32) == r).astype(BF16))
        o_ref[0] = h_ref[0] + gate_ref[0] * _dot(jnp.concatenate(pieces, axis=1), ybuf_ref[...])

    pl.when(jnp.logical_not(narrow))(full)


def combine_tokens(y, pos, offs, h, gate, T):
    E, B, slots, D = y.shape
    S = h.shape[1]
    win = min(slots, COMBINE_WINDOW)
    return pl.pallas_call(
        functools.partial(_combine_kernel, nblk=S // T, n_exp=E, win=win),
        grid_spec=pltpu.PrefetchScalarGridSpec(
            num_scalar_prefetch=1,
            grid=(B, S // T),
            in_specs=[pl.BlockSpec((E, 1, slots, D), lambda b, t, offs: (0, b, 0, 0)),
                      pl.BlockSpec((1, T, LANES), lambda b, t, offs: (b, t, 0)),
                      pl.BlockSpec((1, T, D), lambda b, t, offs: (b, t, 0)),
                      pl.BlockSpec((1, 1, D), lambda b, t, offs: (b, 0, 0))],
            out_specs=pl.BlockSpec((1, T, D), lambda b, t, offs: (b, t, 0)),
            scratch_shapes=[pltpu.VMEM((E * win, D), BF16)]),
        out_shape=jax.ShapeDtypeStruct((B, S, D), F32),
        compiler_params=_params(("parallel", "parallel"), 56),
        name="combine_tokens",
    )(offs, y, pos, h, gate)


def moe_block(h, g, scale, shift, gate, router_w, w_gate, w_up, w_down, layer):
    B, S, D = h.shape
    E = router_w.shape[1]
    cap = EC_CAPACITY * S // E
    slots = cap
    x, aff, afft = ffn_prep(h, g, scale, shift, router_w, min(S, 512))
    pos, post, offs, tb = select_tokens(aff, afft, cap)
    xs, gs = gather_tokens(x, post, afft, offs, slots, tb)
    M = B * slots
    y = expert_ffn(xs.reshape(E, M, D), gs.reshape(E, M, LANES), w_gate, w_up, w_down, layer, min(M, 1024), 512)
    return combine_tokens(y.reshape(E, B, slots, D), pos, offs, h, gate, tb)


def kernel(x, c, ctx, c_ctx, ada_w, ada_b, norm_mix_g, norm_ffn_g, w_in, w_out, hy_conv_w, hy_conv_b, hy_f_w1, hy_f_b1, hy_f_w2, hy_f_b2, hy_f_w3, hy_f_b3, hy_f_freq, hy_f_wout, hy_bias, q_norm_g, k_norm_g, diff_lambda, subln_g, pool_w, pool_scale, router_w, exp_w_gate, exp_w_up, exp_w_down):
    B, S, D = x.shape
    Lc = ctx.shape[1]
    depth = ada_w.shape[0]
    hy_width = hy_bias.shape[2]
    hy_proj = (HYENA_ORDER + 1) * hy_width
    qk_width = DIFF_HEADS * 2 * DIFF_HEAD_DIM
    last_attn = ((depth - 1) // 2) * 2

    s_all = jnp.concatenate([jax.nn.silu(c), jax.nn.silu(c_ctx)[None, :]], axis=0)
    rows = -(-s_all.shape[0] // SUBLANES) * SUBLANES
    mods = ada_modulation(jnp.pad(s_all, ((0, rows - s_all.shape[0]), (0, 0))), ada_w, ada_b)

    nblk_lat = max(1, S // HYENA_BLOCK)
    tables_lat = dft_tables(S // nblk_lat)
    tables_ctx = dft_tables(Lc)
    rope_lat = rope_tables(S, True)
    rope_ctx = rope_tables(Lc, False)

    h, hc = x, ctx
    for l in range(depth):
        m = [mods[l, :B, i * D:(i + 1) * D].reshape(B, 1, D) for i in range(6)]
        mc = [jnp.broadcast_to(mods[l, B, i * D:(i + 1) * D].reshape(1, 1, D), (B, 1, D)) for i in range(6)]
        ctx_full = l < last_attn
        if l % 2 == 0:
            e = l // 2
            lam_init = 0.8 - 0.6 * math.exp(-0.3 * l)
            lv = diff_lambda[e]
            lam = jnp.exp(jnp.sum(lv[0] * lv[1])) - jnp.exp(jnp.sum(lv[2] * lv[3])) + lam_init
            filt = (hy_f_w1[e], hy_f_b1[e], hy_f_w2[e], hy_f_b2[e], hy_f_w3[e], hy_f_b3[e], hy_f_freq[e], hy_f_wout[e])
            w_in_b = w_in[e].astype(BF16)
            w_out_b = w_out[e].astype(BF16)

            p = norm_mod_matmul(h, norm_mix_g[l], m[1], m[0], w_in_b, 512)
            gts = short_conv(p, hy_conv_w[e], hy_conv_b[e], 0, 2 * hy_width, 256, F32)
            hv = short_conv(p, hy_conv_w[e], hy_conv_b[e], 2 * hy_width, hy_width, 256, BF16)
            hy = hyena_operator(gts, hv, tables_lat, *hyena_filter_taps(S, *filt, hy_width), hy_bias[e], nblk_lat, 256)
            q, k, v = qkv_prep(p, 1, True, *rope_lat, q_norm_g[e], k_norm_g[e], 512)

            if l <= last_attn:
                if ctx_full:
                    pc = norm_mod_matmul(hc, norm_mix_g[l], mc[1], mc[0], w_in_b, Lc)
                    qc, kc, vc = qkv_prep(pc, 1, True, *rope_ctx, q_norm_g[e], k_norm_g[e], Lc)
                else:
                    pc = norm_mod_matmul(hc, norm_mix_g[l], mc[1], mc[0], w_in_b[:, hy_proj + qk_width:], Lc)
                    kc, vc = qkv_prep(pc, 0, False, *rope_ctx, q_norm_g[e], k_norm_g[e], Lc)
            o = diff_attention(lam, q, kc, vc, k, v, subln_g[e], 1.0 - lam_init, min(S, 512))
            h = out_proj(hy, o, w_out_b, h, m[2], 512)
            if ctx_full:
                gtc = short_conv(pc, hy_conv_w[e], hy_conv_b[e], 0, 2 * hy_width, 256, F32)
                hvc = short_conv(pc, hy_conv_w[e], hy_conv_b[e], 2 * hy_width, hy_width, 256, BF16)
                hyc = hyena_operator(gtc, hvc, tables_ctx, *hyena_filter_taps(Lc, *filt, hy_width), hy_bias[e], 1, Lc)
                oc = diff_attention(lam, qc, kc, vc, None, None, subln_g[e], 1.0 - lam_init, Lc)
                hc = out_proj(hyc, oc, w_out_b, hc, mc[2], Lc)
        else:
            o_idx = l // 2
            pw = pool_w[o_idx].astype(BF16)
            h = pool_mixer(h, norm_mix_g[l], m[1], m[0], m[2], pw, pool_scale[o_idx], 512)
            if ctx_full:
                hc = pool_mixer(hc, norm_mix_g[l], mc[1], mc[0], mc[2], pw, pool_scale[o_idx], Lc)
        h = moe_block(h, norm_ffn_g[l], m[4], m[3], m[5], router_w[l], exp_w_gate, exp_w_up, exp_w_down, l)
        if ctx_full:
            hc = moe_block(hc, norm_ffn_g[l], mc[4], mc[3], mc[5], router_w[l], exp_w_gate, exp_w_up, exp_w_down, l)
    return h
```

```python
import functools
import math

import jax
import jax.numpy as jnp
from jax import lax
from jax.experimental import pallas as pl
from jax.experimental.pallas import tpu as pltpu

F32 = jnp.float32
BF16 = jnp.bfloat16
I32 = jnp.int32

NORM_EPS = 1e-6
GRID_W = 64
HYENA_ORDER = 2
FILTER_EMB = 33
FILTER_FAST_DECAY = 0.3
FILTER_SLOW_DECAY = 1.5
FILTER_TARGET = 1e-2
DIFF_HEADS = 4
DIFF_HEAD_DIM = 64
ROPE_BASE = 10000.0
POOL_WINDOWS = (2, 4, 8, 16)
EC_CAPACITY = 2
LANES = 128
SUBLANES = 8
POOL_HALO = 8
LOG2E = 1.4426950408889634
ATTN_KEY_CHUNK = 2048
GATHER_WINDOW = 80
COMBINE_WINDOW = 80
BF16_ROWS = 16
HYENA_BLOCK = 1024


def _params(sem, vmem_mb):
    return pltpu.CompilerParams(dimension_semantics=sem, vmem_limit_bytes=vmem_mb * 1024 * 1024)


def _dot(a, b):
    return jnp.dot(a, b, preferred_element_type=F32)


def _norm_mod(x, g, scale, shift):
    ms = jnp.mean(x * x, axis=-1, keepdims=True)
    return (x * lax.rsqrt(ms + NORM_EPS) * g) * (1.0 + scale) + shift


def _mm_kernel(a_ref, b_ref, o_ref, acc_ref, *, nk):
    k = pl.program_id(2)
    part = _dot(a_ref[...].astype(BF16), b_ref[...].astype(BF16))

    @pl.when(k == 0)
    def _():
        acc_ref[...] = part

    @pl.when(k > 0)
    def _():
        acc_ref[...] += part

    @pl.when(k == nk - 1)
    def _():
        o_ref[...] = acc_ref[...].astype(o_ref.dtype)


def matmul(a, b, tm, tn, tk):
    M, K = a.shape
    N = b.shape[1]
    nk = K // tk
    return pl.pallas_call(
        functools.partial(_mm_kernel, nk=nk),
        grid=(M // tm, N // tn, nk),
        in_specs=[pl.BlockSpec((tm, tk), lambda i, j, k: (i, k)),
                  pl.BlockSpec((tk, tn), lambda i, j, k: (k, j))],
        out_specs=pl.BlockSpec((tm, tn), lambda i, j, k: (i, j)),
        out_shape=jax.ShapeDtypeStruct((M, N), F32),
        scratch_shapes=[pltpu.VMEM((tm, tn), F32)],
        compiler_params=_params(("parallel", "parallel", "arbitrary"), 40),
        name="matmul",
    )(a, b)


def _ada_kernel(s_ref, w_ref, b_ref, o_ref):
    o_ref[...] = _dot(s_ref[...].astype(BF16), w_ref[...].astype(BF16)) + b_ref[...]


def ada_modulation(s, ada_w, ada_b):
    depth, D, N = ada_w.shape
    R = s.shape[0]
    tn = 1024
    return pl.pallas_call(
        _ada_kernel,
        grid=(depth, N // tn),
        in_specs=[pl.BlockSpec((R, D), lambda l, j: (0, 0)),
                  pl.BlockSpec((None, D, tn), lambda l, j: (l, 0, j)),
                  pl.BlockSpec((None, 1, tn), lambda l, j: (l, 0, j))],
        out_specs=pl.BlockSpec((None, R, tn), lambda l, j: (l, 0, j)),
        out_shape=jax.ShapeDtypeStruct((depth, R, N), F32),
        compiler_params=_params(("parallel", "parallel"), 32),
        name="ada_modulation",
    )(s, ada_w, ada_b.reshape(depth, 1, N))


def _in_proj_kernel(h_ref, g_ref, sc_ref, sh_ref, w_ref, cos_ref, sin_ref, qg_ref, kg_ref, *o_refs, hy_cols, has_q, width):
    a = _norm_mod(h_ref[0], g_ref[...], sc_ref[0], sh_ref[0])
    p = _dot(a.astype(BF16), w_ref[...])
    cos = cos_ref[...]
    sin = sin_ref[...]
    outs = list(o_refs)
    col = 0
    if hy_cols:
        outs.pop(0)[0] = p[:, :hy_cols]
        col = hy_cols
    if has_q:
        q_ref = outs.pop(0)
        for hd in range(width // LANES):
            sl = slice(col + hd * LANES, col + (hd + 1) * LANES)
            q_ref[0, :, hd * LANES:(hd + 1) * LANES] = _head_norm_rope(
                p[:, sl], qg_ref[...], cos, sin, LOG2E * DIFF_HEAD_DIM ** -0.5).astype(BF16)
        col += width
    k_ref, v_ref = outs
    for hd in range(width // LANES):
        sl = slice(col + hd * LANES, col + (hd + 1) * LANES)
        k_ref[0, :, hd * LANES:(hd + 1) * LANES] = _head_norm_rope(p[:, sl], kg_ref[...], cos, sin, 1.0).astype(BF16)
    col += width
    ones = jnp.ones((p.shape[0], LANES), BF16)
    for hd in range(width // LANES):
        v_ref[0, :, 2 * hd * LANES:(2 * hd + 1) * LANES] = p[:, col + hd * LANES:col + (hd + 1) * LANES].astype(BF16)
        v_ref[0, :, (2 * hd + 1) * LANES:(2 * hd + 2) * LANES] = ones


def in_proj(h, g, scale, shift, w, hy_cols, has_q, cos, sin_signed, q_g, k_g, tm):
    B, S, D = h.shape
    N = w.shape[1]
    width = DIFF_HEADS * 2 * DIFF_HEAD_DIM
    g2 = lambda gg: jnp.concatenate([gg, gg]).reshape(1, LANES)
    mod = pl.BlockSpec((1, 1, D), lambda b, i: (b, 0, 0))
    tab = pl.BlockSpec((tm, LANES), lambda b, i: (i, 0))
    vec = pl.BlockSpec((1, LANES), lambda b, i: (0, 0))
    widths = ([hy_cols] if hy_cols else []) + ([width] if has_q else []) + [width, 2 * width]
    dtypes = ([F32] if hy_cols else []) + ([BF16] if has_q else []) + [BF16, BF16]
    return pl.pallas_call(
        functools.partial(_in_proj_kernel, hy_cols=hy_cols, has_q=has_q, width=width),
        grid=(B, S // tm),
        in_specs=[pl.BlockSpec((1, tm, D), lambda b, i: (b, i, 0)),
                  pl.BlockSpec((1, D), lambda b, i: (0, 0)), mod, mod,
                  pl.BlockSpec((D, N), lambda b, i: (0, 0)), tab, tab, vec, vec],
        out_specs=[pl.BlockSpec((1, tm, wd), lambda b, i: (b, i, 0)) for wd in widths],
        out_shape=[jax.ShapeDtypeStruct((B, S, wd), dt) for wd, dt in zip(widths, dtypes)],
        compiler_params=_params(("parallel", "parallel"), 56),
        name="in_proj",
    )(h, g.reshape(1, D), scale, shift, w, cos, sin_signed, g2(q_g), g2(k_g))


def _sconv_kernel(p_ref, w_ref, b_ref, o_ref):
    x = p_ref[0]
    S = x.shape[0]
    row = lax.broadcasted_iota(I32, x.shape, 0)
    xm = jnp.where(row == 0, 0.0, pltpu.roll(x, 1, 0))
    xp = jnp.where(row == S - 1, 0.0, pltpu.roll(x, S - 1, 0))
    w = w_ref[...]
    o_ref[0] = (xm * w[0:1] + x * w[1:2] + xp * w[2:3] + b_ref[...]).astype(o_ref.dtype)


def short_conv(p, conv_w, conv_b, col0, width, tc, out_dtype):
    B, S, _ = p.shape
    c0 = col0 // tc
    return pl.pallas_call(
        _sconv_kernel,
        grid=(B, width // tc),
        in_specs=[pl.BlockSpec((1, S, tc), lambda b, c: (b, 0, c + c0)),
                  pl.BlockSpec((3, tc), lambda b, c: (0, c + c0)),
                  pl.BlockSpec((1, tc), lambda b, c: (0, c + c0))],
        out_specs=pl.BlockSpec((1, S, tc), lambda b, c: (b, 0, c)),
        out_shape=jax.ShapeDtypeStruct((B, S, width), out_dtype),
        compiler_params=_params(("parallel", "parallel"), 48),
        name="short_conv",
    )(p, conv_w, conv_b.reshape(1, -1))


def _dft_fwd_kernel(c_ref, s_ref, u_ref, kre_ref, kim_ref, y_ref, *, nblk):
    c = c_ref[...]
    s = s_ref[...]
    C = u_ref.shape[3]
    ure, uim = [], []
    for j in range(nblk):
        u = u_ref[0, j]
        ure.append(_dot(c, u))
        uim.append(-_dot(s, u))
    for i in range(nblk):
        yre = yim = None
        for j in range(nblk):
            d = i - j + nblk - 1
            kre = kre_ref[:, d * C:(d + 1) * C]
            kim = kim_ref[:, d * C:(d + 1) * C]
            tre = kre * ure[j] - kim * uim[j]
            tim = kre * uim[j] + kim * ure[j]
            yre = tre if yre is None else yre + tre
            yim = tim if yim is None else yim + tim
        y_ref[0, i, 0] = yre.astype(BF16)
        y_ref[0, i, 1] = yim.astype(BF16)


def dft_forward(cf, sf, u, kre, kim, order, tf):
    B, nblk, Lb, C = u.shape
    nd = 2 * nblk - 1
    return pl.pallas_call(
        functools.partial(_dft_fwd_kernel, nblk=nblk),
        grid=(Lb // tf, B),
        in_specs=[pl.BlockSpec((tf, Lb), lambda f, b: (f, 0)),
                  pl.BlockSpec((tf, Lb), lambda f, b: (f, 0)),
                  pl.BlockSpec((1, nblk, Lb, C), lambda f, b: (b, 0, 0, 0)),
                  pl.BlockSpec((tf, nd * C), lambda f, b: (f, order)),
                  pl.BlockSpec((tf, nd * C), lambda f, b: (f, order))],
        out_specs=pl.BlockSpec((1, nblk, 2, tf, C), lambda f, b: (b, 0, 0, f, 0)),
        out_shape=jax.ShapeDtypeStruct((B, nblk, 2, Lb, C), BF16),
        compiler_params=_params(("parallel", "parallel"), 56),
        name="dft_forward",
    )(cf, sf, u, kre, kim)


def _dft_inv_kernel(ct_ref, st_ref, y_ref, gate_ref, o_ref, *, scale):
    acc = _dot(ct_ref[...], y_ref[0, 0, 0]) - _dot(st_ref[...], y_ref[0, 0, 1])
    o_ref[0, 0] = (gate_ref[0, 0] * (acc * scale)).astype(BF16)


def dft_inverse(ct, st, y, gates, gate_col):
    B, nblk, _, Lb, C = y.shape
    return pl.pallas_call(
        functools.partial(_dft_inv_kernel, scale=1.0 / Lb),
        grid=(B, nblk),
        in_specs=[pl.BlockSpec((Lb, Lb), lambda b, i: (0, 0)),
                  pl.BlockSpec((Lb, Lb), lambda b, i: (0, 0)),
                  pl.BlockSpec((1, 1, 2, Lb, C), lambda b, i: (b, i, 0, 0, 0)),
                  pl.BlockSpec((1, 1, Lb, C), lambda b, i: (b, i, 0, gate_col))],
        out_specs=pl.BlockSpec((1, 1, Lb, C), lambda b, i: (b, i, 0, 0)),
        out_shape=jax.ShapeDtypeStruct((B, nblk, Lb, C), BF16),
        compiler_params=_params(("parallel", "parallel"), 48),
        name="dft_inverse",
    )(ct, st, y, gates)


def dft_tables(L):
    n = 2 * L
    f = lax.broadcasted_iota(I32, (L, L), 0)
    t = lax.broadcasted_iota(I32, (L, L), 1)
    m = ((2 * f + 1) * t) % (2 * n)
    ang = m.astype(F32) * (math.pi / n)
    cf = jnp.cos(ang)
    sf = jnp.sin(ang)
    return cf.astype(BF16), sf.astype(BF16), cf.T.astype(BF16), sf.T.astype(BF16)


def hyena_filter_taps(L, w1, b1, w2, b2, w3, b3, freq, wout, width):
    hp = lax.Precision.HIGHEST
    t = jnp.linspace(0.0, 1.0, L, dtype=F32)[:, None]
    bands = (FILTER_EMB - 1) // 2
    w = 2.0 * math.pi * jnp.arange(L, dtype=F32)[:, None] / L
    f = jnp.linspace(1e-4, bands - 1, bands, dtype=F32)[None, :]
    z = jnp.concatenate([t, jnp.cos(f * w), -jnp.sin(f * w)], axis=-1)
    h = jnp.sin(freq * (jnp.dot(z, w1, precision=hp) + b1))
    h = jnp.sin(freq * (jnp.dot(h, w2, precision=hp) + b2))
    h = jnp.sin(freq * (jnp.dot(h, w3, precision=hp) + b3))
    max_decay = math.log(FILTER_TARGET) / FILTER_FAST_DECAY
    min_decay = math.log(FILTER_TARGET) / FILTER_SLOW_DECAY
    deltas = jnp.abs(jnp.linspace(min_decay, max_decay, width, dtype=F32))

    def taps(hh, tt):
        return jnp.dot(hh, wout, precision=hp).reshape(L, HYENA_ORDER, 2, width) * jnp.exp(-tt * deltas[None, :])[:, None, None, :]

    return taps(h, t), taps(h[::-1], t[::-1])


def filter_spectra(cf, sf, taps, taps_rev, bias, nblk):
    L, C = taps.shape[0], taps.shape[3]
    Lb = L // nblk
    zero = jnp.zeros((1, C), F32)
    sums, diffs = [], []
    for o in range(HYENA_ORDER):
        h_fwd, h_bwd = taps[:, o, 0], taps[:, o, 1]
        h_fwd_rev, h_bwd_rev = taps_rev[:, o, 0], taps_rev[:, o, 1]
        k0 = (h_fwd[0] + h_bwd[0] + bias[o])[None, :]
        kfull = jnp.concatenate([zero, h_bwd_rev[:L - 1], k0, h_fwd[1:]], axis=0)
        krev = jnp.concatenate([h_fwd_rev[:L - 1], k0, h_bwd[1:], zero], axis=0)
        for dlt in range(-(nblk - 1), nblk):
            base = L + dlt * Lb
            kp = kfull[base:base + Lb]
            km = jnp.concatenate([zero, krev[2 * L - base:2 * L - base + Lb - 1]], axis=0)
            sums.append(kp + km)
            diffs.append(km - kp)
    tmm = min(Lb, 512)
    kre = matmul(cf, jnp.concatenate(sums, axis=1), tmm, tmm, tmm)
    kim = matmul(sf, jnp.concatenate(diffs, axis=1), tmm, tmm, tmm)
    return kre, kim


def hyena_operator(gates, v, tables, taps, taps_rev, bias, nblk, tf):
    cf, sf, ct, st = tables
    B, L, C = v.shape
    Lb = L // nblk
    kre, kim = filter_spectra(cf, sf, taps, taps_rev, bias, nblk)
    gates = gates.reshape(B, nblk, Lb, 2 * C)
    z = v.reshape(B, nblk, Lb, C)
    for o in range(HYENA_ORDER):
        y = dft_forward(cf, sf, z, kre, kim, o, tf)
        z = dft_inverse(ct, st, y, gates, o)
    return z.reshape(B, L, C)


def _head_norm_rope(x, g, cos, sin_signed, scale):
    lane = lax.broadcasted_iota(I32, x.shape, 1)
    lo = lane < DIFF_HEAD_DIM
    x2 = x * x
    s_lo = jnp.sum(jnp.where(lo, x2, 0.0), axis=-1, keepdims=True)
    s_hi = jnp.sum(jnp.where(lo, 0.0, x2), axis=-1, keepdims=True)
    ms = jnp.where(lo, s_lo, s_hi) * (1.0 / DIFF_HEAD_DIM)
    xn = x * lax.rsqrt(ms + NORM_EPS) * g
    first = (lane & 16) == 0
    partner = jnp.where(first, pltpu.roll(xn, LANES - 16, 1), pltpu.roll(xn, 16, 1))
    return (xn * cos + partner * sin_signed) * scale


def rope_tables(S, use_rope):
    if not use_rope:
        return jnp.ones((S, LANES), F32), jnp.zeros((S, LANES), F32)
    t = jnp.arange(S, dtype=I32)
    row = (t // GRID_W).astype(F32)[:, None]
    colp = (t % GRID_W).astype(F32)[:, None]
    nf = DIFF_HEAD_DIM // 4
    inv = ROPE_BASE ** (-jnp.arange(nf, dtype=F32) / nf)
    lane = jnp.arange(LANES)
    grp = (lane % DIFF_HEAD_DIM) // nf
    j = lane % nf
    pos = jnp.where((grp < 2)[None, :], row, colp)
    ang = pos * inv[j][None, :]
    sign = jnp.where((grp % 2 == 0)[None, :], -1.0, 1.0)
    return jnp.cos(ang), jnp.sin(ang) * sign


def _attn_kernel(lam_ref, q_ref, kc_ref, vc_ref, *rest, has_lat, out_scale, ck):
    if has_lat:
        kl_ref, vl_ref, g_ref, o_ref = rest
    else:
        g_ref, o_ref = rest
    lam = lam_ref[0]
    q = q_ref[0]
    lane = lax.broadcasted_iota(I32, q.shape, 1)
    nt = (((1,), (1,)), ((), ()))
    zero = jnp.zeros_like(q)
    qm = [jnp.where(lane < DIFF_HEAD_DIM, q, zero), jnp.where(lane >= DIFF_HEAD_DIM, q, zero)]
    chunks = [(kc_ref, vc_ref, 0, kc_ref.shape[1])]
    if has_lat:
        chunks += [(kl_ref, vl_ref, c * ck, ck) for c in range(kl_ref.shape[1] // ck)]

    def scores(ch):
        k = ch[0][0, ch[2]:ch[2] + ch[3], :]
        return [lax.dot_general(qm[mp], k, nt, preferred_element_type=F32) for mp in range(2)]

    m, acc = [None, None], [None, None]
    s_next = scores(chunks[0])
    for ci, ch in enumerate(chunks):
        s_cur = s_next
        if ci + 1 < len(chunks):
            s_next = scores(chunks[ci + 1])
        v1 = ch[1][0, ch[2]:ch[2] + ch[3], :]
        for mp in range(2):
            s = s_cur[mp]
            mx = jnp.max(s, axis=-1, keepdims=True)
            m_new = mx if ci == 0 else jnp.maximum(m[mp], mx)
            pv = _dot(jnp.exp2(s - m_new).astype(BF16), v1)
            acc[mp] = pv if ci == 0 else jnp.exp2(m[mp] - m_new) * acc[mp] + pv
            m[mp] = m_new
    o = acc[0][:, :LANES] / acc[0][:, LANES:] - lam * (acc[1][:, :LANES] / acc[1][:, LANES:])
    ms = jnp.mean(o * o, axis=-1, keepdims=True)
    o_ref[0] = ((o * lax.rsqrt(ms + NORM_EPS) * g_ref[...]) * out_scale).astype(BF16)


def diff_attention(lam, q, k_ctx, v_ctx, k_lat, v_lat, subln_g, out_scale, tq):
    B, Sq, W = q.shape
    H = W // LANES
    has_lat = k_lat is not None
    Sc = k_ctx.shape[1]
    head = lambda S, w: pl.BlockSpec((1, S, w), lambda b, h, i: (b, 0, h))
    in_specs = [pl.BlockSpec(memory_space=pltpu.SMEM),
                pl.BlockSpec((1, tq, LANES), lambda b, h, i: (b, i, h)), head(Sc, LANES), head(Sc, 2 * LANES)]
    args = [lam.reshape(1), q, k_ctx, v_ctx]
    if has_lat:
        in_specs += [head(k_lat.shape[1], LANES), head(k_lat.shape[1], 2 * LANES)]
        args += [k_lat, v_lat]
    in_specs.append(pl.BlockSpec((1, LANES), lambda b, h, i: (0, 0)))
    args.append(subln_g.reshape(1, LANES))
    return pl.pallas_call(
        functools.partial(_attn_kernel, has_lat=has_lat, out_scale=out_scale,
                          ck=min(ATTN_KEY_CHUNK, k_lat.shape[1]) if has_lat else 0),
        grid=(B, H, Sq // tq),
        in_specs=in_specs,
        out_specs=pl.BlockSpec((1, tq, LANES), lambda b, h, i: (b, i, h)),
        out_shape=jax.ShapeDtypeStruct((B, Sq, W), BF16),
        compiler_params=_params(("parallel", "parallel", "parallel"), 56),
        name="diff_attention",
    )(*args)


def _oproj_kernel(hy_ref, o_ref, w_ref, h_ref, gate_ref, out_ref, *, half):
    y = _dot(hy_ref[0].astype(BF16), w_ref[0:half, :]) + _dot(o_ref[0], w_ref[half:, :])
    out_ref[0] = h_ref[0] + gate_ref[0] * y


def out_proj(hy, o, w_out, h, gate, tm):
    B, S, D = h.shape
    half = hy.shape[2]
    return pl.pallas_call(
        functools.partial(_oproj_kernel, half=half),
        grid=(B, S // tm),
        in_specs=[pl.BlockSpec((1, tm, half), lambda b, i: (b, i, 0)),
                  pl.BlockSpec((1, tm, half), lambda b, i: (b, i, 0)),
                  pl.BlockSpec((2 * half, D), lambda b, i: (0, 0)),
                  pl.BlockSpec((1, tm, D), lambda b, i: (b, i, 0)),
                  pl.BlockSpec((1, 1, D), lambda b, i: (b, 0, 0))],
        out_specs=pl.BlockSpec((1, tm, D), lambda b, i: (b, i, 0)),
        out_shape=jax.ShapeDtypeStruct((B, S, D), F32),
        compiler_params=_params(("parallel", "parallel"), 40),
        name="out_proj",
    )(hy, o, w_out, h, gate)


def _shift_rows(x, d):
    return pltpu.roll(x, (-d) % x.shape[0], 0)


def _pool_kernel(hp_ref, hc_ref, hn_ref, g_ref, sc_ref, sh_ref, gate_ref, pw_ref, ps_ref, o_ref, *, T, L):
    i = pl.program_id(1)
    nt = pl.num_programs(1)
    g, sc, sh = g_ref[...], sc_ref[0], sh_ref[0]
    hc = hc_ref[0]
    a_c = _norm_mod(hc, g, sc, sh)
    a_p = jnp.where(i == 0, 0.0, _norm_mod(hp_ref[0], g, sc, sh))
    a_n = jnp.where(i == nt - 1, 0.0, _norm_mod(hn_ref[0], g, sc, sh))
    ext = jnp.concatenate([a_p, a_c, a_n], axis=0)
    tok = i * T + lax.broadcasted_iota(I32, (T, 1), 0)
    G = ext.shape[1] // len(POOL_WINDOWS)
    ys = []
    for gi, w in enumerate(POOL_WINDOWS):
        xg = ext[:, gi * G:(gi + 1) * G]
        s = _shift_rows(xg, -1) + xg
        step = 1
        while 2 * step < w:
            s = _shift_rows(s, -step) + _shift_rows(s, step)
            step *= 2
        cnt = (jnp.minimum(tok + w // 2, L) - jnp.maximum(tok - w // 2, 0)).astype(F32)
        p = s[POOL_HALO:POOL_HALO + T] / cnt - a_c[:, gi * G:(gi + 1) * G]
        ys.append(_dot(p.astype(BF16), pw_ref[gi]))
    y = jnp.concatenate(ys, axis=1) * ps_ref[...]
    o_ref[0] = hc + gate_ref[0] * y


def pool_mixer(h, g, scale, shift, gate, pool_w, pool_scale, T):
    B, S, D = h.shape
    nh = T // POOL_HALO
    last = S // POOL_HALO - 1
    mod = pl.BlockSpec((1, 1, D), lambda b, i: (b, 0, 0))
    return pl.pallas_call(
        functools.partial(_pool_kernel, T=T, L=S),
        grid=(B, S // T),
        in_specs=[pl.BlockSpec((1, POOL_HALO, D), lambda b, i: (b, jnp.maximum(i * nh - 1, 0), 0)),
                  pl.BlockSpec((1, T, D), lambda b, i: (b, i, 0)),
                  pl.BlockSpec((1, POOL_HALO, D), lambda b, i: (b, jnp.minimum((i + 1) * nh, last), 0)),
                  pl.BlockSpec((1, D), lambda b, i: (0, 0)), mod, mod, mod,
                  pl.BlockSpec(pool_w.shape, lambda b, i: (0, 0, 0)),
                  pl.BlockSpec((1, D), lambda b, i: (0, 0))],
        out_specs=pl.BlockSpec((1, T, D), lambda b, i: (b, i, 0)),
        out_shape=jax.ShapeDtypeStruct((B, S, D), F32),
        compiler_params=_params(("parallel", "parallel"), 48),
        name="pool_mixer",
    )(h, h, h, g.reshape(1, D), scale, shift, gate, pool_w, pool_scale.reshape(1, D))


def _ffn_prep_kernel(h_ref, g_ref, sc_ref, sh_ref, rw_ref, x_ref, aff_ref, afft_ref, *, n_exp):
    a = _norm_mod(h_ref[0], g_ref[...], sc_ref[0], sh_ref[0]).astype(BF16)
    x_ref[0] = a
    logits = _dot(a, rw_ref[...])
    lane = lax.broadcasted_iota(I32, logits.shape, 1)
    valid = lane < n_exp
    mx = jnp.max(jnp.where(valid, logits, -jnp.inf), axis=-1, keepdims=True)
    e = jnp.where(valid, jnp.exp(logits - mx), 0.0)
    aff = e / jnp.sum(e, axis=-1, keepdims=True)
    aff_ref[0] = aff
    afft_ref[0] = aff.T[0:n_exp]


def ffn_prep(h, g, scale, shift, router_w, tm):
    B, S, D = h.shape
    n_exp = router_w.shape[1]
    rw = jnp.pad(router_w, ((0, 0), (0, LANES - n_exp))).astype(BF16)
    mod = pl.BlockSpec((1, 1, D), lambda b, i: (b, 0, 0))
    return pl.pallas_call(
        functools.partial(_ffn_prep_kernel, n_exp=n_exp),
        grid=(B, S // tm),
        in_specs=[pl.BlockSpec((1, tm, D), lambda b, i: (b, i, 0)),
                  pl.BlockSpec((1, D), lambda b, i: (0, 0)), mod, mod,
                  pl.BlockSpec((D, LANES), lambda b, i: (0, 0))],
        out_specs=[pl.BlockSpec((1, tm, D), lambda b, i: (b, i, 0)),
                   pl.BlockSpec((1, tm, LANES), lambda b, i: (b, i, 0)),
                   pl.BlockSpec((1, n_exp, tm), lambda b, i: (b, 0, i))],
        out_shape=[jax.ShapeDtypeStruct((B, S, D), BF16), jax.ShapeDtypeStruct((B, S, LANES), F32),
                   jax.ShapeDtypeStruct((B, n_exp, S), F32)],
        compiler_params=_params(("parallel", "parallel"), 32),
        name="ffn_prep",
    )(h, g.reshape(1, D), scale, shift, rw)


def _prefix_excl(m, tri, tb):
    S = m.shape[0]
    carry = jnp.zeros((1, m.shape[1]), F32)
    outs, carries = [], []
    for blk in range(S // tb):
        mb = m[blk * tb:(blk + 1) * tb]
        outs.append(_dot(tri, mb.astype(BF16)) + carry)
        carries.append(carry)
        carry = carry + jnp.sum(mb, axis=0, keepdims=True)
    carries.append(carry)
    return (jnp.concatenate(outs, axis=0) if len(outs) > 1 else outs[0]), jnp.concatenate(carries, axis=0)


def _select_kernel(aff_ref, afft_ref, tri_ref, pos_ref, post_ref, offs_ref, *, cap, tb, n_exp):
    bits = pltpu.bitcast(afft_ref[0], I32)

    def body(i, cur):
        cand = cur | (jnp.int32(1) << (30 - i))
        cnt = jnp.sum((bits >= cand).astype(F32), axis=1, keepdims=True)
        return jnp.where(cnt >= cap, cand, cur)

    thr_bits = lax.fori_loop(0, 31, body, jnp.zeros((n_exp, 1), I32))
    thr_col = pltpu.bitcast(jnp.broadcast_to(thr_bits, (n_exp, LANES)), F32)
    sub = lax.broadcasted_iota(I32, (n_exp, LANES), 0)
    lane = lax.broadcasted_iota(I32, (n_exp, LANES), 1)
    thr = jnp.sum(jnp.where(sub == lane, thr_col, 0.0), axis=0, keepdims=True)
    aff = aff_ref[0]
    tri = tri_ref[...]
    gt = (aff > thr).astype(F32)
    eq = (aff == thr).astype(F32)
    need = cap - jnp.sum(gt, axis=0, keepdims=True)
    sel = gt + eq * (_prefix_excl(eq, tri, tb)[0] < need).astype(F32)
    slot, offs = _prefix_excl(sel, tri, tb)
    sp = jnp.where(sel > 0.0, slot, -1.0)
    pos_ref[0] = sp
    offs_ref[0] = offs
    for blk in range(sp.shape[0] // tb):
        post_ref[0, :, blk * tb:(blk + 1) * tb] = sp[blk * tb:(blk + 1) * tb].T[0:n_exp]


def select_tokens(aff, afft, cap):
    B, S, _ = aff.shape
    n_exp = afft.shape[1]
    tb = min(S, 256)
    r = lax.broadcasted_iota(I32, (tb, tb), 0)
    c = lax.broadcasted_iota(I32, (tb, tb), 1)
    tri = (c < r).astype(BF16)
    nblk = S // tb
    pos, post, offs = pl.pallas_call(
        functools.partial(_select_kernel, cap=cap, tb=tb, n_exp=n_exp),
        grid=(B,),
        in_specs=[pl.BlockSpec((1, S, LANES), lambda b: (b, 0, 0)),
                  pl.BlockSpec((1, n_exp, S), lambda b: (b, 0, 0)),
                  pl.BlockSpec((tb, tb), lambda b: (0, 0))],
        out_specs=[pl.BlockSpec((1, S, LANES), lambda b: (b, 0, 0)),
                   pl.BlockSpec((1, n_exp, S), lambda b: (b, 0, 0)),
                   pl.BlockSpec((1, nblk + 1, LANES), lambda b: (b, 0, 0))],
        out_shape=[jax.ShapeDtypeStruct((B, S, LANES), F32), jax.ShapeDtypeStruct((B, n_exp, S), F32),
                   jax.ShapeDtypeStruct((B, nblk + 1, LANES), F32)],
        compiler_params=_params(("parallel",), 48),
        name="select_tokens",
    )(aff, afft, tri)
    offs = jnp.swapaxes(offs[:, :, :n_exp], 1, 2).astype(I32).reshape(-1)
    return pos, post, offs, tb


def _gather_kernel(offs_ref, x_ref, post_ref, afft_ref, o_ref, g_ref, *, nblk, n_exp, win):
    b = pl.program_id(0)
    t = pl.program_id(1)
    slots = o_ref.shape[2]

    @pl.when(t == 0)
    def _():
        o_ref[...] = jnp.zeros_like(o_ref)
        g_ref[...] = jnp.zeros_like(g_ref)

    x = x_ref[0]
    T = x.shape[0]

    def full():
        r = lax.broadcasted_iota(I32, (slots, T), 0).astype(F32)
        for e in range(n_exp):
            hit = r == post_ref[0, e:e + 1, :]
            o_ref[e, 0] = (o_ref[e, 0].astype(F32) + _dot(hit.astype(BF16), x)).astype(BF16)
            g = jnp.sum(jnp.where(hit, afft_ref[0, e:e + 1, :], 0.0), axis=1, keepdims=True)
            g_ref[e, 0] = g_ref[e, 0] + jnp.broadcast_to(g, (slots, LANES))

    if win >= slots:
        full()
        return
    starts = []
    narrow = None
    for e in range(n_exp):
        base = (b * n_exp + e) * (nblk + 1) + t
        lo = offs_ref[base]
        hi = offs_ref[base + 1]
        a0 = pl.multiple_of(jnp.minimum((lo // BF16_ROWS) * BF16_ROWS, slots - win), BF16_ROWS)
        ok = hi - a0 <= win
        narrow = ok if narrow is None else jnp.logical_and(narrow, ok)
        starts.append(a0)

    @pl.when(narrow)
    def _():
        r = lax.broadcasted_iota(I32, (win, T), 0).astype(F32)
        hits = [r + starts[e].astype(F32) == post_ref[0, e:e + 1, :] for e in range(n_exp)]
        rows = _dot(jnp.concatenate([h.astype(BF16) for h in hits], axis=0), x)
        old = [o_ref[e, 0, pl.ds(starts[e], win), :] for e in range(n_exp)]
        old_g = [g_ref[e, 0, pl.ds(starts[e], win), :] for e in range(n_exp)]
        for e in range(n_exp):
            o_ref[e, 0, pl.ds(starts[e], win), :] = (old[e].astype(F32) + rows[e * win:(e + 1) * win]).astype(BF16)
            g = jnp.sum(jnp.where(hits[e], afft_ref[0, e:e + 1, :], 0.0), axis=1, keepdims=True)
            g_ref[e, 0, pl.ds(starts[e], win), :] = old_g[e] + jnp.broadcast_to(g, (win, LANES))

    pl.when(jnp.logical_not(narrow))(full)


def gather_tokens(x, post, afft, offs, slots, T):
    B, S, D = x.shape
    E = post.shape[1]
    row = pl.BlockSpec((1, E, T), lambda b, t, offs: (b, 0, t))
    return pl.pallas_call(
        functools.partial(_gather_kernel, nblk=S // T, n_exp=E, win=min(slots, GATHER_WINDOW)),
        grid_spec=pltpu.PrefetchScalarGridSpec(
            num_scalar_prefetch=1,
            grid=(B, S // T),
            in_specs=[pl.BlockSpec((1, T, D), lambda b, t, offs: (b, t, 0)), row, row],
            out_specs=[pl.BlockSpec((E, 1, slots, D), lambda b, t, offs: (0, b, 0, 0)),
                       pl.BlockSpec((E, 1, slots, LANES), lambda b, t, offs: (0, b, 0, 0))]),
        out_shape=[jax.ShapeDtypeStruct((E, B, slots, D), BF16), jax.ShapeDtypeStruct((E, B, slots, LANES), F32)],
        compiler_params=_params(("parallel", "arbitrary"), 56),
        name="gather_tokens",
    )(offs, x, post, afft)


def _ffn_kernel(xs_ref, g_ref, wg_ref, wu_ref, wd_ref, o_ref, acc_ref, *, nf):
    f = pl.program_id(2)
    x = xs_ref[0]
    a = _dot(x, wg_ref[...].astype(BF16))
    u = _dot(x, wu_ref[...].astype(BF16))
    hm = (a * jax.nn.sigmoid(a) * u).astype(BF16)
    part = _dot(hm, wd_ref[...].astype(BF16))

    if nf == 1:
        o_ref[0] = (part * g_ref[0, :, 0:1]).astype(BF16)
        return

    @pl.when(f == 0)
    def _():
        acc_ref[...] = part

    @pl.when(jnp.logical_and(f > 0, f < nf - 1))
    def _():
        acc_ref[...] += part

    @pl.when(f == nf - 1)
    def _():
        o_ref[0] = ((acc_ref[...] + part) * g_ref[0, :, 0:1]).astype(BF16)


def expert_ffn(xs, gs, w_gate, w_up, w_down, layer, tm, tf):
    E, M, D = xs.shape
    F = w_gate.shape[3]
    nf = F // tf
    return pl.pallas_call(
        functools.partial(_ffn_kernel, nf=nf),
        grid=(E, M // tm, nf),
        in_specs=[pl.BlockSpec((1, tm, D), lambda e, m, f: (e, m, 0)),
                  pl.BlockSpec((1, tm, LANES), lambda e, m, f: (e, m, 0)),
                  pl.BlockSpec((None, None, D, tf), lambda e, m, f: (layer, e, 0, f)),
                  pl.BlockSpec((None, None, D, tf), lambda e, m, f: (layer, e, 0, f)),
                  pl.BlockSpec((None, None, tf, D), lambda e, m, f: (layer, e, f, 0))],
        out_specs=pl.BlockSpec((1, tm, D), lambda e, m, f: (e, m, 0)),
        out_shape=jax.ShapeDtypeStruct((E, M, D), BF16),
        scratch_shapes=[pltpu.VMEM((tm, D), F32)],
        compiler_params=_params(("parallel", "parallel", "arbitrary"), 56),
        name="expert_ffn",
    )(xs, gs, w_gate, w_up, w_down)


def _combine_kernel(offs_ref, y_ref, pos_ref, h_ref, gate_ref, o_ref, ybuf_ref, *, nblk, n_exp, win):
    b = pl.program_id(0)
    t = pl.program_id(1)
    T = pos_ref.shape[1]
    slots = y_ref.shape[2]

    def full():
        if slots < LANES:
            K = n_exp * slots
            col = lax.broadcasted_iota(I32, (LANES, K), 1).astype(F32)
            owner = jnp.floor((col + 0.5) * (1.0 / slots))
            rep = owner == lax.broadcasted_iota(I32, (LANES, K), 0).astype(F32)
            slot_rep = _dot(pos_ref[0].astype(BF16), rep.astype(BF16))
            want = (col - owner * slots)[0:1]
            acc = _dot((slot_rep == want).astype(BF16), y_ref[:, 0].reshape(K, y_ref.shape[3]))
        else:
            r = lax.broadcasted_iota(I32, (T, slots), 1).astype(F32)
            acc = None
            for e in range(n_exp):
                part = _dot((pos_ref[0, :, e:e + 1] == r).astype(BF16), y_ref[e, 0])
                acc = part if acc is None else acc + part
        o_ref[0] = h_ref[0] + gate_ref[0] * acc

    if win >= slots:
        full()
        return
    starts = []
    narrow = None
    for e in range(n_exp):
        base = (b * n_exp + e) * (nblk + 1) + t
        lo = offs_ref[base]
        hi = offs_ref[base + 1]
        a0 = jnp.minimum((lo // BF16_ROWS) * BF16_ROWS, slots - win)
        ok = hi - a0 <= win
        narrow = ok if narrow is None else jnp.logical_and(narrow, ok)
        starts.append(a0)

    @pl.when(narrow)
    def _():
        r = lax.broadcasted_iota(I32, (T, win), 1).astype(F32)
        pieces = []
        for e in range(n_exp):
            a0 = starts[e]
            ybuf_ref[e * win:(e + 1) * win, :] = y_ref[e, 0, pl.ds(pl.multiple_of(a0, BF16_ROWS), win), :]
            pieces.append((pos_ref[0, :, e:e + 1] - a0.astype(F32) == r).astype(BF16))
        o_ref[0] = h_ref[0] + gate_ref[0] * _dot(jnp.concatenate(pieces, axis=1), ybuf_ref[...])

    pl.when(jnp.logical_not(narrow))(full)


def combine_tokens(y, pos, offs, h, gate, T):
    E, B, slots, D = y.shape
    S = h.shape[1]
    win = min(slots, COMBINE_WINDOW)
    return pl.pallas_call(
        functools.partial(_combine_kernel, nblk=S // T, n_exp=E, win=win),
        grid_spec=pltpu.PrefetchScalarGridSpec(
            num_scalar_prefetch=1,
            grid=(B, S // T),
            in_specs=[pl.BlockSpec((E, 1, slots, D), lambda b, t, offs: (0, b, 0, 0)),
                      pl.BlockSpec((1, T, LANES), lambda b, t, offs: (b, t, 0)),
                      pl.BlockSpec((1, T, D), lambda b, t, offs: (b, t, 0)),
                      pl.BlockSpec((1, 1, D), lambda b, t, offs: (b, 0, 0))],
            out_specs=pl.BlockSpec((1, T, D), lambda b, t, offs: (b, t, 0)),
            scratch_shapes=[pltpu.VMEM((E * win, D), BF16)]),
        out_shape=jax.ShapeDtypeStruct((B, S, D), F32),
        compiler_params=_params(("parallel", "parallel"), 56),
        name="combine_tokens",
    )(offs, y, pos, h, gate)


def moe_block(h, g, scale, shift, gate, router_w, w_gate, w_up, w_down, layer):
    B, S, D = h.shape
    E = router_w.shape[1]
    cap = EC_CAPACITY * S // E
    slots = cap
    x, aff, afft = ffn_prep(h, g, scale, shift, router_w, min(S, 512))
    pos, post, offs, tb = select_tokens(aff, afft, cap)
    xs, gs = gather_tokens(x, post, afft, offs, slots, tb)
    M = B * slots
    y = expert_ffn(xs.reshape(E, M, D), gs.reshape(E, M, LANES), w_gate, w_up, w_down, layer, min(M, 1024), 512)
    return combine_tokens(y.reshape(E, B, slots, D), pos, offs, h, gate, tb)


def kernel(x, c, ctx, c_ctx, ada_w, ada_b, norm_mix_g, norm_ffn_g, w_in, w_out, hy_conv_w, hy_conv_b, hy_f_w1, hy_f_b1, hy_f_w2, hy_f_b2, hy_f_w3, hy_f_b3, hy_f_freq, hy_f_wout, hy_bias, q_norm_g, k_norm_g, diff_lambda, subln_g, pool_w, pool_scale, router_w, exp_w_gate, exp_w_up, exp_w_down):
    B, S, D = x.shape
    Lc = ctx.shape[1]
    depth = ada_w.shape[0]
    hy_width = hy_bias.shape[2]
    hy_proj = (HYENA_ORDER + 1) * hy_width
    qk_width = DIFF_HEADS * 2 * DIFF_HEAD_DIM
    last_attn = ((depth - 1) // 2) * 2

    s_all = jnp.concatenate([jax.nn.silu(c), jax.nn.silu(c_ctx)[None, :]], axis=0)
    rows = -(-s_all.shape[0] // SUBLANES) * SUBLANES
    mods = ada_modulation(jnp.pad(s_all, ((0, rows - s_all.shape[0]), (0, 0))), ada_w, ada_b)

    nblk_lat = max(1, S // HYENA_BLOCK)
    tables_lat = dft_tables(S // nblk_lat)
    tables_ctx = dft_tables(Lc)
    rope_lat = rope_tables(S, True)
    rope_ctx = rope_tables(Lc, False)

    h, hc = x, ctx
    for l in range(depth):
        m = [mods[l, :B, i * D:(i + 1) * D].reshape(B, 1, D) for i in range(6)]
        mc = [jnp.broadcast_to(mods[l, B, i * D:(i + 1) * D].reshape(1, 1, D), (B, 1, D)) for i in range(6)]
        ctx_full = l < last_attn
        if l % 2 == 0:
            e = l // 2
            lam_init = 0.8 - 0.6 * math.exp(-0.3 * l)
            lv = diff_lambda[e]
            lam = jnp.exp(jnp.sum(lv[0] * lv[1])) - jnp.exp(jnp.sum(lv[2] * lv[3])) + lam_init
            filt = (hy_f_w1[e], hy_f_b1[e], hy_f_w2[e], hy_f_b2[e], hy_f_w3[e], hy_f_b3[e], hy_f_freq[e], hy_f_wout[e])
            w_in_b = w_in[e].astype(BF16)
            w_out_b = w_out[e].astype(BF16)

            p, q, k, v = in_proj(h, norm_mix_g[l], m[1], m[0], w_in_b, hy_proj, True, *rope_lat,
                                 q_norm_g[e], k_norm_g[e], 512)
            gts = short_conv(p, hy_conv_w[e], hy_conv_b[e], 0, 2 * hy_width, 256, F32)
            hv = short_conv(p, hy_conv_w[e], hy_conv_b[e], 2 * hy_width, hy_width, 256, BF16)
            hy = hyena_operator(gts, hv, tables_lat, *hyena_filter_taps(S, *filt, hy_width), hy_bias[e], nblk_lat, 256)

            if l <= last_attn:
                if ctx_full:
                    pc, qc, kc, vc = in_proj(hc, norm_mix_g[l], mc[1], mc[0], w_in_b, hy_proj, True, *rope_ctx,
                                             q_norm_g[e], k_norm_g[e], Lc)
                else:
                    kc, vc = in_proj(hc, norm_mix_g[l], mc[1], mc[0], w_in_b[:, hy_proj + qk_width:], 0, False,
                                     *rope_ctx, q_norm_g[e], k_norm_g[e], Lc)
            o = diff_attention(lam, q, kc, vc, k, v, subln_g[e], 1.0 - lam_init, min(S, 512))
            h = out_proj(hy, o, w_out_b, h, m[2], 512)
            if ctx_full:
                gtc = short_conv(pc, hy_conv_w[e], hy_conv_b[e], 0, 2 * hy_width, 256, F32)
                hvc = short_conv(pc, hy_conv_w[e], hy_conv_b[e], 2 * hy_width, hy_width, 256, BF16)
                hyc = hyena_operator(gtc, hvc, tables_ctx, *hyena_filter_taps(Lc, *filt, hy_width), hy_bias[e], 1, Lc)
                oc = diff_attention(lam, qc, kc, vc, None, None, subln_g[e], 1.0 - lam_init, Lc)
                hc = out_proj(hyc, oc, w_out_b, hc, mc[2], Lc)
        else:
            o_idx = l // 2
            pw = pool_w[o_idx].astype(BF16)
            h = pool_mixer(h, norm_mix_g[l], m[1], m[0], m[2], pw, pool_scale[o_idx], 512)
            if ctx_full:
                hc = pool_mixer(hc, norm_mix_g[l], mc[1], mc[0], mc[2], pw, pool_scale[o_idx], Lc)
        h = moe_block(h, norm_ffn_g[l], m[4], m[3], m[5], router_w[l], exp_w_gate, exp_w_up, exp_w_down, l)
        if ctx_full:
            hc = moe_block(hc, norm_ffn_g[l], mc[4], mc[3], mc[5], router_w[l], exp_w_gate, exp_w_up, exp_w_down, l)
    return h
```

```python
import functools
import math

import jax
import jax.numpy as jnp
from jax import lax
from jax.experimental import pallas as pl
from jax.experimental.pallas import tpu as pltpu

F32 = jnp.float32
BF16 = jnp.bfloat16
I32 = jnp.int32

NORM_EPS = 1e-6
GRID_W = 64
HYENA_ORDER = 2
FILTER_EMB = 33
FILTER_FAST_DECAY = 0.3
FILTER_SLOW_DECAY = 1.5
FILTER_TARGET = 1e-2
DIFF_HEADS = 4
DIFF_HEAD_DIM = 64
ROPE_BASE = 10000.0
POOL_WINDOWS = (2, 4, 8, 16)
EC_CAPACITY = 2
LANES = 128
SUBLANES = 8
POOL_HALO = 8
LOG2E = 1.4426950408889634
ATTN_KEY_CHUNK = 2048
GATHER_WINDOW = 80
COMBINE_WINDOW = 128
BF16_ROWS = 16
HYENA_BLOCK = 1024


def _params(sem, vmem_mb):
    return pltpu.CompilerParams(dimension_semantics=sem, vmem_limit_bytes=vmem_mb * 1024 * 1024)


def _dot(a, b):
    return jnp.dot(a, b, preferred_element_type=F32)


def _norm_mod(x, g, scale, shift):
    ms = jnp.mean(x * x, axis=-1, keepdims=True)
    return (x * lax.rsqrt(ms + NORM_EPS) * g) * (1.0 + scale) + shift


def _mm_kernel(a_ref, b_ref, o_ref, acc_ref, *, nk):
    k = pl.program_id(2)
    part = _dot(a_ref[...].astype(BF16), b_ref[...].astype(BF16))

    @pl.when(k == 0)
    def _():
        acc_ref[...] = part

    @pl.when(k > 0)
    def _():
        acc_ref[...] += part

    @pl.when(k == nk - 1)
    def _():
        o_ref[...] = acc_ref[...].astype(o_ref.dtype)


def matmul(a, b, tm, tn, tk):
    M, K = a.shape
    N = b.shape[1]
    nk = K // tk
    return pl.pallas_call(
        functools.partial(_mm_kernel, nk=nk),
        grid=(M // tm, N // tn, nk),
        in_specs=[pl.BlockSpec((tm, tk), lambda i, j, k: (i, k)),
                  pl.BlockSpec((tk, tn), lambda i, j, k: (k, j))],
        out_specs=pl.BlockSpec((tm, tn), lambda i, j, k: (i, j)),
        out_shape=jax.ShapeDtypeStruct((M, N), F32),
        scratch_shapes=[pltpu.VMEM((tm, tn), F32)],
        compiler_params=_params(("parallel", "parallel", "arbitrary"), 40),
        name="matmul",
    )(a, b)


def _ada_kernel(s_ref, w_ref, b_ref, o_ref):
    o_ref[...] = _dot(s_ref[...].astype(BF16), w_ref[...].astype(BF16)) + b_ref[...]


def ada_modulation(s, ada_w, ada_b):
    depth, D, N = ada_w.shape
    R = s.shape[0]
    tn = 1024
    return pl.pallas_call(
        _ada_kernel,
        grid=(depth, N // tn),
        in_specs=[pl.BlockSpec((R, D), lambda l, j: (0, 0)),
                  pl.BlockSpec((None, D, tn), lambda l, j: (l, 0, j)),
                  pl.BlockSpec((None, 1, tn), lambda l, j: (l, 0, j))],
        out_specs=pl.BlockSpec((None, R, tn), lambda l, j: (l, 0, j)),
        out_shape=jax.ShapeDtypeStruct((depth, R, N), F32),
        compiler_params=_params(("parallel", "parallel"), 32),
        name="ada_modulation",
    )(s, ada_w, ada_b.reshape(depth, 1, N))


def _in_proj_kernel(h_ref, g_ref, sc_ref, sh_ref, w_ref, cos_ref, sin_ref, qg_ref, kg_ref, *o_refs, hy_cols, has_q, width):
    a = _norm_mod(h_ref[0], g_ref[...], sc_ref[0], sh_ref[0])
    p = _dot(a.astype(BF16), w_ref[...])
    cos = cos_ref[...]
    sin = sin_ref[...]
    outs = list(o_refs)
    col = 0
    if hy_cols:
        outs.pop(0)[0] = p[:, :hy_cols]
        col = hy_cols
    if has_q:
        q_ref = outs.pop(0)
        for hd in range(width // LANES):
            sl = slice(col + hd * LANES, col + (hd + 1) * LANES)
            q_ref[0, :, hd * LANES:(hd + 1) * LANES] = _head_norm_rope(
                p[:, sl], qg_ref[...], cos, sin, LOG2E * DIFF_HEAD_DIM ** -0.5).astype(BF16)
        col += width
    k_ref, v_ref = outs
    for hd in range(width // LANES):
        sl = slice(col + hd * LANES, col + (hd + 1) * LANES)
        k_ref[0, :, hd * LANES:(hd + 1) * LANES] = _head_norm_rope(p[:, sl], kg_ref[...], cos, sin, 1.0).astype(BF16)
    col += width
    ones = jnp.ones((p.shape[0], LANES), BF16)
    for hd in range(width // LANES):
        v_ref[0, :, 2 * hd * LANES:(2 * hd + 1) * LANES] = p[:, col + hd * LANES:col + (hd + 1) * LANES].astype(BF16)
        v_ref[0, :, (2 * hd + 1) * LANES:(2 * hd + 2) * LANES] = ones


def in_proj(h, g, scale, shift, w, hy_cols, has_q, cos, sin_signed, q_g, k_g, tm):
    B, S, D = h.shape
    N = w.shape[1]
    width = DIFF_HEADS * 2 * DIFF_HEAD_DIM
    g2 = lambda gg: jnp.concatenate([gg, gg]).reshape(1, LANES)
    mod = pl.BlockSpec((1, 1, D), lambda b, i: (b, 0, 0))
    tab = pl.BlockSpec((tm, LANES), lambda b, i: (i, 0))
    vec = pl.BlockSpec((1, LANES), lambda b, i: (0, 0))
    widths = ([hy_cols] if hy_cols else []) + ([width] if has_q else []) + [width, 2 * width]
    dtypes = ([F32] if hy_cols else []) + ([BF16] if has_q else []) + [BF16, BF16]
    return pl.pallas_call(
        functools.partial(_in_proj_kernel, hy_cols=hy_cols, has_q=has_q, width=width),
        grid=(B, S // tm),
        in_specs=[pl.BlockSpec((1, tm, D), lambda b, i: (b, i, 0)),
                  pl.BlockSpec((1, D), lambda b, i: (0, 0)), mod, mod,
                  pl.BlockSpec((D, N), lambda b, i: (0, 0)), tab, tab, vec, vec],
        out_specs=[pl.BlockSpec((1, tm, wd), lambda b, i: (b, i, 0)) for wd in widths],
        out_shape=[jax.ShapeDtypeStruct((B, S, wd), dt) for wd, dt in zip(widths, dtypes)],
        compiler_params=_params(("parallel", "parallel"), 56),
        name="in_proj",
    )(h, g.reshape(1, D), scale, shift, w, cos, sin_signed, g2(q_g), g2(k_g))


def _sconv_kernel(p_ref, w_ref, b_ref, o_ref):
    x = p_ref[0]
    S = x.shape[0]
    row = lax.broadcasted_iota(I32, x.shape, 0)
    xm = jnp.where(row == 0, 0.0, pltpu.roll(x, 1, 0))
    xp = jnp.where(row == S - 1, 0.0, pltpu.roll(x, S - 1, 0))
    w = w_ref[...]
    o_ref[0] = (xm * w[0:1] + x * w[1:2] + xp * w[2:3] + b_ref[...]).astype(o_ref.dtype)


def short_conv(p, conv_w, conv_b, col0, width, tc, out_dtype):
    B, S, _ = p.shape
    c0 = col0 // tc
    return pl.pallas_call(
        _sconv_kernel,
        grid=(B, width // tc),
        in_specs=[pl.BlockSpec((1, S, tc), lambda b, c: (b, 0, c + c0)),
                  pl.BlockSpec((3, tc), lambda b, c: (0, c + c0)),
                  pl.BlockSpec((1, tc), lambda b, c: (0, c + c0))],
        out_specs=pl.BlockSpec((1, S, tc), lambda b, c: (b, 0, c)),
        out_shape=jax.ShapeDtypeStruct((B, S, width), out_dtype),
        compiler_params=_params(("parallel", "parallel"), 48),
        name="short_conv",
    )(p, conv_w, conv_b.reshape(1, -1))


def _dft_fwd_kernel(c_ref, s_ref, u_ref, kre_ref, kim_ref, y_ref, *, nblk):
    c = c_ref[...]
    s = s_ref[...]
    C = u_ref.shape[3]
    ure, uim = [], []
    for j in range(nblk):
        u = u_ref[0, j]
        ure.append(_dot(c, u))
        uim.append(-_dot(s, u))
    for i in range(nblk):
        yre = yim = None
        for j in range(nblk):
            d = i - j + nblk - 1
            kre = kre_ref[:, d * C:(d + 1) * C]
            kim = kim_ref[:, d * C:(d + 1) * C]
            tre = kre * ure[j] - kim * uim[j]
            tim = kre * uim[j] + kim * ure[j]
            yre = tre if yre is None else yre + tre
            yim = tim if yim is None else yim + tim
        y_ref[0, i, 0] = yre.astype(BF16)
        y_ref[0, i, 1] = yim.astype(BF16)


def dft_forward(cf, sf, u, kre, kim, order, tf):
    B, nblk, Lb, C = u.shape
    nd = 2 * nblk - 1
    return pl.pallas_call(
        functools.partial(_dft_fwd_kernel, nblk=nblk),
        grid=(Lb // tf, B),
        in_specs=[pl.BlockSpec((tf, Lb), lambda f, b: (f, 0)),
                  pl.BlockSpec((tf, Lb), lambda f, b: (f, 0)),
                  pl.BlockSpec((1, nblk, Lb, C), lambda f, b: (b, 0, 0, 0)),
                  pl.BlockSpec((tf, nd * C), lambda f, b: (f, order)),
                  pl.BlockSpec((tf, nd * C), lambda f, b: (f, order))],
        out_specs=pl.BlockSpec((1, nblk, 2, tf, C), lambda f, b: (b, 0, 0, f, 0)),
        out_shape=jax.ShapeDtypeStruct((B, nblk, 2, Lb, C), BF16),
        compiler_params=_params(("parallel", "parallel"), 56),
        name="dft_forward",
    )(cf, sf, u, kre, kim)


def _dft_inv_kernel(ct_ref, st_ref, y_ref, gate_ref, o_ref, *, scale):
    acc = _dot(ct_ref[...], y_ref[0, 0, 0]) - _dot(st_ref[...], y_ref[0, 0, 1])
    o_ref[0, 0] = (gate_ref[0, 0] * (acc * scale)).astype(BF16)


def dft_inverse(ct, st, y, gates, gate_col):
    B, nblk, _, Lb, C = y.shape
    return pl.pallas_call(
        functools.partial(_dft_inv_kernel, scale=1.0 / Lb),
        grid=(B, nblk),
        in_specs=[pl.BlockSpec((Lb, Lb), lambda b, i: (0, 0)),
                  pl.BlockSpec((Lb, Lb), lambda b, i: (0, 0)),
                  pl.BlockSpec((1, 1, 2, Lb, C), lambda b, i: (b, i, 0, 0, 0)),
                  pl.BlockSpec((1, 1, Lb, C), lambda b, i: (b, i, 0, gate_col))],
        out_specs=pl.BlockSpec((1, 1, Lb, C), lambda b, i: (b, i, 0, 0)),
        out_shape=jax.ShapeDtypeStruct((B, nblk, Lb, C), BF16),
        compiler_params=_params(("parallel", "parallel"), 48),
        name="dft_inverse",
    )(ct, st, y, gates)


def dft_tables(L):
    n = 2 * L
    f = lax.broadcasted_iota(I32, (L, L), 0)
    t = lax.broadcasted_iota(I32, (L, L), 1)
    m = ((2 * f + 1) * t) % (2 * n)
    ang = m.astype(F32) * (math.pi / n)
    cf = jnp.cos(ang)
    sf = jnp.sin(ang)
    return cf.astype(BF16), sf.astype(BF16), cf.T.astype(BF16), sf.T.astype(BF16)


def hyena_filter_taps(L, w1, b1, w2, b2, w3, b3, freq, wout, width):
    hp = lax.Precision.HIGHEST
    t = jnp.linspace(0.0, 1.0, L, dtype=F32)[:, None]
    bands = (FILTER_EMB - 1) // 2
    w = 2.0 * math.pi * jnp.arange(L, dtype=F32)[:, None] / L
    f = jnp.linspace(1e-4, bands - 1, bands, dtype=F32)[None, :]
    z = jnp.concatenate([t, jnp.cos(f * w), -jnp.sin(f * w)], axis=-1)
    h = jnp.sin(freq * (jnp.dot(z, w1, precision=hp) + b1))
    h = jnp.sin(freq * (jnp.dot(h, w2, precision=hp) + b2))
    h = jnp.sin(freq * (jnp.dot(h, w3, precision=hp) + b3))
    max_decay = math.log(FILTER_TARGET) / FILTER_FAST_DECAY
    min_decay = math.log(FILTER_TARGET) / FILTER_SLOW_DECAY
    deltas = jnp.abs(jnp.linspace(min_decay, max_decay, width, dtype=F32))

    def taps(hh, tt):
        return jnp.dot(hh, wout, precision=hp).reshape(L, HYENA_ORDER, 2, width) * jnp.exp(-tt * deltas[None, :])[:, None, None, :]

    return taps(h, t), taps(h[::-1], t[::-1])


def filter_spectra(cf, sf, taps, taps_rev, bias, nblk):
    L, C = taps.shape[0], taps.shape[3]
    Lb = L // nblk
    zero = jnp.zeros((1, C), F32)
    sums, diffs = [], []
    for o in range(HYENA_ORDER):
        h_fwd, h_bwd = taps[:, o, 0], taps[:, o, 1]
        h_fwd_rev, h_bwd_rev = taps_rev[:, o, 0], taps_rev[:, o, 1]
        k0 = (h_fwd[0] + h_bwd[0] + bias[o])[None, :]
        kfull = jnp.concatenate([zero, h_bwd_rev[:L - 1], k0, h_fwd[1:]], axis=0)
        krev = jnp.concatenate([h_fwd_rev[:L - 1], k0, h_bwd[1:], zero], axis=0)
        for dlt in range(-(nblk - 1), nblk):
            base = L + dlt * Lb
            kp = kfull[base:base + Lb]
            km = jnp.concatenate([zero, krev[2 * L - base:2 * L - base + Lb - 1]], axis=0)
            sums.append(kp + km)
            diffs.append(km - kp)
    tmm = min(Lb, 512)
    kre = matmul(cf, jnp.concatenate(sums, axis=1), tmm, tmm, tmm)
    kim = matmul(sf, jnp.concatenate(diffs, axis=1), tmm, tmm, tmm)
    return kre, kim


def hyena_operator(gates, v, tables, taps, taps_rev, bias, nblk, tf):
    cf, sf, ct, st = tables
    B, L, C = v.shape
    Lb = L // nblk
    kre, kim = filter_spectra(cf, sf, taps, taps_rev, bias, nblk)
    gates = gates.reshape(B, nblk, Lb, 2 * C)
    z = v.reshape(B, nblk, Lb, C)
    for o in range(HYENA_ORDER):
        y = dft_forward(cf, sf, z, kre, kim, o, tf)
        z = dft_inverse(ct, st, y, gates, o)
    return z.reshape(B, L, C)


def _head_norm_rope(x, g, cos, sin_signed, scale):
    lane = lax.broadcasted_iota(I32, x.shape, 1)
    lo = lane < DIFF_HEAD_DIM
    x2 = x * x
    s_lo = jnp.sum(jnp.where(lo, x2, 0.0), axis=-1, keepdims=True)
    s_hi = jnp.sum(jnp.where(lo, 0.0, x2), axis=-1, keepdims=True)
    ms = jnp.where(lo, s_lo, s_hi) * (1.0 / DIFF_HEAD_DIM)
    xn = x * lax.rsqrt(ms + NORM_EPS) * g
    first = (lane & 16) == 0
    partner = jnp.where(first, pltpu.roll(xn, LANES - 16, 1), pltpu.roll(xn, 16, 1))
    return (xn * cos + partner * sin_signed) * scale


def rope_tables(S, use_rope):
    if not use_rope:
        return jnp.ones((S, LANES), F32), jnp.zeros((S, LANES), F32)
    t = jnp.arange(S, dtype=I32)
    row = (t // GRID_W).astype(F32)[:, None]
    colp = (t % GRID_W).astype(F32)[:, None]
    nf = DIFF_HEAD_DIM // 4
    inv = ROPE_BASE ** (-jnp.arange(nf, dtype=F32) / nf)
    lane = jnp.arange(LANES)
    grp = (lane % DIFF_HEAD_DIM) // nf
    j = lane % nf
    pos = jnp.where((grp < 2)[None, :], row, colp)
    ang = pos * inv[j][None, :]
    sign = jnp.where((grp % 2 == 0)[None, :], -1.0, 1.0)
    return jnp.cos(ang), jnp.sin(ang) * sign


def _attn_kernel(lam_ref, q_ref, kc_ref, vc_ref, *rest, has_lat, out_scale, ck):
    if has_lat:
        kl_ref, vl_ref, g_ref, o_ref = rest
    else:
        g_ref, o_ref = rest
    lam = lam_ref[0]
    q = q_ref[0]
    lane = lax.broadcasted_iota(I32, q.shape, 1)
    nt = (((1,), (1,)), ((), ()))
    zero = jnp.zeros_like(q)
    qm = [jnp.where(lane < DIFF_HEAD_DIM, q, zero), jnp.where(lane >= DIFF_HEAD_DIM, q, zero)]
    chunks = [(kc_ref, vc_ref, 0, kc_ref.shape[1])]
    if has_lat:
        chunks += [(kl_ref, vl_ref, c * ck, ck) for c in range(kl_ref.shape[1] // ck)]

    def scores(ch):
        k = ch[0][0, ch[2]:ch[2] + ch[3], :]
        return [lax.dot_general(qm[mp], k, nt, preferred_element_type=F32) for mp in range(2)]

    m, acc = [None, None], [None, None]
    s_next = scores(chunks[0])
    for ci, ch in enumerate(chunks):
        s_cur = s_next
        if ci + 1 < len(chunks):
            s_next = scores(chunks[ci + 1])
        v1 = ch[1][0, ch[2]:ch[2] + ch[3], :]
        for mp in range(2):
            s = s_cur[mp]
            mx = jnp.max(s, axis=-1, keepdims=True)
            m_new = mx if ci == 0 else jnp.maximum(m[mp], mx)
            pv = _dot(jnp.exp2(s - m_new).astype(BF16), v1)
            acc[mp] = pv if ci == 0 else jnp.exp2(m[mp] - m_new) * acc[mp] + pv
            m[mp] = m_new
    o = acc[0][:, :LANES] / acc[0][:, LANES:] - lam * (acc[1][:, :LANES] / acc[1][:, LANES:])
    ms = jnp.mean(o * o, axis=-1, keepdims=True)
    o_ref[0] = ((o * lax.rsqrt(ms + NORM_EPS) * g_ref[...]) * out_scale).astype(BF16)


def diff_attention(lam, q, k_ctx, v_ctx, k_lat, v_lat, subln_g, out_scale, tq):
    B, Sq, W = q.shape
    H = W // LANES
    has_lat = k_lat is not None
    Sc = k_ctx.shape[1]
    head = lambda S, w: pl.BlockSpec((1, S, w), lambda b, h, i: (b, 0, h))
    in_specs = [pl.BlockSpec(memory_space=pltpu.SMEM),
                pl.BlockSpec((1, tq, LANES), lambda b, h, i: (b, i, h)), head(Sc, LANES), head(Sc, 2 * LANES)]
    args = [lam.reshape(1), q, k_ctx, v_ctx]
    if has_lat:
        in_specs += [head(k_lat.shape[1], LANES), head(k_lat.shape[1], 2 * LANES)]
        args += [k_lat, v_lat]
    in_specs.append(pl.BlockSpec((1, LANES), lambda b, h, i: (0, 0)))
    args.append(subln_g.reshape(1, LANES))
    return pl.pallas_call(
        functools.partial(_attn_kernel, has_lat=has_lat, out_scale=out_scale,
                          ck=min(ATTN_KEY_CHUNK, k_lat.shape[1]) if has_lat else 0),
        grid=(B, H, Sq // tq),
        in_specs=in_specs,
        out_specs=pl.BlockSpec((1, tq, LANES), lambda b, h, i: (b, i, h)),
        out_shape=jax.ShapeDtypeStruct((B, Sq, W), BF16),
        compiler_params=_params(("parallel", "parallel", "parallel"), 56),
        name="diff_attention",
    )(*args)


def _oproj_kernel(hy_ref, o_ref, w_ref, h_ref, gate_ref, out_ref, *, half):
    y = _dot(hy_ref[0].astype(BF16), w_ref[0:half, :]) + _dot(o_ref[0], w_ref[half:, :])
    out_ref[0] = h_ref[0] + gate_ref[0] * y


def out_proj(hy, o, w_out, h, gate, tm):
    B, S, D = h.shape
    half = hy.shape[2]
    return pl.pallas_call(
        functools.partial(_oproj_kernel, half=half),
        grid=(B, S // tm),
        in_specs=[pl.BlockSpec((1, tm, half), lambda b, i: (b, i, 0)),
                  pl.BlockSpec((1, tm, half), lambda b, i: (b, i, 0)),
                  pl.BlockSpec((2 * half, D), lambda b, i: (0, 0)),
                  pl.BlockSpec((1, tm, D), lambda b, i: (b, i, 0)),
                  pl.BlockSpec((1, 1, D), lambda b, i: (b, 0, 0))],
        out_specs=pl.BlockSpec((1, tm, D), lambda b, i: (b, i, 0)),
        out_shape=jax.ShapeDtypeStruct((B, S, D), F32),
        compiler_params=_params(("parallel", "parallel"), 40),
        name="out_proj",
    )(hy, o, w_out, h, gate)


def _shift_rows(x, d):
    return pltpu.roll(x, (-d) % x.shape[0], 0)


def _pool_kernel(hp_ref, hc_ref, hn_ref, g_ref, sc_ref, sh_ref, gate_ref, pw_ref, ps_ref, o_ref, *, T, L):
    i = pl.program_id(1)
    nt = pl.num_programs(1)
    g, sc, sh = g_ref[...], sc_ref[0], sh_ref[0]
    hc = hc_ref[0]
    a_c = _norm_mod(hc, g, sc, sh)
    a_p = jnp.where(i == 0, 0.0, _norm_mod(hp_ref[0], g, sc, sh))
    a_n = jnp.where(i == nt - 1, 0.0, _norm_mod(hn_ref[0], g, sc, sh))
    ext = jnp.concatenate([a_p, a_c, a_n], axis=0)
    tok = i * T + lax.broadcasted_iota(I32, (T, 1), 0)
    G = ext.shape[1] // len(POOL_WINDOWS)
    ys = []
    for gi, w in enumerate(POOL_WINDOWS):
        xg = ext[:, gi * G:(gi + 1) * G]
        s = _shift_rows(xg, -1) + xg
        step = 1
        while 2 * step < w:
            s = _shift_rows(s, -step) + _shift_rows(s, step)
            step *= 2
        cnt = (jnp.minimum(tok + w // 2, L) - jnp.maximum(tok - w // 2, 0)).astype(F32)
        p = s[POOL_HALO:POOL_HALO + T] / cnt - a_c[:, gi * G:(gi + 1) * G]
        ys.append(_dot(p.astype(BF16), pw_ref[gi]))
    y = jnp.concatenate(ys, axis=1) * ps_ref[...]
    o_ref[0] = hc + gate_ref[0] * y


def pool_mixer(h, g, scale, shift, gate, pool_w, pool_scale, T):
    B, S, D = h.shape
    nh = T // POOL_HALO
    last = S // POOL_HALO - 1
    mod = pl.BlockSpec((1, 1, D), lambda b, i: (b, 0, 0))
    return pl.pallas_call(
        functools.partial(_pool_kernel, T=T, L=S),
        grid=(B, S // T),
        in_specs=[pl.BlockSpec((1, POOL_HALO, D), lambda b, i: (b, jnp.maximum(i * nh - 1, 0), 0)),
                  pl.BlockSpec((1, T, D), lambda b, i: (b, i, 0)),
                  pl.BlockSpec((1, POOL_HALO, D), lambda b, i: (b, jnp.minimum((i + 1) * nh, last), 0)),
                  pl.BlockSpec((1, D), lambda b, i: (0, 0)), mod, mod, mod,
                  pl.BlockSpec(pool_w.shape, lambda b, i: (0, 0, 0)),
                  pl.BlockSpec((1, D), lambda b, i: (0, 0))],
        out_specs=pl.BlockSpec((1, T, D), lambda b, i: (b, i, 0)),
        out_shape=jax.ShapeDtypeStruct((B, S, D), F32),
        compiler_params=_params(("parallel", "parallel"), 48),
        name="pool_mixer",
    )(h, h, h, g.reshape(1, D), scale, shift, gate, pool_w, pool_scale.reshape(1, D))


def _ffn_prep_kernel(h_ref, g_ref, sc_ref, sh_ref, rw_ref, x_ref, aff_ref, afft_ref, *, n_exp):
    a = _norm_mod(h_ref[0], g_ref[...], sc_ref[0], sh_ref[0]).astype(BF16)
    x_ref[0] = a
    logits = _dot(a, rw_ref[...])
    lane = lax.broadcasted_iota(I32, logits.shape, 1)
    valid = lane < n_exp
    mx = jnp.max(jnp.where(valid, logits, -jnp.inf), axis=-1, keepdims=True)
    e = jnp.where(valid, jnp.exp(logits - mx), 0.0)
    aff = e / jnp.sum(e, axis=-1, keepdims=True)
    aff_ref[0] = aff
    afft_ref[0] = aff.T[0:n_exp]


def ffn_prep(h, g, scale, shift, router_w, tm):
    B, S, D = h.shape
    n_exp = router_w.shape[1]
    rw = jnp.pad(router_w, ((0, 0), (0, LANES - n_exp))).astype(BF16)
    mod = pl.BlockSpec((1, 1, D), lambda b, i: (b, 0, 0))
    return pl.pallas_call(
        functools.partial(_ffn_prep_kernel, n_exp=n_exp),
        grid=(B, S // tm),
        in_specs=[pl.BlockSpec((1, tm, D), lambda b, i: (b, i, 0)),
                  pl.BlockSpec((1, D), lambda b, i: (0, 0)), mod, mod,
                  pl.BlockSpec((D, LANES), lambda b, i: (0, 0))],
        out_specs=[pl.BlockSpec((1, tm, D), lambda b, i: (b, i, 0)),
                   pl.BlockSpec((1, tm, LANES), lambda b, i: (b, i, 0)),
                   pl.BlockSpec((1, n_exp, tm), lambda b, i: (b, 0, i))],
        out_shape=[jax.ShapeDtypeStruct((B, S, D), BF16), jax.ShapeDtypeStruct((B, S, LANES), F32),
                   jax.ShapeDtypeStruct((B, n_exp, S), F32)],
        compiler_params=_params(("parallel", "parallel"), 32),
        name="ffn_prep",
    )(h, g.reshape(1, D), scale, shift, rw)


def _prefix_excl(m, tri, tb):
    S = m.shape[0]
    carry = jnp.zeros((1, m.shape[1]), F32)
    outs, carries = [], []
    for blk in range(S // tb):
        mb = m[blk * tb:(blk + 1) * tb]
        outs.append(_dot(tri, mb.astype(BF16)) + carry)
        carries.append(carry)
        carry = carry + jnp.sum(mb, axis=0, keepdims=True)
    carries.append(carry)
    return (jnp.concatenate(outs, axis=0) if len(outs) > 1 else outs[0]), jnp.concatenate(carries, axis=0)


def _select_kernel(aff_ref, afft_ref, tri_ref, pos_ref, post_ref, offs_ref, *, cap, tb, n_exp):
    bits = pltpu.bitcast(afft_ref[0], I32)

    def body(i, cur):
        cand = cur | (jnp.int32(1) << (30 - i))
        cnt = jnp.sum((bits >= cand).astype(F32), axis=1, keepdims=True)
        return jnp.where(cnt >= cap, cand, cur)

    thr_bits = lax.fori_loop(0, 31, body, jnp.zeros((n_exp, 1), I32))
    thr_col = pltpu.bitcast(jnp.broadcast_to(thr_bits, (n_exp, LANES)), F32)
    sub = lax.broadcasted_iota(I32, (n_exp, LANES), 0)
    lane = lax.broadcasted_iota(I32, (n_exp, LANES), 1)
    thr = jnp.sum(jnp.where(sub == lane, thr_col, 0.0), axis=0, keepdims=True)
    aff = aff_ref[0]
    tri = tri_ref[...]
    gt = (aff > thr).astype(F32)
    eq = (aff == thr).astype(F32)
    need = cap - jnp.sum(gt, axis=0, keepdims=True)
    sel = gt + eq * (_prefix_excl(eq, tri, tb)[0] < need).astype(F32)
    slot, offs = _prefix_excl(sel, tri, tb)
    sp = jnp.where(sel > 0.0, slot, -1.0)
    pos_ref[0] = sp
    offs_ref[0] = offs
    for blk in range(sp.shape[0] // tb):
        post_ref[0, :, blk * tb:(blk + 1) * tb] = sp[blk * tb:(blk + 1) * tb].T[0:n_exp]


def select_tokens(aff, afft, cap):
    B, S, _ = aff.shape
    n_exp = afft.shape[1]
    tb = min(S, 256)
    r = lax.broadcasted_iota(I32, (tb, tb), 0)
    c = lax.broadcasted_iota(I32, (tb, tb), 1)
    tri = (c < r).astype(BF16)
    nblk = S // tb
    pos, post, offs = pl.pallas_call(
        functools.partial(_select_kernel, cap=cap, tb=tb, n_exp=n_exp),
        grid=(B,),
        in_specs=[pl.BlockSpec((1, S, LANES), lambda b: (b, 0, 0)),
                  pl.BlockSpec((1, n_exp, S), lambda b: (b, 0, 0)),
                  pl.BlockSpec((tb, tb), lambda b: (0, 0))],
        out_specs=[pl.BlockSpec((1, S, LANES), lambda b: (b, 0, 0)),
                   pl.BlockSpec((1, n_exp, S), lambda b: (b, 0, 0)),
                   pl.BlockSpec((1, nblk + 1, LANES), lambda b: (b, 0, 0))],
        out_shape=[jax.ShapeDtypeStruct((B, S, LANES), F32), jax.ShapeDtypeStruct((B, n_exp, S), F32),
                   jax.ShapeDtypeStruct((B, nblk + 1, LANES), F32)],
        compiler_params=_params(("parallel",), 48),
        name="select_tokens",
    )(aff, afft, tri)
    offs = jnp.swapaxes(offs[:, :, :n_exp], 1, 2).astype(I32).reshape(-1)
    return pos, post, offs, tb


def _gather_kernel(offs_ref, x_ref, post_ref, afft_ref, o_ref, g_ref, *, nblk, n_exp, win):
    b = pl.program_id(0)
    t = pl.program_id(1)
    slots = o_ref.shape[2]

    @pl.when(t == 0)
    def _():
        o_ref[...] = jnp.zeros_like(o_ref)
        g_ref[...] = jnp.zeros_like(g_ref)

    x = x_ref[0]
    T = x.shape[0]

    def full():
        r = lax.broadcasted_iota(I32, (slots, T), 0).astype(F32)
        for e in range(n_exp):
            hit = r == post_ref[0, e:e + 1, :]
            o_ref[e, 0] = (o_ref[e, 0].astype(F32) + _dot(hit.astype(BF16), x)).astype(BF16)
            g = jnp.sum(jnp.where(hit, afft_ref[0, e:e + 1, :], 0.0), axis=1, keepdims=True)
            g_ref[e, 0] = g_ref[e, 0] + jnp.broadcast_to(g, (slots, LANES))

    if win >= slots:
        full()
        return
    starts = []
    narrow = None
    for e in range(n_exp):
        base = (b * n_exp + e) * (nblk + 1) + t
        lo = offs_ref[base]
        hi = offs_ref[base + 1]
        a0 = pl.multiple_of(jnp.minimum((lo // BF16_ROWS) * BF16_ROWS, slots - win), BF16_ROWS)
        ok = hi - a0 <= win
        narrow = ok if narrow is None else jnp.logical_and(narrow, ok)
        starts.append(a0)

    @pl.when(narrow)
    def _():
        r = lax.broadcasted_iota(I32, (win, T), 0).astype(F32)
        hits = [r + starts[e].astype(F32) == post_ref[0, e:e + 1, :] for e in range(n_exp)]
        rows = _dot(jnp.concatenate([h.astype(BF16) for h in hits], axis=0), x)
        old = [o_ref[e, 0, pl.ds(starts[e], win), :] for e in range(n_exp)]
        old_g = [g_ref[e, 0, pl.ds(starts[e], win), :] for e in range(n_exp)]
        for e in range(n_exp):
            o_ref[e, 0, pl.ds(starts[e], win), :] = (old[e].astype(F32) + rows[e * win:(e + 1) * win]).astype(BF16)
            g = jnp.sum(jnp.where(hits[e], afft_ref[0, e:e + 1, :], 0.0), axis=1, keepdims=True)
            g_ref[e, 0, pl.ds(starts[e], win), :] = old_g[e] + jnp.broadcast_to(g, (win, LANES))

    pl.when(jnp.logical_not(narrow))(full)


def gather_tokens(x, post, afft, offs, slots, T):
    B, S, D = x.shape
    E = post.shape[1]
    row = pl.BlockSpec((1, E, T), lambda b, t, offs: (b, 0, t))
    return pl.pallas_call(
        functools.partial(_gather_kernel, nblk=S // T, n_exp=E, win=min(slots, GATHER_WINDOW)),
        grid_spec=pltpu.PrefetchScalarGridSpec(
            num_scalar_prefetch=1,
            grid=(B, S // T),
            in_specs=[pl.BlockSpec((1, T, D), lambda b, t, offs: (b, t, 0)), row, row],
            out_specs=[pl.BlockSpec((E, 1, slots, D), lambda b, t, offs: (0, b, 0, 0)),
                       pl.BlockSpec((E, 1, slots, LANES), lambda b, t, offs: (0, b, 0, 0))]),
        out_shape=[jax.ShapeDtypeStruct((E, B, slots, D), BF16), jax.ShapeDtypeStruct((E, B, slots, LANES), F32)],
        compiler_params=_params(("parallel", "arbitrary"), 56),
        name="gather_tokens",
    )(offs, x, post, afft)


def _ffn_kernel(xs_ref, g_ref, wg_ref, wu_ref, wd_ref, o_ref, hm_ref, *, nf):
    f = pl.program_id(2)
    x = xs_ref[0]
    a = _dot(x, wg_ref[...].astype(BF16))
    u = _dot(x, wu_ref[...].astype(BF16))
    hm_ref[f] = (a * jax.nn.sigmoid(a) * u).astype(BF16)

    @pl.when(f == nf - 1)
    def _():
        hm = jnp.concatenate([hm_ref[c] for c in range(nf)], axis=1) if nf > 1 else hm_ref[0]
        o_ref[0] = (_dot(hm, wd_ref[...].astype(BF16)) * g_ref[0, :, 0:1]).astype(BF16)


def expert_ffn(xs, gs, w_gate, w_up, w_down, layer, tm, tf):
    E, M, D = xs.shape
    F = w_gate.shape[3]
    nf = F // tf
    return pl.pallas_call(
        functools.partial(_ffn_kernel, nf=nf),
        grid=(E, M // tm, nf),
        in_specs=[pl.BlockSpec((1, tm, D), lambda e, m, f: (e, m, 0)),
                  pl.BlockSpec((1, tm, LANES), lambda e, m, f: (e, m, 0)),
                  pl.BlockSpec((None, None, D, tf), lambda e, m, f: (layer, e, 0, f)),
                  pl.BlockSpec((None, None, D, tf), lambda e, m, f: (layer, e, 0, f)),
                  pl.BlockSpec((None, None, F, D), lambda e, m, f: (layer, e, 0, 0))],
        out_specs=pl.BlockSpec((1, tm, D), lambda e, m, f: (e, m, 0)),
        out_shape=jax.ShapeDtypeStruct((E, M, D), BF16),
        scratch_shapes=[pltpu.VMEM((nf, tm, tf), BF16)],
        compiler_params=_params(("parallel", "parallel", "arbitrary"), 56),
        name="expert_ffn",
    )(xs, gs, w_gate, w_up, w_down)


def _combine_kernel(offs_ref, y_ref, pos_ref, h_ref, gate_ref, o_ref, ybuf_ref, *, nblk, n_exp, win):
    b = pl.program_id(0)
    t = pl.program_id(1)
    T = pos_ref.shape[1]
    slots = y_ref.shape[2]

    def full():
        if slots < LANES:
            K = n_exp * slots
            col = lax.broadcasted_iota(I32, (LANES, K), 1).astype(F32)
            owner = jnp.floor((col + 0.5) * (1.0 / slots))
            rep = owner == lax.broadcasted_iota(I32, (LANES, K), 0).astype(F32)
            slot_rep = _dot(pos_ref[0].astype(BF16), rep.astype(BF16))
            want = (col - owner * slots)[0:1]
            acc = _dot((slot_rep == want).astype(BF16), y_ref[:, 0].reshape(K, y_ref.shape[3]))
        else:
            r = lax.broadcasted_iota(I32, (T, slots), 1).astype(F32)
            acc = None
            for e in range(n_exp):
                part = _dot((pos_ref[0, :, e:e + 1] == r).astype(BF16), y_ref[e, 0])
                acc = part if acc is None else acc + part
        o_ref[0] = h_ref[0] + gate_ref[0] * acc

    if win >= slots:
        full()
        return
    starts = []
    narrow = None
    for e in range(n_exp):
        base = (b * n_exp + e) * (nblk + 1) + t
        lo = offs_ref[base]
        hi = offs_ref[base + 1]
        a0 = jnp.minimum((lo // BF16_ROWS) * BF16_ROWS, slots - win)
        ok = hi - a0 <= win
        narrow = ok if narrow is None else jnp.logical_and(narrow, ok)
        starts.append(a0)

    @pl.when(narrow)
    def _():
        r = lax.broadcasted_iota(I32, (T, win), 1).astype(F32)
        pieces = []
        for e in range(n_exp):
            a0 = starts[e]
            ybuf_ref[e * win:(e + 1) * win, :] = y_ref[e, 0, pl.ds(pl.multiple_of(a0, BF16_ROWS), win), :]
            pieces.append((pos_ref[0, :, e:e + 1] - a0.astype(F32) == r).astype(BF16))
        o_ref[0] = h_ref[0] + gate_ref[0] * _dot(jnp.concatenate(pieces, axis=1), ybuf_ref[...])

    pl.when(jnp.logical_not(narrow))(full)


def combine_tokens(y, pos, offs, h, gate, T):
    E, B, slots, D = y.shape
    S = h.shape[1]
    win = min(slots, COMBINE_WINDOW)
    return pl.pallas_call(
        functools.partial(_combine_kernel, nblk=S // T, n_exp=E, win=win),
        grid_spec=pltpu.PrefetchScalarGridSpec(
            num_scalar_prefetch=1,
            grid=(B, S // T),
            in_specs=[pl.BlockSpec((E, 1, slots, D), lambda b, t, offs: (0, b, 0, 0)),
                      pl.BlockSpec((1, T, LANES), lambda b, t, offs: (b, t, 0)),
                      pl.BlockSpec((1, T, D), lambda b, t, offs: (b, t, 0)),
                      pl.BlockSpec((1, 1, D), lambda b, t, offs: (b, 0, 0))],
            out_specs=pl.BlockSpec((1, T, D), lambda b, t, offs: (b, t, 0)),
            scratch_shapes=[pltpu.VMEM((E * win, D), BF16)]),
        out_shape=jax.ShapeDtypeStruct((B, S, D), F32),
        compiler_params=_params(("parallel", "parallel"), 56),
        name="combine_tokens",
    )(offs, y, pos, h, gate)


def moe_block(h, g, scale, shift, gate, router_w, w_gate, w_up, w_down, layer):
    B, S, D = h.shape
    E = router_w.shape[1]
    cap = EC_CAPACITY * S // E
    slots = cap
    x, aff, afft = ffn_prep(h, g, scale, shift, router_w, min(S, 512))
    pos, post, offs, tb = select_tokens(aff, afft, cap)
    xs, gs = gather_tokens(x, post, afft, offs, slots, tb)
    M = B * slots
    y = expert_ffn(xs.reshape(E, M, D), gs.reshape(E, M, LANES), w_gate, w_up, w_down, layer, min(M, 1024), 512)
    return combine_tokens(y.reshape(E, B, slots, D), pos, offs, h, gate, tb)


def kernel(x, c, ctx, c_ctx, ada_w, ada_b, norm_mix_g, norm_ffn_g, w_in, w_out, hy_conv_w, hy_conv_b, hy_f_w1, hy_f_b1, hy_f_w2, hy_f_b2, hy_f_w3, hy_f_b3, hy_f_freq, hy_f_wout, hy_bias, q_norm_g, k_norm_g, diff_lambda, subln_g, pool_w, pool_scale, router_w, exp_w_gate, exp_w_up, exp_w_down):
    B, S, D = x.shape
    Lc = ctx.shape[1]
    depth = ada_w.shape[0]
    hy_width = hy_bias.shape[2]
    hy_proj = (HYENA_ORDER + 1) * hy_width
    qk_width = DIFF_HEADS * 2 * DIFF_HEAD_DIM
    last_attn = ((depth - 1) // 2) * 2

    s_all = jnp.concatenate([jax.nn.silu(c), jax.nn.silu(c_ctx)[None, :]], axis=0)
    rows = -(-s_all.shape[0] // SUBLANES) * SUBLANES
    mods = ada_modulation(jnp.pad(s_all, ((0, rows - s_all.shape[0]), (0, 0))), ada_w, ada_b)

    nblk_lat = max(1, S // HYENA_BLOCK)
    tables_lat = dft_tables(S // nblk_lat)
    tables_ctx = dft_tables(Lc)
    rope_lat = rope_tables(S, True)
    rope_ctx = rope_tables(Lc, False)

    h, hc = x, ctx
    for l in range(depth):
        m = [mods[l, :B, i * D:(i + 1) * D].reshape(B, 1, D) for i in range(6)]
        mc = [jnp.broadcast_to(mods[l, B, i * D:(i + 1) * D].reshape(1, 1, D), (B, 1, D)) for i in range(6)]
        ctx_full = l < last_attn
        if l % 2 == 0:
            e = l // 2
            lam_init = 0.8 - 0.6 * math.exp(-0.3 * l)
            lv = diff_lambda[e]
            lam = jnp.exp(jnp.sum(lv[0] * lv[1])) - jnp.exp(jnp.sum(lv[2] * lv[3])) + lam_init
            filt = (hy_f_w1[e], hy_f_b1[e], hy_f_w2[e], hy_f_b2[e], hy_f_w3[e], hy_f_b3[e], hy_f_freq[e], hy_f_wout[e])
            w_in_b = w_in[e].astype(BF16)
            w_out_b = w_out[e].astype(BF16)

            p, q, k, v = in_proj(h, norm_mix_g[l], m[1], m[0], w_in_b, hy_proj, True, *rope_lat,
                                 q_norm_g[e], k_norm_g[e], 512)
            gts = short_conv(p, hy_conv_w[e], hy_conv_b[e], 0, 2 * hy_width, 256, F32)
            hv = short_conv(p, hy_conv_w[e], hy_conv_b[e], 2 * hy_width, hy_width, 256, BF16)
            hy = hyena_operator(gts, hv, tables_lat, *hyena_filter_taps(S, *filt, hy_width), hy_bias[e], nblk_lat, 256)

            if l <= last_attn:
                if ctx_full:
                    pc, qc, kc, vc = in_proj(hc, norm_mix_g[l], mc[1], mc[0], w_in_b, hy_proj, True, *rope_ctx,
                                             q_norm_g[e], k_norm_g[e], Lc)
                else:
                    kc, vc = in_proj(hc, norm_mix_g[l], mc[1], mc[0], w_in_b[:, hy_proj + qk_width:], 0, False,
                                     *rope_ctx, q_norm_g[e], k_norm_g[e], Lc)
            o = diff_attention(lam, q, kc, vc, k, v, subln_g[e], 1.0 - lam_init, min(S, 512))
            h = out_proj(hy, o, w_out_b, h, m[2], 512)
            if ctx_full:
                gtc = short_conv(pc, hy_conv_w[e], hy_conv_b[e], 0, 2 * hy_width, 256, F32)
                hvc = short_conv(pc, hy_conv_w[e], hy_conv_b[e], 2 * hy_width, hy_width, 256, BF16)
                hyc = hyena_operator(gtc, hvc, tables_ctx, *hyena_filter_taps(Lc, *filt, hy_width), hy_bias[e], 1, Lc)
                oc = diff_attention(lam, qc, kc, vc, None, None, subln_g[e], 1.0 - lam_init, Lc)
                hc = out_proj(hyc, oc, w_out_b, hc, mc[2], Lc)
        else:
            o_idx = l // 2
            pw = pool_w[o_idx].astype(BF16)
            h = pool_mixer(h, norm_mix_g[l], m[1], m[0], m[2], pw, pool_scale[o_idx], 512)
            if ctx_full:
                hc = pool_mixer(hc, norm_mix_g[l], mc[1], mc[0], mc[2], pw, pool_scale[o_idx], Lc)
        h = moe_block(h, norm_ffn_g[l], m[4], m[3], m[5], router_w[l], exp_w_gate, exp_w_up, exp_w_down, l)
        if ctx_full:
            hc = moe_block(hc, norm_ffn_g[l], mc[4], mc[3], mc[5], router_w[l], exp_w_gate, exp_w_up, exp_w_down, l)
    return h
```

```python
import functools
import math

import jax
import jax.numpy as jnp
from jax import lax
from jax.experimental import pallas as pl
from jax.experimental.pallas import tpu as pltpu

F32 = jnp.float32
BF16 = jnp.bfloat16
I32 = jnp.int32

NORM_EPS = 1e-6
GRID_W = 64
HYENA_ORDER = 2
FILTER_EMB = 33
FILTER_FAST_DECAY = 0.3
FILTER_SLOW_DECAY = 1.5
FILTER_TARGET = 1e-2
DIFF_HEADS = 4
DIFF_HEAD_DIM = 64
ROPE_BASE = 10000.0
POOL_WINDOWS = (2, 4, 8, 16)
EC_CAPACITY = 2
LANES = 128
SUBLANES = 8
POOL_HALO = 8
LOG2E = 1.4426950408889634
ATTN_KEY_CHUNK = 2048
GATHER_WINDOW = 80
COMBINE_WINDOW = 128
BF16_ROWS = 16
HYENA_BLOCK = 1024


def _params(sem, vmem_mb):
    return pltpu.CompilerParams(dimension_semantics=sem, vmem_limit_bytes=vmem_mb * 1024 * 1024)


def _dot(a, b):
    return jnp.dot(a, b, preferred_element_type=F32)


def _norm_mod(x, g, scale, shift):
    ms = jnp.mean(x * x, axis=-1, keepdims=True)
    return (x * lax.rsqrt(ms + NORM_EPS) * g) * (1.0 + scale) + shift


def _mm_kernel(a_ref, b_ref, o_ref, acc_ref, *, nk):
    k = pl.program_id(2)
    part = _dot(a_ref[...].astype(BF16), b_ref[...].astype(BF16))

    @pl.when(k == 0)
    def _():
        acc_ref[...] = part

    @pl.when(k > 0)
    def _():
        acc_ref[...] += part

    @pl.when(k == nk - 1)
    def _():
        o_ref[...] = acc_ref[...].astype(o_ref.dtype)


def matmul(a, b, tm, tn, tk):
    M, K = a.shape
    N = b.shape[1]
    nk = K // tk
    return pl.pallas_call(
        functools.partial(_mm_kernel, nk=nk),
        grid=(M // tm, N // tn, nk),
        in_specs=[pl.BlockSpec((tm, tk), lambda i, j, k: (i, k)),
                  pl.BlockSpec((tk, tn), lambda i, j, k: (k, j))],
        out_specs=pl.BlockSpec((tm, tn), lambda i, j, k: (i, j)),
        out_shape=jax.ShapeDtypeStruct((M, N), F32),
        scratch_shapes=[pltpu.VMEM((tm, tn), F32)],
        compiler_params=_params(("parallel", "parallel", "arbitrary"), 40),
        name="matmul",
    )(a, b)


def _ada_kernel(s_ref, w_ref, b_ref, o_ref):
    o_ref[...] = _dot(s_ref[...].astype(BF16), w_ref[...].astype(BF16)) + b_ref[...]


def ada_modulation(s, ada_w, ada_b):
    depth, D, N = ada_w.shape
    R = s.shape[0]
    tn = 1024
    return pl.pallas_call(
        _ada_kernel,
        grid=(depth, N // tn),
        in_specs=[pl.BlockSpec((R, D), lambda l, j: (0, 0)),
                  pl.BlockSpec((None, D, tn), lambda l, j: (l, 0, j)),
                  pl.BlockSpec((None, 1, tn), lambda l, j: (l, 0, j))],
        out_specs=pl.BlockSpec((None, R, tn), lambda l, j: (l, 0, j)),
        out_shape=jax.ShapeDtypeStruct((depth, R, N), F32),
        compiler_params=_params(("parallel", "parallel"), 32),
        name="ada_modulation",
    )(s, ada_w, ada_b.reshape(depth, 1, N))


def _in_proj_kernel(h_ref, g_ref, sc_ref, sh_ref, w_ref, cos_ref, sin_ref, qg_ref, kg_ref, *o_refs, hy_cols, has_q, width):
    a = _norm_mod(h_ref[0], g_ref[...], sc_ref[0], sh_ref[0])
    p = _dot(a.astype(BF16), w_ref[...])
    cos = cos_ref[...]
    sin = sin_ref[...]
    outs = list(o_refs)
    col = 0
    if hy_cols:
        outs.pop(0)[0] = p[:, :hy_cols]
        col = hy_cols
    if has_q:
        q_ref = outs.pop(0)
        for hd in range(width // LANES):
            sl = slice(col + hd * LANES, col + (hd + 1) * LANES)
            q_ref[0, :, hd * LANES:(hd + 1) * LANES] = _head_norm_rope(
                p[:, sl], qg_ref[...], cos, sin, LOG2E * DIFF_HEAD_DIM ** -0.5).astype(BF16)
        col += width
    k_ref, v_ref = outs
    for hd in range(width // LANES):
        sl = slice(col + hd * LANES, col + (hd + 1) * LANES)
        k_ref[0, :, hd * LANES:(hd + 1) * LANES] = _head_norm_rope(p[:, sl], kg_ref[...], cos, sin, 1.0).astype(BF16)
    col += width
    ones = jnp.ones((p.shape[0], LANES), BF16)
    for hd in range(width // LANES):
        v_ref[0, :, 2 * hd * LANES:(2 * hd + 1) * LANES] = p[:, col + hd * LANES:col + (hd + 1) * LANES].astype(BF16)
        v_ref[0, :, (2 * hd + 1) * LANES:(2 * hd + 2) * LANES] = ones


def in_proj(h, g, scale, shift, w, hy_cols, has_q, cos, sin_signed, q_g, k_g, tm):
    B, S, D = h.shape
    N = w.shape[1]
    width = DIFF_HEADS * 2 * DIFF_HEAD_DIM
    g2 = lambda gg: jnp.concatenate([gg, gg]).reshape(1, LANES)
    mod = pl.BlockSpec((1, 1, D), lambda b, i: (b, 0, 0))
    tab = pl.BlockSpec((tm, LANES), lambda b, i: (i, 0))
    vec = pl.BlockSpec((1, LANES), lambda b, i: (0, 0))
    widths = ([hy_cols] if hy_cols else []) + ([width] if has_q else []) + [width, 2 * width]
    dtypes = ([F32] if hy_cols else []) + ([BF16] if has_q else []) + [BF16, BF16]
    return pl.pallas_call(
        functools.partial(_in_proj_kernel, hy_cols=hy_cols, has_q=has_q, width=width),
        grid=(B, S // tm),
        in_specs=[pl.BlockSpec((1, tm, D), lambda b, i: (b, i, 0)),
                  pl.BlockSpec((1, D), lambda b, i: (0, 0)), mod, mod,
                  pl.BlockSpec((D, N), lambda b, i: (0, 0)), tab, tab, vec, vec],
        out_specs=[pl.BlockSpec((1, tm, wd), lambda b, i: (b, i, 0)) for wd in widths],
        out_shape=[jax.ShapeDtypeStruct((B, S, wd), dt) for wd, dt in zip(widths, dtypes)],
        compiler_params=_params(("parallel", "parallel"), 56),
        name="in_proj",
    )(h, g.reshape(1, D), scale, shift, w, cos, sin_signed, g2(q_g), g2(k_g))


def _sconv_kernel(p_ref, w_ref, b_ref, o_ref):
    x = p_ref[0]
    S = x.shape[0]
    row = lax.broadcasted_iota(I32, x.shape, 0)
    xm = jnp.where(row == 0, 0.0, pltpu.roll(x, 1, 0))
    xp = jnp.where(row == S - 1, 0.0, pltpu.roll(x, S - 1, 0))
    w = w_ref[...]
    o_ref[0] = (xm * w[0:1] + x * w[1:2] + xp * w[2:3] + b_ref[...]).astype(o_ref.dtype)


def short_conv(p, conv_w, conv_b, col0, width, tc, out_dtype):
    B, S, _ = p.shape
    c0 = col0 // tc
    return pl.pallas_call(
        _sconv_kernel,
        grid=(B, width // tc),
        in_specs=[pl.BlockSpec((1, S, tc), lambda b, c: (b, 0, c + c0)),
                  pl.BlockSpec((3, tc), lambda b, c: (0, c + c0)),
                  pl.BlockSpec((1, tc), lambda b, c: (0, c + c0))],
        out_specs=pl.BlockSpec((1, S, tc), lambda b, c: (b, 0, c)),
        out_shape=jax.ShapeDtypeStruct((B, S, width), out_dtype),
        compiler_params=_params(("parallel", "parallel"), 48),
        name="short_conv",
    )(p, conv_w, conv_b.reshape(1, -1))


def _dft_fwd_kernel(c_ref, s_ref, u_ref, kre_ref, kim_ref, y_ref, *, nblk):
    c = c_ref[...]
    s = s_ref[...]
    C = u_ref.shape[3]
    groups = [slice(g * LANES * 2, (g + 1) * LANES * 2) for g in range(C // (2 * LANES))] if C % (2 * LANES) == 0 else [slice(0, C)]

    def transforms(ch):
        return [(_dot(c, u_ref[0, j, :, ch]), _dot(s, u_ref[0, j, :, ch])) for j in range(nblk)]

    def products(ch, spec):
        for i in range(nblk):
            yre = yim = None
            for j in range(nblk):
                d = i - j + nblk - 1
                kre = kre_ref[:, d * C + ch.start:d * C + ch.stop]
                kim = kim_ref[:, d * C + ch.start:d * C + ch.stop]
                ure, usn = spec[j]
                tre = kre * ure + kim * usn
                tim = kim * ure - kre * usn
                yre = tre if yre is None else yre + tre
                yim = tim if yim is None else yim + tim
            y_ref[0, i, 0, :, ch] = yre.astype(BF16)
            y_ref[0, i, 1, :, ch] = yim.astype(BF16)

    spec = transforms(groups[0])
    for g, ch in enumerate(groups):
        nxt = transforms(groups[g + 1]) if g + 1 < len(groups) else None
        products(ch, spec)
        spec = nxt


def dft_forward(cf, sf, u, kre, kim, order, tf):
    B, nblk, Lb, C = u.shape
    nd = 2 * nblk - 1
    return pl.pallas_call(
        functools.partial(_dft_fwd_kernel, nblk=nblk),
        grid=(Lb // tf, B),
        in_specs=[pl.BlockSpec((tf, Lb), lambda f, b: (f, 0)),
                  pl.BlockSpec((tf, Lb), lambda f, b: (f, 0)),
                  pl.BlockSpec((1, nblk, Lb, C), lambda f, b: (b, 0, 0, 0)),
                  pl.BlockSpec((tf, nd * C), lambda f, b: (f, order)),
                  pl.BlockSpec((tf, nd * C), lambda f, b: (f, order))],
        out_specs=pl.BlockSpec((1, nblk, 2, tf, C), lambda f, b: (b, 0, 0, f, 0)),
        out_shape=jax.ShapeDtypeStruct((B, nblk, 2, Lb, C), BF16),
        compiler_params=_params(("parallel", "parallel"), 56),
        name="dft_forward",
    )(cf, sf, u, kre, kim)


def _dft_inv_kernel(ct_ref, st_ref, y_ref, gate_ref, o_ref, *, scale):
    acc = _dot(ct_ref[...], y_ref[0, 0, 0]) - _dot(st_ref[...], y_ref[0, 0, 1])
    o_ref[0, 0] = (gate_ref[0, 0] * (acc * scale)).astype(BF16)


def dft_inverse(ct, st, y, gates, gate_col):
    B, nblk, _, Lb, C = y.shape
    return pl.pallas_call(
        functools.partial(_dft_inv_kernel, scale=1.0 / Lb),
        grid=(B, nblk),
        in_specs=[pl.BlockSpec((Lb, Lb), lambda b, i: (0, 0)),
                  pl.BlockSpec((Lb, Lb), lambda b, i: (0, 0)),
                  pl.BlockSpec((1, 1, 2, Lb, C), lambda b, i: (b, i, 0, 0, 0)),
                  pl.BlockSpec((1, 1, Lb, C), lambda b, i: (b, i, 0, gate_col))],
        out_specs=pl.BlockSpec((1, 1, Lb, C), lambda b, i: (b, i, 0, 0)),
        out_shape=jax.ShapeDtypeStruct((B, nblk, Lb, C), BF16),
        compiler_params=_params(("parallel", "parallel"), 48),
        name="dft_inverse",
    )(ct, st, y, gates)


def dft_tables(L):
    n = 2 * L
    f = lax.broadcasted_iota(I32, (L, L), 0)
    t = lax.broadcasted_iota(I32, (L, L), 1)
    m = ((2 * f + 1) * t) % (2 * n)
    ang = m.astype(F32) * (math.pi / n)
    cf = jnp.cos(ang)
    sf = jnp.sin(ang)
    return cf.astype(BF16), sf.astype(BF16), cf.T.astype(BF16), sf.T.astype(BF16)


def hyena_filter_taps(L, w1, b1, w2, b2, w3, b3, freq, wout, width):
    hp = lax.Precision.HIGHEST
    t = jnp.linspace(0.0, 1.0, L, dtype=F32)[:, None]
    bands = (FILTER_EMB - 1) // 2
    w = 2.0 * math.pi * jnp.arange(L, dtype=F32)[:, None] / L
    f = jnp.linspace(1e-4, bands - 1, bands, dtype=F32)[None, :]
    z = jnp.concatenate([t, jnp.cos(f * w), -jnp.sin(f * w)], axis=-1)
    h = jnp.sin(freq * (jnp.dot(z, w1, precision=hp) + b1))
    h = jnp.sin(freq * (jnp.dot(h, w2, precision=hp) + b2))
    h = jnp.sin(freq * (jnp.dot(h, w3, precision=hp) + b3))
    max_decay = math.log(FILTER_TARGET) / FILTER_FAST_DECAY
    min_decay = math.log(FILTER_TARGET) / FILTER_SLOW_DECAY
    deltas = jnp.abs(jnp.linspace(min_decay, max_decay, width, dtype=F32))

    def taps(hh, tt):
        return jnp.dot(hh, wout, precision=hp).reshape(L, HYENA_ORDER, 2, width) * jnp.exp(-tt * deltas[None, :])[:, None, None, :]

    return taps(h, t), taps(h[::-1], t[::-1])


def filter_spectra(cf, sf, taps, taps_rev, bias, nblk):
    L, C = taps.shape[0], taps.shape[3]
    Lb = L // nblk
    zero = jnp.zeros((1, C), F32)
    sums, diffs = [], []
    for o in range(HYENA_ORDER):
        h_fwd, h_bwd = taps[:, o, 0], taps[:, o, 1]
        h_fwd_rev, h_bwd_rev = taps_rev[:, o, 0], taps_rev[:, o, 1]
        k0 = (h_fwd[0] + h_bwd[0] + bias[o])[None, :]
        kfull = jnp.concatenate([zero, h_bwd_rev[:L - 1], k0, h_fwd[1:]], axis=0)
        krev = jnp.concatenate([h_fwd_rev[:L - 1], k0, h_bwd[1:], zero], axis=0)
        for dlt in range(-(nblk - 1), nblk):
            base = L + dlt * Lb
            kp = kfull[base:base + Lb]
            km = jnp.concatenate([zero, krev[2 * L - base:2 * L - base + Lb - 1]], axis=0)
            sums.append(kp + km)
            diffs.append(km - kp)
    tmm = min(Lb, 512)
    kre = matmul(cf, jnp.concatenate(sums, axis=1), tmm, tmm, tmm)
    kim = matmul(sf, jnp.concatenate(diffs, axis=1), tmm, tmm, tmm)
    return kre, kim


def hyena_operator(gates, v, tables, taps, taps_rev, bias, nblk, tf):
    cf, sf, ct, st = tables
    B, L, C = v.shape
    Lb = L // nblk
    kre, kim = filter_spectra(cf, sf, taps, taps_rev, bias, nblk)
    gates = gates.reshape(B, nblk, Lb, 2 * C)
    z = v.reshape(B, nblk, Lb, C)
    for o in range(HYENA_ORDER):
        y = dft_forward(cf, sf, z, kre, kim, o, tf)
        z = dft_inverse(ct, st, y, gates, o)
    return z.reshape(B, L, C)


def _head_norm_rope(x, g, cos, sin_signed, scale):
    lane = lax.broadcasted_iota(I32, x.shape, 1)
    lo = lane < DIFF_HEAD_DIM
    x2 = x * x
    s_lo = jnp.sum(jnp.where(lo, x2, 0.0), axis=-1, keepdims=True)
    s_hi = jnp.sum(jnp.where(lo, 0.0, x2), axis=-1, keepdims=True)
    ms = jnp.where(lo, s_lo, s_hi) * (1.0 / DIFF_HEAD_DIM)
    xn = x * lax.rsqrt(ms + NORM_EPS) * g
    first = (lane & 16) == 0
    partner = jnp.where(first, pltpu.roll(xn, LANES - 16, 1), pltpu.roll(xn, 16, 1))
    return (xn * cos + partner * sin_signed) * scale


def rope_tables(S, use_rope):
    if not use_rope:
        return jnp.ones((S, LANES), F32), jnp.zeros((S, LANES), F32)
    t = jnp.arange(S, dtype=I32)
    row = (t // GRID_W).astype(F32)[:, None]
    colp = (t % GRID_W).astype(F32)[:, None]
    nf = DIFF_HEAD_DIM // 4
    inv = ROPE_BASE ** (-jnp.arange(nf, dtype=F32) / nf)
    lane = jnp.arange(LANES)
    grp = (lane % DIFF_HEAD_DIM) // nf
    j = lane % nf
    pos = jnp.where((grp < 2)[None, :], row, colp)
    ang = pos * inv[j][None, :]
    sign = jnp.where((grp % 2 == 0)[None, :], -1.0, 1.0)
    return jnp.cos(ang), jnp.sin(ang) * sign


def _attn_kernel(lam_ref, q_ref, kc_ref, vc_ref, *rest, has_lat, out_scale, ck):
    if has_lat:
        kl_ref, vl_ref, g_ref, o_ref = rest
    else:
        g_ref, o_ref = rest
    lam = lam_ref[0]
    q = q_ref[0]
    lane = lax.broadcasted_iota(I32, q.shape, 1)
    nt = (((1,), (1,)), ((), ()))
    zero = jnp.zeros_like(q)
    qm = [jnp.where(lane < DIFF_HEAD_DIM, q, zero), jnp.where(lane >= DIFF_HEAD_DIM, q, zero)]
    chunks = [(kc_ref, vc_ref, 0, kc_ref.shape[1])]
    if has_lat:
        chunks += [(kl_ref, vl_ref, c * ck, ck) for c in range(kl_ref.shape[1] // ck)]

    def scores(ch):
        k = ch[0][0, ch[2]:ch[2] + ch[3], :]
        return [lax.dot_general(qm[mp], k, nt, preferred_element_type=F32) for mp in range(2)]

    m, acc = [None, None], [None, None]
    s_next = scores(chunks[0])
    for ci, ch in enumerate(chunks):
        s_cur = s_next
        if ci + 1 < len(chunks):
            s_next = scores(chunks[ci + 1])
        v1 = ch[1][0, ch[2]:ch[2] + ch[3], :]
        for mp in range(2):
            s = s_cur[mp]
            mx = jnp.max(s, axis=-1, keepdims=True)
            m_new = mx if ci == 0 else jnp.maximum(m[mp], mx)
            pv = _dot(jnp.exp2(s - m_new).astype(BF16), v1)
            acc[mp] = pv if ci == 0 else jnp.exp2(m[mp] - m_new) * acc[mp] + pv
            m[mp] = m_new
    o = acc[0][:, :LANES] / acc[0][:, LANES:] - lam * (acc[1][:, :LANES] / acc[1][:, LANES:])
    ms = jnp.mean(o * o, axis=-1, keepdims=True)
    o_ref[0] = ((o * lax.rsqrt(ms + NORM_EPS) * g_ref[...]) * out_scale).astype(BF16)


def diff_attention(lam, q, k_ctx, v_ctx, k_lat, v_lat, subln_g, out_scale, tq):
    B, Sq, W = q.shape
    H = W // LANES
    has_lat = k_lat is not None
    Sc = k_ctx.shape[1]
    head = lambda S, w: pl.BlockSpec((1, S, w), lambda b, h, i: (b, 0, h))
    in_specs = [pl.BlockSpec(memory_space=pltpu.SMEM),
                pl.BlockSpec((1, tq, LANES), lambda b, h, i: (b, i, h)), head(Sc, LANES), head(Sc, 2 * LANES)]
    args = [lam.reshape(1), q, k_ctx, v_ctx]
    if has_lat:
        in_specs += [head(k_lat.shape[1], LANES), head(k_lat.shape[1], 2 * LANES)]
        args += [k_lat, v_lat]
    in_specs.append(pl.BlockSpec((1, LANES), lambda b, h, i: (0, 0)))
    args.append(subln_g.reshape(1, LANES))
    return pl.pallas_call(
        functools.partial(_attn_kernel, has_lat=has_lat, out_scale=out_scale,
                          ck=min(ATTN_KEY_CHUNK, k_lat.shape[1]) if has_lat else 0),
        grid=(B, H, Sq // tq),
        in_specs=in_specs,
        out_specs=pl.BlockSpec((1, tq, LANES), lambda b, h, i: (b, i, h)),
        out_shape=jax.ShapeDtypeStruct((B, Sq, W), BF16),
        compiler_params=_params(("parallel", "parallel", "parallel"), 56),
        name="diff_attention",
    )(*args)


def _oproj_kernel(hy_ref, o_ref, w_ref, h_ref, gate_ref, *rest, half, n_exp):
    prep_in, (out_ref, *prep_out) = rest[:4], rest[4:]
    y = _dot(hy_ref[0].astype(BF16), w_ref[0:half, :]) + _dot(o_ref[0], w_ref[half:, :])
    h_new = h_ref[0] + gate_ref[0] * y
    out_ref[0] = h_new
    _ffn_prep_store(h_new, *prep_in, *prep_out, n_exp)


def out_proj(hy, o, w_out, h, gate, prep, tm):
    B, S, D = h.shape
    half = hy.shape[2]
    n_exp = prep[3].shape[1]
    p_in, p_out, p_shape = _ffn_prep_specs(B, S, D, tm, n_exp)
    return pl.pallas_call(
        functools.partial(_oproj_kernel, half=half, n_exp=n_exp),
        grid=(B, S // tm),
        in_specs=[pl.BlockSpec((1, tm, half), lambda b, i: (b, i, 0)),
                  pl.BlockSpec((1, tm, half), lambda b, i: (b, i, 0)),
                  pl.BlockSpec((2 * half, D), lambda b, i: (0, 0)),
                  pl.BlockSpec((1, tm, D), lambda b, i: (b, i, 0)),
                  pl.BlockSpec((1, 1, D), lambda b, i: (b, 0, 0))] + p_in,
        out_specs=[pl.BlockSpec((1, tm, D), lambda b, i: (b, i, 0))] + p_out,
        out_shape=[jax.ShapeDtypeStruct((B, S, D), F32)] + p_shape,
        compiler_params=_params(("parallel", "parallel"), 40),
        name="out_proj",
    )(hy, o, w_out, h, gate, *_ffn_prep_args(prep))


def _shift_rows(x, d):
    return pltpu.roll(x, (-d) % x.shape[0], 0)


def _pool_kernel(hp_ref, hc_ref, hn_ref, g_ref, sc_ref, sh_ref, gate_ref, pw_ref, ps_ref, *rest, T, L, n_exp):
    prep_in, (o_ref, *prep_out) = rest[:4], rest[4:]
    i = pl.program_id(1)
    nt = pl.num_programs(1)
    g, sc, sh = g_ref[...], sc_ref[0], sh_ref[0]
    hc = hc_ref[0]
    a_c = _norm_mod(hc, g, sc, sh)
    a_p = jnp.where(i == 0, 0.0, _norm_mod(hp_ref[0], g, sc, sh))
    a_n = jnp.where(i == nt - 1, 0.0, _norm_mod(hn_ref[0], g, sc, sh))
    ext = jnp.concatenate([a_p, a_c, a_n], axis=0)
    tok = i * T + lax.broadcasted_iota(I32, (T, 1), 0)
    G = ext.shape[1] // len(POOL_WINDOWS)
    ys = []
    for gi, w in enumerate(POOL_WINDOWS):
        xg = ext[:, gi * G:(gi + 1) * G]
        s = _shift_rows(xg, -1) + xg
        step = 1
        while 2 * step < w:
            s = _shift_rows(s, -step) + _shift_rows(s, step)
            step *= 2
        cnt = (jnp.minimum(tok + w // 2, L) - jnp.maximum(tok - w // 2, 0)).astype(F32)
        p = s[POOL_HALO:POOL_HALO + T] / cnt - a_c[:, gi * G:(gi + 1) * G]
        ys.append(_dot(p.astype(BF16), pw_ref[gi]))
    y = jnp.concatenate(ys, axis=1) * ps_ref[...]
    h_new = hc + gate_ref[0] * y
    o_ref[0] = h_new
    _ffn_prep_store(h_new, *prep_in, *prep_out, n_exp)


def pool_mixer(h, g, scale, shift, gate, pool_w, pool_scale, prep, T):
    B, S, D = h.shape
    nh = T // POOL_HALO
    last = S // POOL_HALO - 1
    n_exp = prep[3].shape[1]
    p_in, p_out, p_shape = _ffn_prep_specs(B, S, D, T, n_exp)
    mod = pl.BlockSpec((1, 1, D), lambda b, i: (b, 0, 0))
    return pl.pallas_call(
        functools.partial(_pool_kernel, T=T, L=S, n_exp=n_exp),
        grid=(B, S // T),
        in_specs=[pl.BlockSpec((1, POOL_HALO, D), lambda b, i: (b, jnp.maximum(i * nh - 1, 0), 0)),
                  pl.BlockSpec((1, T, D), lambda b, i: (b, i, 0)),
                  pl.BlockSpec((1, POOL_HALO, D), lambda b, i: (b, jnp.minimum((i + 1) * nh, last), 0)),
                  pl.BlockSpec((1, D), lambda b, i: (0, 0)), mod, mod, mod,
                  pl.BlockSpec(pool_w.shape, lambda b, i: (0, 0, 0)),
                  pl.BlockSpec((1, D), lambda b, i: (0, 0))] + p_in,
        out_specs=[pl.BlockSpec((1, T, D), lambda b, i: (b, i, 0))] + p_out,
        out_shape=[jax.ShapeDtypeStruct((B, S, D), F32)] + p_shape,
        compiler_params=_params(("parallel", "parallel"), 48),
        name="pool_mixer",
    )(h, h, h, g.reshape(1, D), scale, shift, gate, pool_w, pool_scale.reshape(1, D), *_ffn_prep_args(prep))


def _ffn_prep_store(h_new, g_ref, sc_ref, sh_ref, rw_ref, x_ref, aff_ref, afft_ref, n_exp):
    a = _norm_mod(h_new, g_ref[...], sc_ref[0], sh_ref[0]).astype(BF16)
    x_ref[0] = a
    logits = _dot(a, rw_ref[...])
    lane = lax.broadcasted_iota(I32, logits.shape, 1)
    valid = lane < n_exp
    mx = jnp.max(jnp.where(valid, logits, -jnp.inf), axis=-1, keepdims=True)
    e = jnp.where(valid, jnp.exp(logits - mx), 0.0)
    aff = e / jnp.sum(e, axis=-1, keepdims=True)
    aff_ref[0] = aff
    afft_ref[0] = aff.T[0:n_exp]


def _ffn_prep_specs(B, S, D, tm, n_exp):
    mod = pl.BlockSpec((1, 1, D), lambda b, i: (b, 0, 0))
    ins = [pl.BlockSpec((1, D), lambda b, i: (0, 0)), mod, mod, pl.BlockSpec((D, LANES), lambda b, i: (0, 0))]
    outs = [pl.BlockSpec((1, tm, D), lambda b, i: (b, i, 0)),
            pl.BlockSpec((1, tm, LANES), lambda b, i: (b, i, 0)),
            pl.BlockSpec((1, n_exp, tm), lambda b, i: (b, 0, i))]
    shapes = [jax.ShapeDtypeStruct((B, S, D), BF16), jax.ShapeDtypeStruct((B, S, LANES), F32),
              jax.ShapeDtypeStruct((B, n_exp, S), F32)]
    return ins, outs, shapes


def _ffn_prep_args(prep):
    g, scale, shift, router_w = prep
    rw = jnp.pad(router_w, ((0, 0), (0, LANES - router_w.shape[1]))).astype(BF16)
    return [g.reshape(1, -1), scale, shift, rw]


def _prefix_excl(m, tri, tb):
    S = m.shape[0]
    carry = jnp.zeros((1, m.shape[1]), F32)
    outs, carries = [], []
    for blk in range(S // tb):
        mb = m[blk * tb:(blk + 1) * tb]
        outs.append(_dot(tri, mb.astype(BF16)) + carry)
        carries.append(carry)
        carry = carry + jnp.sum(mb, axis=0, keepdims=True)
    carries.append(carry)
    return (jnp.concatenate(outs, axis=0) if len(outs) > 1 else outs[0]), jnp.concatenate(carries, axis=0)


def _select_kernel(aff_ref, afft_ref, tri_ref, pos_ref, post_ref, offs_ref, *, cap, tb, n_exp):
    bits = pltpu.bitcast(afft_ref[0], I32)

    def body(i, cur):
        cand = cur | (jnp.int32(1) << (30 - i))
        cnt = jnp.sum((bits >= cand).astype(F32), axis=1, keepdims=True)
        return jnp.where(cnt >= cap, cand, cur)

    thr_bits = lax.fori_loop(0, 31, body, jnp.zeros((n_exp, 1), I32))
    thr_col = pltpu.bitcast(jnp.broadcast_to(thr_bits, (n_exp, LANES)), F32)
    sub = lax.broadcasted_iota(I32, (n_exp, LANES), 0)
    lane = lax.broadcasted_iota(I32, (n_exp, LANES), 1)
    thr = jnp.sum(jnp.where(sub == lane, thr_col, 0.0), axis=0, keepdims=True)
    aff = aff_ref[0]
    tri = tri_ref[...]
    gt = (aff > thr).astype(F32)
    eq = (aff == thr).astype(F32)
    need = cap - jnp.sum(gt, axis=0, keepdims=True)
    sel = gt + eq * (_prefix_excl(eq, tri, tb)[0] < need).astype(F32)
    slot, offs = _prefix_excl(sel, tri, tb)
    sp = jnp.where(sel > 0.0, slot, -1.0)
    pos_ref[0] = sp
    offs_ref[0] = offs
    for blk in range(sp.shape[0] // tb):
        post_ref[0, :, blk * tb:(blk + 1) * tb] = sp[blk * tb:(blk + 1) * tb].T[0:n_exp]


def select_tokens(aff, afft, cap):
    B, S, _ = aff.shape
    n_exp = afft.shape[1]
    tb = min(S, 256)
    r = lax.broadcasted_iota(I32, (tb, tb), 0)
    c = lax.broadcasted_iota(I32, (tb, tb), 1)
    tri = (c < r).astype(BF16)
    nblk = S // tb
    pos, post, offs = pl.pallas_call(
        functools.partial(_select_kernel, cap=cap, tb=tb, n_exp=n_exp),
        grid=(B,),
        in_specs=[pl.BlockSpec((1, S, LANES), lambda b: (b, 0, 0)),
                  pl.BlockSpec((1, n_exp, S), lambda b: (b, 0, 0)),
                  pl.BlockSpec((tb, tb), lambda b: (0, 0))],
        out_specs=[pl.BlockSpec((1, S, LANES), lambda b: (b, 0, 0)),
                   pl.BlockSpec((1, n_exp, S), lambda b: (b, 0, 0)),
                   pl.BlockSpec((1, nblk + 1, LANES), lambda b: (b, 0, 0))],
        out_shape=[jax.ShapeDtypeStruct((B, S, LANES), F32), jax.ShapeDtypeStruct((B, n_exp, S), F32),
                   jax.ShapeDtypeStruct((B, nblk + 1, LANES), F32)],
        compiler_params=_params(("parallel",), 48),
        name="select_tokens",
    )(aff, afft, tri)
    offs = jnp.swapaxes(offs[:, :, :n_exp], 1, 2).astype(I32).reshape(-1)
    return pos, post, offs, tb


def _gather_kernel(offs_ref, x_ref, post_ref, afft_ref, o_ref, g_ref, *, nblk, n_exp, win):
    b = pl.program_id(0)
    t = pl.program_id(1)
    slots = o_ref.shape[2]

    @pl.when(t == 0)
    def _():
        o_ref[...] = jnp.zeros_like(o_ref)
        g_ref[...] = jnp.zeros_like(g_ref)

    x = x_ref[0]
    T = x.shape[0]

    def full():
        r = lax.broadcasted_iota(I32, (slots, T), 0).astype(F32)
        for e in range(n_exp):
            hit = r == post_ref[0, e:e + 1, :]
            o_ref[e, 0] = (o_ref[e, 0].astype(F32) + _dot(hit.astype(BF16), x)).astype(BF16)
            g = jnp.sum(jnp.where(hit, afft_ref[0, e:e + 1, :], 0.0), axis=1, keepdims=True)
            g_ref[e, 0] = g_ref[e, 0] + jnp.broadcast_to(g, (slots, LANES))

    if win >= slots:
        full()
        return
    starts = []
    narrow = None
    for e in range(n_exp):
        base = (b * n_exp + e) * (nblk + 1) + t
        lo = offs_ref[base]
        hi = offs_ref[base + 1]
        a0 = pl.multiple_of(jnp.minimum((lo // BF16_ROWS) * BF16_ROWS, slots - win), BF16_ROWS)
        ok = hi - a0 <= win
        narrow = ok if narrow is None else jnp.logical_and(narrow, ok)
        starts.append(a0)

    @pl.when(narrow)
    def _():
        r = lax.broadcasted_iota(I32, (win, T), 0).astype(F32)
        hits = [r + starts[e].astype(F32) == post_ref[0, e:e + 1, :] for e in range(n_exp)]
        rows = _dot(jnp.concatenate([h.astype(BF16) for h in hits], axis=0), x)
        old = [o_ref[e, 0, pl.ds(starts[e], win), :] for e in range(n_exp)]
        old_g = [g_ref[e, 0, pl.ds(starts[e], win), :] for e in range(n_exp)]
        for e in range(n_exp):
            o_ref[e, 0, pl.ds(starts[e], win), :] = (old[e].astype(F32) + rows[e * win:(e + 1) * win]).astype(BF16)
            g = jnp.sum(jnp.where(hits[e], afft_ref[0, e:e + 1, :], 0.0), axis=1, keepdims=True)
            g_ref[e, 0, pl.ds(starts[e], win), :] = old_g[e] + jnp.broadcast_to(g, (win, LANES))

    pl.when(jnp.logical_not(narrow))(full)


def gather_tokens(x, post, afft, offs, slots, T):
    B, S, D = x.shape
    E = post.shape[1]
    row = pl.BlockSpec((1, E, T), lambda b, t, offs: (b, 0, t))
    return pl.pallas_call(
        functools.partial(_gather_kernel, nblk=S // T, n_exp=E, win=min(slots, GATHER_WINDOW)),
        grid_spec=pltpu.PrefetchScalarGridSpec(
            num_scalar_prefetch=1,
            grid=(B, S // T),
            in_specs=[pl.BlockSpec((1, T, D), lambda b, t, offs: (b, t, 0)), row, row],
            out_specs=[pl.BlockSpec((E, 1, slots, D), lambda b, t, offs: (0, b, 0, 0)),
                       pl.BlockSpec((E, 1, slots, LANES), lambda b, t, offs: (0, b, 0, 0))]),
        out_shape=[jax.ShapeDtypeStruct((E, B, slots, D), BF16), jax.ShapeDtypeStruct((E, B, slots, LANES), F32)],
        compiler_params=_params(("parallel", "arbitrary"), 56),
        name="gather_tokens",
    )(offs, x, post, afft)


def _ffn_kernel(xs_ref, g_ref, wg_ref, wu_ref, wd_ref, o_ref, hm_ref, *, nf):
    f = pl.program_id(2)
    x = xs_ref[0]
    a = _dot(x, wg_ref[...].astype(BF16))
    u = _dot(x, wu_ref[...].astype(BF16))
    hm_ref[f] = (a * jax.nn.sigmoid(a) * u).astype(BF16)

    @pl.when(f == nf - 1)
    def _():
        hm = jnp.concatenate([hm_ref[c] for c in range(nf)], axis=1) if nf > 1 else hm_ref[0]
        o_ref[0] = (_dot(hm, wd_ref[...].astype(BF16)) * g_ref[0, :, 0:1]).astype(BF16)


def expert_ffn(xs, gs, w_gate, w_up, w_down, layer, tm, tf):
    E, M, D = xs.shape
    F = w_gate.shape[3]
    nf = F // tf
    return pl.pallas_call(
        functools.partial(_ffn_kernel, nf=nf),
        grid=(E, M // tm, nf),
        in_specs=[pl.BlockSpec((1, tm, D), lambda e, m, f: (e, m, 0)),
                  pl.BlockSpec((1, tm, LANES), lambda e, m, f: (e, m, 0)),
                  pl.BlockSpec((None, None, D, tf), lambda e, m, f: (layer, e, 0, f)),
                  pl.BlockSpec((None, None, D, tf), lambda e, m, f: (layer, e, 0, f)),
                  pl.BlockSpec((None, None, F, D), lambda e, m, f: (layer, e, 0, 0))],
        out_specs=pl.BlockSpec((1, tm, D), lambda e, m, f: (e, m, 0)),
        out_shape=jax.ShapeDtypeStruct((E, M, D), BF16),
        scratch_shapes=[pltpu.VMEM((nf, tm, tf), BF16)],
        compiler_params=_params(("parallel", "parallel", "arbitrary"), 56),
        name="expert_ffn",
    )(xs, gs, w_gate, w_up, w_down)


def _combine_kernel(offs_ref, y_ref, pos_ref, h_ref, gate_ref, o_ref, ybuf_ref, *, nblk, n_exp, win):
    b = pl.program_id(0)
    t = pl.program_id(1)
    T = pos_ref.shape[1]
    slots = y_ref.shape[2]

    def full():
        if slots < LANES:
            K = n_exp * slots
            col = lax.broadcasted_iota(I32, (LANES, K), 1).astype(F32)
            owner = jnp.floor((col + 0.5) * (1.0 / slots))
            rep = owner == lax.broadcasted_iota(I32, (LANES, K), 0).astype(F32)
            slot_rep = _dot(pos_ref[0].astype(BF16), rep.astype(BF16))
            want = (col - owner * slots)[0:1]
            acc = _dot((slot_rep == want).astype(BF16), y_ref[:, 0].reshape(K, y_ref.shape[3]))
        else:
            r = lax.broadcasted_iota(I32, (T, slots), 1).astype(F32)
            acc = None
            for e in range(n_exp):
                part = _dot((pos_ref[0, :, e:e + 1] == r).astype(BF16), y_ref[e, 0])
                acc = part if acc is None else acc + part
        o_ref[0] = h_ref[0] + gate_ref[0] * acc

    if win >= slots:
        full()
        return
    starts = []
    narrow = None
    for e in range(n_exp):
        base = (b * n_exp + e) * (nblk + 1) + t
        lo = offs_ref[base]
        hi = offs_ref[base + 1]
        a0 = jnp.minimum((lo // BF16_ROWS) * BF16_ROWS, slots - win)
        ok = hi - a0 <= win
        narrow = ok if narrow is None else jnp.logical_and(narrow, ok)
        starts.append(a0)

    @pl.when(narrow)
    def _():
        r = lax.broadcasted_iota(I32, (T, win), 1).astype(F32)
        pieces = []
        for e in range(n_exp):
            a0 = starts[e]
            ybuf_ref[e * win:(e + 1) * win, :] = y_ref[e, 0, pl.ds(pl.multiple_of(a0, BF16_ROWS), win), :]
            pieces.append((pos_ref[0, :, e:e + 1] - a0.astype(F32) == r).astype(BF16))
        o_ref[0] = h_ref[0] + gate_ref[0] * _dot(jnp.concatenate(pieces, axis=1), ybuf_ref[...])

    pl.when(jnp.logical_not(narrow))(full)


def combine_tokens(y, pos, offs, h, gate, T):
    E, B, slots, D = y.shape
    S = h.shape[1]
    win = min(slots, COMBINE_WINDOW)
    return pl.pallas_call(
        functools.partial(_combine_kernel, nblk=S // T, n_exp=E, win=win),
        grid_spec=pltpu.PrefetchScalarGridSpec(
            num_scalar_prefetch=1,
            grid=(B, S // T),
            in_specs=[pl.BlockSpec((E, 1, slots, D), lambda b, t, offs: (0, b, 0, 0)),
                      pl.BlockSpec((1, T, LANES), lambda b, t, offs: (b, t, 0)),
                      pl.BlockSpec((1, T, D), lambda b, t, offs: (b, t, 0)),
                      pl.BlockSpec((1, 1, D), lambda b, t, offs: (b, 0, 0))],
            out_specs=pl.BlockSpec((1, T, D), lambda b, t, offs: (b, t, 0)),
            scratch_shapes=[pltpu.VMEM((E * win, D), BF16)]),
        out_shape=jax.ShapeDtypeStruct((B, S, D), F32),
        compiler_params=_params(("parallel", "parallel"), 56),
        name="combine_tokens",
    )(offs, y, pos, h, gate)


def moe_block(mixed, gate, w_gate, w_up, w_down, layer):
    h, x, aff, afft = mixed
    B, S, D = h.shape
    E = afft.shape[1]
    cap = EC_CAPACITY * S // E
    slots = cap
    pos, post, offs, tb = select_tokens(aff, afft, cap)
    xs, gs = gather_tokens(x, post, afft, offs, slots, tb)
    M = B * slots
    y = expert_ffn(xs.reshape(E, M, D), gs.reshape(E, M, LANES), w_gate, w_up, w_down, layer, min(M, 1024), 512)
    return combine_tokens(y.reshape(E, B, slots, D), pos, offs, h, gate, tb)


def kernel(x, c, ctx, c_ctx, ada_w, ada_b, norm_mix_g, norm_ffn_g, w_in, w_out, hy_conv_w, hy_conv_b, hy_f_w1, hy_f_b1, hy_f_w2, hy_f_b2, hy_f_w3, hy_f_b3, hy_f_freq, hy_f_wout, hy_bias, q_norm_g, k_norm_g, diff_lambda, subln_g, pool_w, pool_scale, router_w, exp_w_gate, exp_w_up, exp_w_down):
    B, S, D = x.shape
    Lc = ctx.shape[1]
    depth = ada_w.shape[0]
    hy_width = hy_bias.shape[2]
    hy_proj = (HYENA_ORDER + 1) * hy_width
    qk_width = DIFF_HEADS * 2 * DIFF_HEAD_DIM
    last_attn = ((depth - 1) // 2) * 2

    s_all = jnp.concatenate([jax.nn.silu(c), jax.nn.silu(c_ctx)[None, :]], axis=0)
    rows = -(-s_all.shape[0] // SUBLANES) * SUBLANES
    mods = ada_modulation(jnp.pad(s_all, ((0, rows - s_all.shape[0]), (0, 0))), ada_w, ada_b)

    nblk_lat = max(1, S // HYENA_BLOCK)
    tables_lat = dft_tables(S // nblk_lat)
    tables_ctx = dft_tables(Lc)
    rope_lat = rope_tables(S, True)
    rope_ctx = rope_tables(Lc, False)

    h, hc = x, ctx
    for l in range(depth):
        m = [mods[l, :B, i * D:(i + 1) * D].reshape(B, 1, D) for i in range(6)]
        mc = [jnp.broadcast_to(mods[l, B, i * D:(i + 1) * D].reshape(1, 1, D), (B, 1, D)) for i in range(6)]
        ctx_full = l < last_attn
        prep = (norm_ffn_g[l], m[4], m[3], router_w[l])
        prep_c = (norm_ffn_g[l], mc[4], mc[3], router_w[l])
        if l % 2 == 0:
            e = l // 2
            lam_init = 0.8 - 0.6 * math.exp(-0.3 * l)
            lv = diff_lambda[e]
            lam = jnp.exp(jnp.sum(lv[0] * lv[1])) - jnp.exp(jnp.sum(lv[2] * lv[3])) + lam_init
            filt = (hy_f_w1[e], hy_f_b1[e], hy_f_w2[e], hy_f_b2[e], hy_f_w3[e], hy_f_b3[e], hy_f_freq[e], hy_f_wout[e])
            w_in_b = w_in[e].astype(BF16)
            w_out_b = w_out[e].astype(BF16)

            p, q, k, v = in_proj(h, norm_mix_g[l], m[1], m[0], w_in_b, hy_proj, True, *rope_lat,
                                 q_norm_g[e], k_norm_g[e], 512)
            gts = short_conv(p, hy_conv_w[e], hy_conv_b[e], 0, 2 * hy_width, 256, F32)
            hv = short_conv(p, hy_conv_w[e], hy_conv_b[e], 2 * hy_width, hy_width, 256, BF16)
            hy = hyena_operator(gts, hv, tables_lat, *hyena_filter_taps(S, *filt, hy_width), hy_bias[e], nblk_lat, 256)

            if l <= last_attn:
                if ctx_full:
                    pc, qc, kc, vc = in_proj(hc, norm_mix_g[l], mc[1], mc[0], w_in_b, hy_proj, True, *rope_ctx,
                                             q_norm_g[e], k_norm_g[e], Lc)
                else:
                    kc, vc = in_proj(hc, norm_mix_g[l], mc[1], mc[0], w_in_b[:, hy_proj + qk_width:], 0, False,
                                     *rope_ctx, q_norm_g[e], k_norm_g[e], Lc)
            o = diff_attention(lam, q, kc, vc, k, v, subln_g[e], 1.0 - lam_init, min(S, 512))
            mixed = out_proj(hy, o, w_out_b, h, m[2], prep, 512)
            if ctx_full:
                gtc = short_conv(pc, hy_conv_w[e], hy_conv_b[e], 0, 2 * hy_width, 256, F32)
                hvc = short_conv(pc, hy_conv_w[e], hy_conv_b[e], 2 * hy_width, hy_width, 256, BF16)
                hyc = hyena_operator(gtc, hvc, tables_ctx, *hyena_filter_taps(Lc, *filt, hy_width), hy_bias[e], 1, Lc)
                oc = diff_attention(lam, qc, kc, vc, None, None, subln_g[e], 1.0 - lam_init, Lc)
                mixed_c = out_proj(hyc, oc, w_out_b, hc, mc[2], prep_c, Lc)
        else:
            o_idx = l // 2
            pw = pool_w[o_idx].astype(BF16)
            mixed = pool_mixer(h, norm_mix_g[l], m[1], m[0], m[2], pw, pool_scale[o_idx], prep, 512)
            if ctx_full:
                mixed_c = pool_mixer(hc, norm_mix_g[l], mc[1], mc[0], mc[2], pw, pool_scale[o_idx], prep_c, Lc)
        h = moe_block(mixed, m[5], exp_w_gate, exp_w_up, exp_w_down, l)
        if ctx_full:
            hc = moe_block(mixed_c, mc[5], exp_w_gate, exp_w_up, exp_w_down, l)
    return h
```

```python
import functools
import math

import jax
import jax.numpy as jnp
from jax import lax
from jax.experimental import pallas as pl
from jax.experimental.pallas import tpu as pltpu

F32 = jnp.float32
BF16 = jnp.bfloat16
I32 = jnp.int32

NORM_EPS = 1e-6
GRID_W = 64
HYENA_ORDER = 2
FILTER_EMB = 33
FILTER_FAST_DECAY = 0.3
FILTER_SLOW_DECAY = 1.5
FILTER_TARGET = 1e-2
DIFF_HEADS = 4
DIFF_HEAD_DIM = 64
ROPE_BASE = 10000.0
POOL_WINDOWS = (2, 4, 8, 16)
EC_CAPACITY = 2
LANES = 128
SUBLANES = 8
POOL_HALO = 8
LOG2E = 1.4426950408889634
ATTN_KEY_CHUNK = 2048
GATHER_WINDOW = 80
COMBINE_WINDOW = 128
BF16_ROWS = 16
HYENA_BLOCK = 1024


def _params(sem, vmem_mb):
    return pltpu.CompilerParams(dimension_semantics=sem, vmem_limit_bytes=vmem_mb * 1024 * 1024)


def _dot(a, b):
    return jnp.dot(a, b, preferred_element_type=F32)


def _norm_mod(x, g, scale, shift):
    ms = jnp.mean(x * x, axis=-1, keepdims=True)
    return (x * lax.rsqrt(ms + NORM_EPS) * g) * (1.0 + scale) + shift


def _mm_kernel(a_ref, b_ref, o_ref, acc_ref, *, nk):
    k = pl.program_id(2)
    part = _dot(a_ref[...].astype(BF16), b_ref[...].astype(BF16))

    @pl.when(k == 0)
    def _():
        acc_ref[...] = part

    @pl.when(k > 0)
    def _():
        acc_ref[...] += part

    @pl.when(k == nk - 1)
    def _():
        o_ref[...] = acc_ref[...].astype(o_ref.dtype)


def matmul(a, b, tm, tn, tk):
    M, K = a.shape
    N = b.shape[1]
    nk = K // tk
    return pl.pallas_call(
        functools.partial(_mm_kernel, nk=nk),
        grid=(M // tm, N // tn, nk),
        in_specs=[pl.BlockSpec((tm, tk), lambda i, j, k: (i, k)),
                  pl.BlockSpec((tk, tn), lambda i, j, k: (k, j))],
        out_specs=pl.BlockSpec((tm, tn), lambda i, j, k: (i, j)),
        out_shape=jax.ShapeDtypeStruct((M, N), F32),
        scratch_shapes=[pltpu.VMEM((tm, tn), F32)],
        compiler_params=_params(("parallel", "parallel", "arbitrary"), 40),
        name="matmul",
    )(a, b)


def _ada_kernel(s_ref, w_ref, b_ref, o_ref):
    o_ref[...] = _dot(s_ref[...].astype(BF16), w_ref[...].astype(BF16)) + b_ref[...]


def ada_modulation(s, ada_w, ada_b):
    depth, D, N = ada_w.shape
    R = s.shape[0]
    tn = 1024
    return pl.pallas_call(
        _ada_kernel,
        grid=(depth, N // tn),
        in_specs=[pl.BlockSpec((R, D), lambda l, j: (0, 0)),
                  pl.BlockSpec((None, D, tn), lambda l, j: (l, 0, j)),
                  pl.BlockSpec((None, 1, tn), lambda l, j: (l, 0, j))],
        out_specs=pl.BlockSpec((None, R, tn), lambda l, j: (l, 0, j)),
        out_shape=jax.ShapeDtypeStruct((depth, R, N), F32),
        compiler_params=_params(("parallel", "parallel"), 32),
        name="ada_modulation",
    )(s, ada_w, ada_b.reshape(depth, 1, N))


def _in_proj_kernel(h_ref, g_ref, sc_ref, sh_ref, w_ref, cos_ref, sin_ref, qg_ref, kg_ref, *o_refs, hy_cols, has_q, width):
    a = _norm_mod(h_ref[0], g_ref[...], sc_ref[0], sh_ref[0]).astype(BF16)
    cos = cos_ref[...]
    sin = sin_ref[...]
    outs = list(o_refs)
    hy_ref = outs.pop(0) if hy_cols else None
    q_ref = outs.pop(0) if has_q else None
    k_ref, v_ref = outs
    q_col = hy_cols
    k_col = q_col + (width if has_q else 0)
    v_col = k_col + width

    def proj(c0, n):
        return _dot(a, w_ref[:, c0:c0 + n])

    pq = proj(q_col, width) if has_q else None
    pk = proj(k_col, width)
    if hy_cols:
        hy_ref[0] = proj(0, hy_cols)
    for hd in range(width // LANES):
        sl = slice(hd * LANES, (hd + 1) * LANES)
        if has_q:
            q_ref[0, :, sl] = _head_norm_rope(pq[:, sl], qg_ref[...], cos, sin, LOG2E * DIFF_HEAD_DIM ** -0.5).astype(BF16)
        k_ref[0, :, sl] = _head_norm_rope(pk[:, sl], kg_ref[...], cos, sin, 1.0).astype(BF16)
    pv = proj(v_col, width)
    ones = jnp.ones((pv.shape[0], LANES), BF16)
    for hd in range(width // LANES):
        v_ref[0, :, 2 * hd * LANES:(2 * hd + 1) * LANES] = pv[:, hd * LANES:(hd + 1) * LANES].astype(BF16)
        v_ref[0, :, (2 * hd + 1) * LANES:(2 * hd + 2) * LANES] = ones


def in_proj(h, g, scale, shift, w, hy_cols, has_q, cos, sin_signed, q_g, k_g, tm):
    B, S, D = h.shape
    N = w.shape[1]
    width = DIFF_HEADS * 2 * DIFF_HEAD_DIM
    g2 = lambda gg: jnp.concatenate([gg, gg]).reshape(1, LANES)
    mod = pl.BlockSpec((1, 1, D), lambda b, i: (b, 0, 0))
    tab = pl.BlockSpec((tm, LANES), lambda b, i: (i, 0))
    vec = pl.BlockSpec((1, LANES), lambda b, i: (0, 0))
    widths = ([hy_cols] if hy_cols else []) + ([width] if has_q else []) + [width, 2 * width]
    dtypes = ([F32] if hy_cols else []) + ([BF16] if has_q else []) + [BF16, BF16]
    return pl.pallas_call(
        functools.partial(_in_proj_kernel, hy_cols=hy_cols, has_q=has_q, width=width),
        grid=(B, S // tm),
        in_specs=[pl.BlockSpec((1, tm, D), lambda b, i: (b, i, 0)),
                  pl.BlockSpec((1, D), lambda b, i: (0, 0)), mod, mod,
                  pl.BlockSpec((D, N), lambda b, i: (0, 0)), tab, tab, vec, vec],
        out_specs=[pl.BlockSpec((1, tm, wd), lambda b, i: (b, i, 0)) for wd in widths],
        out_shape=[jax.ShapeDtypeStruct((B, S, wd), dt) for wd, dt in zip(widths, dtypes)],
        compiler_params=_params(("parallel", "parallel"), 56),
        name="in_proj",
    )(h, g.reshape(1, D), scale, shift, w, cos, sin_signed, g2(q_g), g2(k_g))


def _sconv_kernel(p_ref, w_ref, b_ref, o_ref):
    x = p_ref[0]
    S = x.shape[0]
    row = lax.broadcasted_iota(I32, x.shape, 0)
    xm = jnp.where(row == 0, 0.0, pltpu.roll(x, 1, 0))
    xp = jnp.where(row == S - 1, 0.0, pltpu.roll(x, S - 1, 0))
    w = w_ref[...]
    o_ref[0] = (xm * w[0:1] + x * w[1:2] + xp * w[2:3] + b_ref[...]).astype(o_ref.dtype)


def short_conv(p, conv_w, conv_b, col0, width, tc, out_dtype):
    B, S, _ = p.shape
    c0 = col0 // tc
    return pl.pallas_call(
        _sconv_kernel,
        grid=(B, width // tc),
        in_specs=[pl.BlockSpec((1, S, tc), lambda b, c: (b, 0, c + c0)),
                  pl.BlockSpec((3, tc), lambda b, c: (0, c + c0)),
                  pl.BlockSpec((1, tc), lambda b, c: (0, c + c0))],
        out_specs=pl.BlockSpec((1, S, tc), lambda b, c: (b, 0, c)),
        out_shape=jax.ShapeDtypeStruct((B, S, width), out_dtype),
        compiler_params=_params(("parallel", "parallel"), 48),
        name="short_conv",
    )(p, conv_w, conv_b.reshape(1, -1))


def _dft_fwd_kernel(c_ref, s_ref, u_ref, kre_ref, kim_ref, y_ref, *, nblk):
    c = c_ref[...]
    s = s_ref[...]
    C = u_ref.shape[3]
    groups = [slice(g * LANES * 2, (g + 1) * LANES * 2) for g in range(C // (2 * LANES))] if C % (2 * LANES) == 0 else [slice(0, C)]

    def transforms(ch):
        return [(_dot(c, u_ref[0, j, :, ch]), _dot(s, u_ref[0, j, :, ch])) for j in range(nblk)]

    def products(ch, spec):
        for i in range(nblk):
            yre = yim = None
            for j in range(nblk):
                d = i - j + nblk - 1
                kre = kre_ref[:, d * C + ch.start:d * C + ch.stop]
                kim = kim_ref[:, d * C + ch.start:d * C + ch.stop]
                ure, usn = spec[j]
                tre = kre * ure + kim * usn
                tim = kim * ure - kre * usn
                yre = tre if yre is None else yre + tre
                yim = tim if yim is None else yim + tim
            y_ref[0, i, 0, :, ch] = yre.astype(BF16)
            y_ref[0, i, 1, :, ch] = yim.astype(BF16)

    spec = transforms(groups[0])
    for g, ch in enumerate(groups):
        nxt = transforms(groups[g + 1]) if g + 1 < len(groups) else None
        products(ch, spec)
        spec = nxt


def dft_forward(cf, sf, u, kre, kim, order, tf):
    B, nblk, Lb, C = u.shape
    nd = 2 * nblk - 1
    return pl.pallas_call(
        functools.partial(_dft_fwd_kernel, nblk=nblk),
        grid=(Lb // tf, B),
        in_specs=[pl.BlockSpec((tf, Lb), lambda f, b: (f, 0)),
                  pl.BlockSpec((tf, Lb), lambda f, b: (f, 0)),
                  pl.BlockSpec((1, nblk, Lb, C), lambda f, b: (b, 0, 0, 0)),
                  pl.BlockSpec((tf, nd * C), lambda f, b: (f, order)),
                  pl.BlockSpec((tf, nd * C), lambda f, b: (f, order))],
        out_specs=pl.BlockSpec((1, nblk, 2, tf, C), lambda f, b: (b, 0, 0, f, 0)),
        out_shape=jax.ShapeDtypeStruct((B, nblk, 2, Lb, C), BF16),
        compiler_params=_params(("parallel", "parallel"), 56),
        name="dft_forward",
    )(cf, sf, u, kre, kim)


def _dft_inv_kernel(ct_ref, st_ref, y_ref, gate_ref, o_ref, *, scale):
    acc = _dot(ct_ref[...], y_ref[0, 0, 0]) - _dot(st_ref[...], y_ref[0, 0, 1])
    o_ref[0, 0] = (gate_ref[0, 0] * (acc * scale)).astype(BF16)


def dft_inverse(ct, st, y, gates, gate_col):
    B, nblk, _, Lb, C = y.shape
    return pl.pallas_call(
        functools.partial(_dft_inv_kernel, scale=1.0 / Lb),
        grid=(B, nblk),
        in_specs=[pl.BlockSpec((Lb, Lb), lambda b, i: (0, 0)),
                  pl.BlockSpec((Lb, Lb), lambda b, i: (0, 0)),
                  pl.BlockSpec((1, 1, 2, Lb, C), lambda b, i: (b, i, 0, 0, 0)),
                  pl.BlockSpec((1, 1, Lb, C), lambda b, i: (b, i, 0, gate_col))],
        out_specs=pl.BlockSpec((1, 1, Lb, C), lambda b, i: (b, i, 0, 0)),
        out_shape=jax.ShapeDtypeStruct((B, nblk, Lb, C), BF16),
        compiler_params=_params(("parallel", "parallel"), 48),
        name="dft_inverse",
    )(ct, st, y, gates)


def dft_tables(L):
    n = 2 * L
    f = lax.broadcasted_iota(I32, (L, L), 0)
    t = lax.broadcasted_iota(I32, (L, L), 1)
    m = ((2 * f + 1) * t) % (2 * n)
    ang = m.astype(F32) * (math.pi / n)
    cf = jnp.cos(ang)
    sf = jnp.sin(ang)
    return cf.astype(BF16), sf.astype(BF16), cf.T.astype(BF16), sf.T.astype(BF16)


def hyena_filter_taps(L, w1, b1, w2, b2, w3, b3, freq, wout, width):
    hp = lax.Precision.HIGHEST
    t = jnp.linspace(0.0, 1.0, L, dtype=F32)[:, None]
    bands = (FILTER_EMB - 1) // 2
    w = 2.0 * math.pi * jnp.arange(L, dtype=F32)[:, None] / L
    f = jnp.linspace(1e-4, bands - 1, bands, dtype=F32)[None, :]
    z = jnp.concatenate([t, jnp.cos(f * w), -jnp.sin(f * w)], axis=-1)
    h = jnp.sin(freq * (jnp.dot(z, w1, precision=hp) + b1))
    h = jnp.sin(freq * (jnp.dot(h, w2, precision=hp) + b2))
    h = jnp.sin(freq * (jnp.dot(h, w3, precision=hp) + b3))
    max_decay = math.log(FILTER_TARGET) / FILTER_FAST_DECAY
    min_decay = math.log(FILTER_TARGET) / FILTER_SLOW_DECAY
    deltas = jnp.abs(jnp.linspace(min_decay, max_decay, width, dtype=F32))

    def taps(hh, tt):
        return jnp.dot(hh, wout, precision=hp).reshape(L, HYENA_ORDER, 2, width) * jnp.exp(-tt * deltas[None, :])[:, None, None, :]

    return taps(h, t), taps(h[::-1], t[::-1])


def filter_spectra(cf, sf, taps, taps_rev, bias, nblk):
    L, C = taps.shape[0], taps.shape[3]
    Lb = L // nblk
    zero = jnp.zeros((1, C), F32)
    sums, diffs = [], []
    for o in range(HYENA_ORDER):
        h_fwd, h_bwd = taps[:, o, 0], taps[:, o, 1]
        h_fwd_rev, h_bwd_rev = taps_rev[:, o, 0], taps_rev[:, o, 1]
        k0 = (h_fwd[0] + h_bwd[0] + bias[o])[None, :]
        kfull = jnp.concatenate([zero, h_bwd_rev[:L - 1], k0, h_fwd[1:]], axis=0)
        krev = jnp.concatenate([h_fwd_rev[:L - 1], k0, h_bwd[1:], zero], axis=0)
        for dlt in range(-(nblk - 1), nblk):
            base = L + dlt * Lb
            kp = kfull[base:base + Lb]
            km = jnp.concatenate([zero, krev[2 * L - base:2 * L - base + Lb - 1]], axis=0)
            sums.append(kp + km)
            diffs.append(km - kp)
    tmm = min(Lb, 512)
    kre = matmul(cf, jnp.concatenate(sums, axis=1), tmm, tmm, tmm)
    kim = matmul(sf, jnp.concatenate(diffs, axis=1), tmm, tmm, tmm)
    return kre, kim


def hyena_operator(gates, v, tables, taps, taps_rev, bias, nblk, tf):
    cf, sf, ct, st = tables
    B, L, C = v.shape
    Lb = L // nblk
    kre, kim = filter_spectra(cf, sf, taps, taps_rev, bias, nblk)
    gates = gates.reshape(B, nblk, Lb, 2 * C)
    z = v.reshape(B, nblk, Lb, C)
    for o in range(HYENA_ORDER):
        y = dft_forward(cf, sf, z, kre, kim, o, tf)
        z = dft_inverse(ct, st, y, gates, o)
    return z.reshape(B, L, C)


def _head_norm_rope(x, g, cos, sin_signed, scale):
    lane = lax.broadcasted_iota(I32, x.shape, 1)
    lo = lane < DIFF_HEAD_DIM
    x2 = x * x
    s_lo = jnp.sum(jnp.where(lo, x2, 0.0), axis=-1, keepdims=True)
    s_hi = jnp.sum(jnp.where(lo, 0.0, x2), axis=-1, keepdims=True)
    ms = jnp.where(lo, s_lo, s_hi) * (1.0 / DIFF_HEAD_DIM)
    xn = x * lax.rsqrt(ms + NORM_EPS) * g
    first = (lane & 16) == 0
    partner = jnp.where(first, pltpu.roll(xn, LANES - 16, 1), pltpu.roll(xn, 16, 1))
    return (xn * cos + partner * sin_signed) * scale


def rope_tables(S, use_rope):
    if not use_rope:
        return jnp.ones((S, LANES), F32), jnp.zeros((S, LANES), F32)
    t = jnp.arange(S, dtype=I32)
    row = (t // GRID_W).astype(F32)[:, None]
    colp = (t % GRID_W).astype(F32)[:, None]
    nf = DIFF_HEAD_DIM // 4
    inv = ROPE_BASE ** (-jnp.arange(nf, dtype=F32) / nf)
    lane = jnp.arange(LANES)
    grp = (lane % DIFF_HEAD_DIM) // nf
    j = lane % nf
    pos = jnp.where((grp < 2)[None, :], row, colp)
    ang = pos * inv[j][None, :]
    sign = jnp.where((grp % 2 == 0)[None, :], -1.0, 1.0)
    return jnp.cos(ang), jnp.sin(ang) * sign


def _attn_kernel(lam_ref, q_ref, kc_ref, vc_ref, *rest, has_lat, out_scale, ck):
    if has_lat:
        kl_ref, vl_ref, g_ref, o_ref = rest
    else:
        g_ref, o_ref = rest
    lam = lam_ref[0]
    q = q_ref[0]
    lane = lax.broadcasted_iota(I32, q.shape, 1)
    nt = (((1,), (1,)), ((), ()))
    zero = jnp.zeros_like(q)
    qm = [jnp.where(lane < DIFF_HEAD_DIM, q, zero), jnp.where(lane >= DIFF_HEAD_DIM, q, zero)]
    chunks = [(kc_ref, vc_ref, 0, kc_ref.shape[1])]
    if has_lat:
        chunks += [(kl_ref, vl_ref, c * ck, ck) for c in range(kl_ref.shape[1] // ck)]

    def scores(ch):
        k = ch[0][0, ch[2]:ch[2] + ch[3], :]
        return [lax.dot_general(qm[mp], k, nt, preferred_element_type=F32) for mp in range(2)]

    m, acc = [None, None], [None, None]
    s_next = scores(chunks[0])
    for ci, ch in enumerate(chunks):
        s_cur = s_next
        if ci + 1 < len(chunks):
            s_next = scores(chunks[ci + 1])
        v1 = ch[1][0, ch[2]:ch[2] + ch[3], :]
        for mp in range(2):
            s = s_cur[mp]
            mx = jnp.max(s, axis=-1, keepdims=True)
            m_new = mx if ci == 0 else jnp.maximum(m[mp], mx)
            pv = _dot(jnp.exp2(s - m_new).astype(BF16), v1)
            acc[mp] = pv if ci == 0 else jnp.exp2(m[mp] - m_new) * acc[mp] + pv
            m[mp] = m_new
    o = acc[0][:, :LANES] / acc[0][:, LANES:] - lam * (acc[1][:, :LANES] / acc[1][:, LANES:])
    ms = jnp.mean(o * o, axis=-1, keepdims=True)
    o_ref[0] = ((o * lax.rsqrt(ms + NORM_EPS) * g_ref[...]) * out_scale).astype(BF16)


def diff_attention(lam, q, k_ctx, v_ctx, k_lat, v_lat, subln_g, out_scale, tq):
    B, Sq, W = q.shape
    H = W // LANES
    has_lat = k_lat is not None
    Sc = k_ctx.shape[1]
    head = lambda S, w: pl.BlockSpec((1, S, w), lambda b, h, i: (b, 0, h))
    in_specs = [pl.BlockSpec(memory_space=pltpu.SMEM),
                pl.BlockSpec((1, tq, LANES), lambda b, h, i: (b, i, h)), head(Sc, LANES), head(Sc, 2 * LANES)]
    args = [lam.reshape(1), q, k_ctx, v_ctx]
    if has_lat:
        in_specs += [head(k_lat.shape[1], LANES), head(k_lat.shape[1], 2 * LANES)]
        args += [k_lat, v_lat]
    in_specs.append(pl.BlockSpec((1, LANES), lambda b, h, i: (0, 0)))
    args.append(subln_g.reshape(1, LANES))
    return pl.pallas_call(
        functools.partial(_attn_kernel, has_lat=has_lat, out_scale=out_scale,
                          ck=min(ATTN_KEY_CHUNK, k_lat.shape[1]) if has_lat else 0),
        grid=(B, H, Sq // tq),
        in_specs=in_specs,
        out_specs=pl.BlockSpec((1, tq, LANES), lambda b, h, i: (b, i, h)),
        out_shape=jax.ShapeDtypeStruct((B, Sq, W), BF16),
        compiler_params=_params(("parallel", "parallel", "parallel"), 56),
        name="diff_attention",
    )(*args)


def _oproj_kernel(hy_ref, o_ref, w_ref, h_ref, gate_ref, *rest, half, n_exp):
    prep_in, (out_ref, *prep_out) = rest[:4], rest[4:]
    y = _dot(hy_ref[0].astype(BF16), w_ref[0:half, :]) + _dot(o_ref[0], w_ref[half:, :])
    h_new = h_ref[0] + gate_ref[0] * y
    out_ref[0] = h_new
    _ffn_prep_store(h_new, *prep_in, *prep_out, n_exp)


def out_proj(hy, o, w_out, h, gate, prep, tm):
    B, S, D = h.shape
    half = hy.shape[2]
    n_exp = prep[3].shape[1]
    p_in, p_out, p_shape = _ffn_prep_specs(B, S, D, tm, n_exp)
    return pl.pallas_call(
        functools.partial(_oproj_kernel, half=half, n_exp=n_exp),
        grid=(B, S // tm),
        in_specs=[pl.BlockSpec((1, tm, half), lambda b, i: (b, i, 0)),
                  pl.BlockSpec((1, tm, half), lambda b, i: (b, i, 0)),
                  pl.BlockSpec((2 * half, D), lambda b, i: (0, 0)),
                  pl.BlockSpec((1, tm, D), lambda b, i: (b, i, 0)),
                  pl.BlockSpec((1, 1, D), lambda b, i: (b, 0, 0))] + p_in,
        out_specs=[pl.BlockSpec((1, tm, D), lambda b, i: (b, i, 0))] + p_out,
        out_shape=[jax.ShapeDtypeStruct((B, S, D), F32)] + p_shape,
        compiler_params=_params(("parallel", "parallel"), 40),
        name="out_proj",
    )(hy, o, w_out, h, gate, *_ffn_prep_args(prep))


def _shift_rows(x, d):
    return pltpu.roll(x, (-d) % x.shape[0], 0)


def _pool_kernel(hp_ref, hc_ref, hn_ref, g_ref, sc_ref, sh_ref, gate_ref, pw_ref, ps_ref, *rest, T, L, n_exp):
    prep_in, (o_ref, *prep_out) = rest[:4], rest[4:]
    i = pl.program_id(1)
    nt = pl.num_programs(1)
    g, sc, sh = g_ref[...], sc_ref[0], sh_ref[0]
    hc = hc_ref[0]
    a_c = _norm_mod(hc, g, sc, sh)
    a_p = jnp.where(i == 0, 0.0, _norm_mod(hp_ref[0], g, sc, sh))
    a_n = jnp.where(i == nt - 1, 0.0, _norm_mod(hn_ref[0], g, sc, sh))
    ext = jnp.concatenate([a_p, a_c, a_n], axis=0)
    tok = i * T + lax.broadcasted_iota(I32, (T, 1), 0)
    G = ext.shape[1] // len(POOL_WINDOWS)
    ys = []
    for gi, w in enumerate(POOL_WINDOWS):
        xg = ext[:, gi * G:(gi + 1) * G]
        s = _shift_rows(xg, -1) + xg
        step = 1
        while 2 * step < w:
            s = _shift_rows(s, -step) + _shift_rows(s, step)
            step *= 2
        cnt = (jnp.minimum(tok + w // 2, L) - jnp.maximum(tok - w // 2, 0)).astype(F32)
        p = s[POOL_HALO:POOL_HALO + T] / cnt - a_c[:, gi * G:(gi + 1) * G]
        ys.append(_dot(p.astype(BF16), pw_ref[gi]))
    y = jnp.concatenate(ys, axis=1) * ps_ref[...]
    h_new = hc + gate_ref[0] * y
    o_ref[0] = h_new
    _ffn_prep_store(h_new, *prep_in, *prep_out, n_exp)


def pool_mixer(h, g, scale, shift, gate, pool_w, pool_scale, prep, T):
    B, S, D = h.shape
    nh = T // POOL_HALO
    last = S // POOL_HALO - 1
    n_exp = prep[3].shape[1]
    p_in, p_out, p_shape = _ffn_prep_specs(B, S, D, T, n_exp)
    mod = pl.BlockSpec((1, 1, D), lambda b, i: (b, 0, 0))
    return pl.pallas_call(
        functools.partial(_pool_kernel, T=T, L=S, n_exp=n_exp),
        grid=(B, S // T),
        in_specs=[pl.BlockSpec((1, POOL_HALO, D), lambda b, i: (b, jnp.maximum(i * nh - 1, 0), 0)),
                  pl.BlockSpec((1, T, D), lambda b, i: (b, i, 0)),
                  pl.BlockSpec((1, POOL_HALO, D), lambda b, i: (b, jnp.minimum((i + 1) * nh, last), 0)),
                  pl.BlockSpec((1, D), lambda b, i: (0, 0)), mod, mod, mod,
                  pl.BlockSpec(pool_w.shape, lambda b, i: (0, 0, 0)),
                  pl.BlockSpec((1, D), lambda b, i: (0, 0))] + p_in,
        out_specs=[pl.BlockSpec((1, T, D), lambda b, i: (b, i, 0))] + p_out,
        out_shape=[jax.ShapeDtypeStruct((B, S, D), F32)] + p_shape,
        compiler_params=_params(("parallel", "parallel"), 48),
        name="pool_mixer",
    )(h, h, h, g.reshape(1, D), scale, shift, gate, pool_w, pool_scale.reshape(1, D), *_ffn_prep_args(prep))


def _ffn_prep_store(h_new, g_ref, sc_ref, sh_ref, rw_ref, x_ref, aff_ref, afft_ref, n_exp):
    a = _norm_mod(h_new, g_ref[...], sc_ref[0], sh_ref[0]).astype(BF16)
    x_ref[0] = a
    logits = _dot(a, rw_ref[...])
    lane = lax.broadcasted_iota(I32, logits.shape, 1)
    valid = lane < n_exp
    mx = jnp.max(jnp.where(valid, logits, -jnp.inf), axis=-1, keepdims=True)
    e = jnp.where(valid, jnp.exp(logits - mx), 0.0)
    aff = e / jnp.sum(e, axis=-1, keepdims=True)
    aff_ref[0] = aff
    afft_ref[0] = aff.T[0:n_exp]


def _ffn_prep_specs(B, S, D, tm, n_exp):
    mod = pl.BlockSpec((1, 1, D), lambda b, i: (b, 0, 0))
    ins = [pl.BlockSpec((1, D), lambda b, i: (0, 0)), mod, mod, pl.BlockSpec((D, LANES), lambda b, i: (0, 0))]
    outs = [pl.BlockSpec((1, tm, D), lambda b, i: (b, i, 0)),
            pl.BlockSpec((1, tm, LANES), lambda b, i: (b, i, 0)),
            pl.BlockSpec((1, n_exp, tm), lambda b, i: (b, 0, i))]
    shapes = [jax.ShapeDtypeStruct((B, S, D), BF16), jax.ShapeDtypeStruct((B, S, LANES), F32),
              jax.ShapeDtypeStruct((B, n_exp, S), F32)]
    return ins, outs, shapes


def _ffn_prep_args(prep):
    g, scale, shift, router_w = prep
    rw = jnp.pad(router_w, ((0, 0), (0, LANES - router_w.shape[1]))).astype(BF16)
    return [g.reshape(1, -1), scale, shift, rw]


def _prefix_excl(m, tri, tb):
    S = m.shape[0]
    carry = jnp.zeros((1, m.shape[1]), F32)
    outs, carries = [], []
    for blk in range(S // tb):
        mb = m[blk * tb:(blk + 1) * tb]
        outs.append(_dot(tri, mb.astype(BF16)) + carry)
        carries.append(carry)
        carry = carry + jnp.sum(mb, axis=0, keepdims=True)
    carries.append(carry)
    return (jnp.concatenate(outs, axis=0) if len(outs) > 1 else outs[0]), jnp.concatenate(carries, axis=0)


def _select_kernel(aff_ref, afft_ref, tri_ref, pos_ref, post_ref, offs_ref, *, cap, tb, n_exp):
    bits = pltpu.bitcast(afft_ref[0], I32)

    def body(i, cur):
        cand = cur | (jnp.int32(1) << (30 - i))
        cnt = jnp.sum((bits >= cand).astype(F32), axis=1, keepdims=True)
        return jnp.where(cnt >= cap, cand, cur)

    thr_bits = lax.fori_loop(0, 31, body, jnp.zeros((n_exp, 1), I32))
    thr_col = pltpu.bitcast(jnp.broadcast_to(thr_bits, (n_exp, LANES)), F32)
    sub = lax.broadcasted_iota(I32, (n_exp, LANES), 0)
    lane = lax.broadcasted_iota(I32, (n_exp, LANES), 1)
    thr = jnp.sum(jnp.where(sub == lane, thr_col, 0.0), axis=0, keepdims=True)
    aff = aff_ref[0]
    tri = tri_ref[...]
    gt = (aff > thr).astype(F32)
    eq = (aff == thr).astype(F32)
    need = cap - jnp.sum(gt, axis=0, keepdims=True)
    sel = gt + eq * (_prefix_excl(eq, tri, tb)[0] < need).astype(F32)
    slot, offs = _prefix_excl(sel, tri, tb)
    sp = jnp.where(sel > 0.0, slot, -1.0)
    pos_ref[0] = sp
    offs_ref[0] = offs
    for blk in range(sp.shape[0] // tb):
        post_ref[0, :, blk * tb:(blk + 1) * tb] = sp[blk * tb:(blk + 1) * tb].T[0:n_exp]


def select_tokens(aff, afft, cap):
    B, S, _ = aff.shape
    n_exp = afft.shape[1]
    tb = min(S, 256)
    r = lax.broadcasted_iota(I32, (tb, tb), 0)
    c = lax.broadcasted_iota(I32, (tb, tb), 1)
    tri = (c < r).astype(BF16)
    nblk = S // tb
    pos, post, offs = pl.pallas_call(
        functools.partial(_select_kernel, cap=cap, tb=tb, n_exp=n_exp),
        grid=(B,),
        in_specs=[pl.BlockSpec((1, S, LANES), lambda b: (b, 0, 0)),
                  pl.BlockSpec((1, n_exp, S), lambda b: (b, 0, 0)),
                  pl.BlockSpec((tb, tb), lambda b: (0, 0))],
        out_specs=[pl.BlockSpec((1, S, LANES), lambda b: (b, 0, 0)),
                   pl.BlockSpec((1, n_exp, S), lambda b: (b, 0, 0)),
                   pl.BlockSpec((1, nblk + 1, LANES), lambda b: (b, 0, 0))],
        out_shape=[jax.ShapeDtypeStruct((B, S, LANES), F32), jax.ShapeDtypeStruct((B, n_exp, S), F32),
                   jax.ShapeDtypeStruct((B, nblk + 1, LANES), F32)],
        compiler_params=_params(("parallel",), 48),
        name="select_tokens",
    )(aff, afft, tri)
    offs = jnp.swapaxes(offs[:, :, :n_exp], 1, 2).astype(I32).reshape(-1)
    return pos, post, offs, tb


def _gather_kernel(offs_ref, x_ref, post_ref, afft_ref, o_ref, g_ref, *, nblk, n_exp, win):
    b = pl.program_id(0)
    t = pl.program_id(1)
    slots = o_ref.shape[2]

    @pl.when(t == 0)
    def _():
        o_ref[...] = jnp.zeros_like(o_ref)
        g_ref[...] = jnp.zeros_like(g_ref)

    x = x_ref[0]
    T = x.shape[0]

    def full():
        r = lax.broadcasted_iota(I32, (slots, T), 0).astype(F32)
        for e in range(n_exp):
            hit = r == post_ref[0, e:e + 1, :]
            o_ref[e, 0] = (o_ref[e, 0].astype(F32) + _dot(hit.astype(BF16), x)).astype(BF16)
            g = jnp.sum(jnp.where(hit, afft_ref[0, e:e + 1, :], 0.0), axis=1, keepdims=True)
            g_ref[e, 0] = g_ref[e, 0] + jnp.broadcast_to(g, (slots, LANES))

    if win >= slots:
        full()
        return
    starts = []
    narrow = None
    for e in range(n_exp):
        base = (b * n_exp + e) * (nblk + 1) + t
        lo = offs_ref[base]
        hi = offs_ref[base + 1]
        a0 = pl.multiple_of(jnp.minimum((lo // BF16_ROWS) * BF16_ROWS, slots - win), BF16_ROWS)
        ok = hi - a0 <= win
        narrow = ok if narrow is None else jnp.logical_and(narrow, ok)
        starts.append(a0)

    @pl.when(narrow)
    def _():
        r = lax.broadcasted_iota(I32, (win, T), 0).astype(F32)
        hits = [r + starts[e].astype(F32) == post_ref[0, e:e + 1, :] for e in range(n_exp)]
        rows = _dot(jnp.concatenate([h.astype(BF16) for h in hits], axis=0), x)
        old = [o_ref[e, 0, pl.ds(starts[e], win), :] for e in range(n_exp)]
        old_g = [g_ref[e, 0, pl.ds(starts[e], win), :] for e in range(n_exp)]
        for e in range(n_exp):
            o_ref[e, 0, pl.ds(starts[e], win), :] = (old[e].astype(F32) + rows[e * win:(e + 1) * win]).astype(BF16)
            g = jnp.sum(jnp.where(hits[e], afft_ref[0, e:e + 1, :], 0.0), axis=1, keepdims=True)
            g_ref[e, 0, pl.ds(starts[e], win), :] = old_g[e] + jnp.broadcast_to(g, (win, LANES))

    pl.when(jnp.logical_not(narrow))(full)


def gather_tokens(x, post, afft, offs, slots, T):
    B, S, D = x.shape
    E = post.shape[1]
    row = pl.BlockSpec((1, E, T), lambda b, t, offs: (b, 0, t))
    return pl.pallas_call(
        functools.partial(_gather_kernel, nblk=S // T, n_exp=E, win=min(slots, GATHER_WINDOW)),
        grid_spec=pltpu.PrefetchScalarGridSpec(
            num_scalar_prefetch=1,
            grid=(B, S // T),
            in_specs=[pl.BlockSpec((1, T, D), lambda b, t, offs: (b, t, 0)), row, row],
            out_specs=[pl.BlockSpec((E, 1, slots, D), lambda b, t, offs: (0, b, 0, 0)),
                       pl.BlockSpec((E, 1, slots, LANES), lambda b, t, offs: (0, b, 0, 0))]),
        out_shape=[jax.ShapeDtypeStruct((E, B, slots, D), BF16), jax.ShapeDtypeStruct((E, B, slots, LANES), F32)],
        compiler_params=_params(("parallel", "arbitrary"), 56),
        name="gather_tokens",
    )(offs, x, post, afft)


def _ffn_kernel(xs_ref, g_ref, wg_ref, wu_ref, wd_ref, o_ref, hm_ref, *, nf):
    f = pl.program_id(2)
    x = xs_ref[0]
    a = _dot(x, wg_ref[...].astype(BF16))
    u = _dot(x, wu_ref[...].astype(BF16))
    hm_ref[f] = (a * jax.nn.sigmoid(a) * u).astype(BF16)

    @pl.when(f == nf - 1)
    def _():
        hm = jnp.concatenate([hm_ref[c] for c in range(nf)], axis=1) if nf > 1 else hm_ref[0]
        o_ref[0] = (_dot(hm, wd_ref[...].astype(BF16)) * g_ref[0, :, 0:1]).astype(BF16)


def expert_ffn(xs, gs, w_gate, w_up, w_down, layer, tm, tf):
    E, M, D = xs.shape
    F = w_gate.shape[3]
    nf = F // tf
    return pl.pallas_call(
        functools.partial(_ffn_kernel, nf=nf),
        grid=(E, M // tm, nf),
        in_specs=[pl.BlockSpec((1, tm, D), lambda e, m, f: (e, m, 0)),
                  pl.BlockSpec((1, tm, LANES), lambda e, m, f: (e, m, 0)),
                  pl.BlockSpec((None, None, D, tf), lambda e, m, f: (layer, e, 0, f)),
                  pl.BlockSpec((None, None, D, tf), lambda e, m, f: (layer, e, 0, f)),
                  pl.BlockSpec((None, None, F, D), lambda e, m, f: (layer, e, 0, 0))],
        out_specs=pl.BlockSpec((1, tm, D), lambda e, m, f: (e, m, 0)),
        out_shape=jax.ShapeDtypeStruct((E, M, D), BF16),
        scratch_shapes=[pltpu.VMEM((nf, tm, tf), BF16)],
        compiler_params=_params(("parallel", "parallel", "arbitrary"), 56),
        name="expert_ffn",
    )(xs, gs, w_gate, w_up, w_down)


def _combine_kernel(offs_ref, y_ref, pos_ref, h_ref, gate_ref, o_ref, ybuf_ref, *, nblk, n_exp, win):
    b = pl.program_id(0)
    t = pl.program_id(1)
    T = pos_ref.shape[1]
    slots = y_ref.shape[2]

    def full():
        if slots < LANES:
            K = n_exp * slots
            col = lax.broadcasted_iota(I32, (LANES, K), 1).astype(F32)
            owner = jnp.floor((col + 0.5) * (1.0 / slots))
            rep = owner == lax.broadcasted_iota(I32, (LANES, K), 0).astype(F32)
            slot_rep = _dot(pos_ref[0].astype(BF16), rep.astype(BF16))
            want = (col - owner * slots)[0:1]
            acc = _dot((slot_rep == want).astype(BF16), y_ref[:, 0].reshape(K, y_ref.shape[3]))
        else:
            r = lax.broadcasted_iota(I32, (T, slots), 1).astype(F32)
            acc = None
            for e in range(n_exp):
                part = _dot((pos_ref[0, :, e:e + 1] == r).astype(BF16), y_ref[e, 0])
                acc = part if acc is None else acc + part
        o_ref[0] = h_ref[0] + gate_ref[0] * acc

    if win >= slots:
        full()
        return
    starts = []
    narrow = None
    for e in range(n_exp):
        base = (b * n_exp + e) * (nblk + 1) + t
        lo = offs_ref[base]
        hi = offs_ref[base + 1]
        a0 = jnp.minimum((lo // BF16_ROWS) * BF16_ROWS, slots - win)
        ok = hi - a0 <= win
        narrow = ok if narrow is None else jnp.logical_and(narrow, ok)
        starts.append(a0)

    @pl.when(narrow)
    def _():
        r = lax.broadcasted_iota(I32, (T, win), 1).astype(F32)
        pieces = []
        for e in range(n_exp):
            a0 = starts[e]
            ybuf_ref[e * win:(e + 1) * win, :] = y_ref[e, 0, pl.ds(pl.multiple_of(a0, BF16_ROWS), win), :]
            pieces.append((pos_ref[0, :, e:e + 1] - a0.astype(F32) == r).astype(BF16))
        o_ref[0] = h_ref[0] + gate_ref[0] * _dot(jnp.concatenate(pieces, axis=1), ybuf_ref[...])

    pl.when(jnp.logical_not(narrow))(full)


def combine_tokens(y, pos, offs, h, gate, T):
    E, B, slots, D = y.shape
    S = h.shape[1]
    win = min(slots, COMBINE_WINDOW)
    return pl.pallas_call(
        functools.partial(_combine_kernel, nblk=S // T, n_exp=E, win=win),
        grid_spec=pltpu.PrefetchScalarGridSpec(
            num_scalar_prefetch=1,
            grid=(B, S // T),
            in_specs=[pl.BlockSpec((E, 1, slots, D), lambda b, t, offs: (0, b, 0, 0)),
                      pl.BlockSpec((1, T, LANES), lambda b, t, offs: (b, t, 0)),
                      pl.BlockSpec((1, T, D), lambda b, t, offs: (b, t, 0)),
                      pl.BlockSpec((1, 1, D), lambda b, t, offs: (b, 0, 0))],
            out_specs=pl.BlockSpec((1, T, D), lambda b, t, offs: (b, t, 0)),
            scratch_shapes=[pltpu.VMEM((E * win, D), BF16)]),
        out_shape=jax.ShapeDtypeStruct((B, S, D), F32),
        compiler_params=_params(("parallel", "parallel"), 56),
        name="combine_tokens",
    )(offs, y, pos, h, gate)


def moe_block(mixed, gate, w_gate, w_up, w_down, layer):
    h, x, aff, afft = mixed
    B, S, D = h.shape
    E = afft.shape[1]
    cap = EC_CAPACITY * S // E
    slots = cap
    pos, post, offs, tb = select_tokens(aff, afft, cap)
    xs, gs = gather_tokens(x, post, afft, offs, slots, tb)
    M = B * slots
    y = expert_ffn(xs.reshape(E, M, D), gs.reshape(E, M, LANES), w_gate, w_up, w_down, layer, min(M, 1024), 512)
    return combine_tokens(y.reshape(E, B, slots, D), pos, offs, h, gate, tb)


def kernel(x, c, ctx, c_ctx, ada_w, ada_b, norm_mix_g, norm_ffn_g, w_in, w_out, hy_conv_w, hy_conv_b, hy_f_w1, hy_f_b1, hy_f_w2, hy_f_b2, hy_f_w3, hy_f_b3, hy_f_freq, hy_f_wout, hy_bias, q_norm_g, k_norm_g, diff_lambda, subln_g, pool_w, pool_scale, router_w, exp_w_gate, exp_w_up, exp_w_down):
    B, S, D = x.shape
    Lc = ctx.shape[1]
    depth = ada_w.shape[0]
    hy_width = hy_bias.shape[2]
    hy_proj = (HYENA_ORDER + 1) * hy_width
    qk_width = DIFF_HEADS * 2 * DIFF_HEAD_DIM
    last_attn = ((depth - 1) // 2) * 2

    s_all = jnp.concatenate([jax.nn.silu(c), jax.nn.silu(c_ctx)[None, :]], axis=0)
    rows = -(-s_all.shape[0] // SUBLANES) * SUBLANES
    mods = ada_modulation(jnp.pad(s_all, ((0, rows - s_all.shape[0]), (0, 0))), ada_w, ada_b)

    nblk_lat = max(1, S // HYENA_BLOCK)
    tables_lat = dft_tables(S // nblk_lat)
    tables_ctx = dft_tables(Lc)
    rope_lat = rope_tables(S, True)
    rope_ctx = rope_tables(Lc, False)

    h, hc = x, ctx
    for l in range(depth):
        m = [mods[l, :B, i * D:(i + 1) * D].reshape(B, 1, D) for i in range(6)]
        mc = [jnp.broadcast_to(mods[l, B, i * D:(i + 1) * D].reshape(1, 1, D), (B, 1, D)) for i in range(6)]
        ctx_full = l < last_attn
        prep = (norm_ffn_g[l], m[4], m[3], router_w[l])
        prep_c = (norm_ffn_g[l], mc[4], mc[3], router_w[l])
        if l % 2 == 0:
            e = l // 2
            lam_init = 0.8 - 0.6 * math.exp(-0.3 * l)
            lv = diff_lambda[e]
            lam = jnp.exp(jnp.sum(lv[0] * lv[1])) - jnp.exp(jnp.sum(lv[2] * lv[3])) + lam_init
            filt = (hy_f_w1[e], hy_f_b1[e], hy_f_w2[e], hy_f_b2[e], hy_f_w3[e], hy_f_b3[e], hy_f_freq[e], hy_f_wout[e])
            w_in_b = w_in[e].astype(BF16)
            w_out_b = w_out[e].astype(BF16)

            p, q, k, v = in_proj(h, norm_mix_g[l], m[1], m[0], w_in_b, hy_proj, True, *rope_lat,
                                 q_norm_g[e], k_norm_g[e], 512)
            gts = short_conv(p, hy_conv_w[e], hy_conv_b[e], 0, 2 * hy_width, 256, F32)
            hv = short_conv(p, hy_conv_w[e], hy_conv_b[e], 2 * hy_width, hy_width, 256, BF16)
            hy = hyena_operator(gts, hv, tables_lat, *hyena_filter_taps(S, *filt, hy_width), hy_bias[e], nblk_lat, 256)

            if l <= last_attn:
                if ctx_full:
                    pc, qc, kc, vc = in_proj(hc, norm_mix_g[l], mc[1], mc[0], w_in_b, hy_proj, True, *rope_ctx,
                                             q_norm_g[e], k_norm_g[e], Lc)
                else:
                    kc, vc = in_proj(hc, norm_mix_g[l], mc[1], mc[0], w_in_b[:, hy_proj + qk_width:], 0, False,
                                     *rope_ctx, q_norm_g[e], k_norm_g[e], Lc)
            o = diff_attention(lam, q, kc, vc, k, v, subln_g[e], 1.0 - lam_init, min(S, 512))
            mixed = out_proj(hy, o, w_out_b, h, m[2], prep, 512)
            if ctx_full:
                gtc = short_conv(pc, hy_conv_w[e], hy_conv_b[e], 0, 2 * hy_width, 256, F32)
                hvc = short_conv(pc, hy_conv_w[e], hy_conv_b[e], 2 * hy_width, hy_width, 256, BF16)
                hyc = hyena_operator(gtc, hvc, tables_ctx, *hyena_filter_taps(Lc, *filt, hy_width), hy_bias[e], 1, Lc)
                oc = diff_attention(lam, qc, kc, vc, None, None, subln_g[e], 1.0 - lam_init, Lc)
                mixed_c = out_proj(hyc, oc, w_out_b, hc, mc[2], prep_c, Lc)
        else:
            o_idx = l // 2
            pw = pool_w[o_idx].astype(BF16)
            mixed = pool_mixer(h, norm_mix_g[l], m[1], m[0], m[2], pw, pool_scale[o_idx], prep, 512)
            if ctx_full:
                mixed_c = pool_mixer(hc, norm_mix_g[l], mc[1], mc[0], mc[2], pw, pool_scale[o_idx], prep_c, Lc)
        h = moe_block(mixed, m[5], exp_w_gate, exp_w_up, exp_w_down, l)
        if ctx_full:
            hc = moe_block(mixed_c, mc[5], exp_w_gate, exp_w_up, exp_w_down, l)
    return h
```

```python
import functools
import math

import jax
import jax.numpy as jnp
from jax import lax
from jax.experimental import pallas as pl
from jax.experimental.pallas import tpu as pltpu

F32 = jnp.float32
BF16 = jnp.bfloat16
I32 = jnp.int32

NORM_EPS = 1e-6
GRID_W = 64
HYENA_ORDER = 2
FILTER_EMB = 33
FILTER_FAST_DECAY = 0.3
FILTER_SLOW_DECAY = 1.5
FILTER_TARGET = 1e-2
DIFF_HEADS = 4
DIFF_HEAD_DIM = 64
ROPE_BASE = 10000.0
POOL_WINDOWS = (2, 4, 8, 16)
EC_CAPACITY = 2
LANES = 128
SUBLANES = 8
POOL_HALO = 8
LOG2E = 1.4426950408889634
ATTN_KEY_CHUNK = 2048
GATHER_WINDOW = 80
COMBINE_WINDOW = 128
COMBINE_GROUPS = 4
GATHER_GROUPS = 4
BF16_ROWS = 16
HYENA_BLOCK = 1024


def _params(sem, vmem_mb):
    return pltpu.CompilerParams(dimension_semantics=sem, vmem_limit_bytes=vmem_mb * 1024 * 1024)


def _dot(a, b):
    return jnp.dot(a, b, preferred_element_type=F32)


def _norm_mod(x, g, scale, shift):
    ms = jnp.mean(x * x, axis=-1, keepdims=True)
    return (x * lax.rsqrt(ms + NORM_EPS) * g) * (1.0 + scale) + shift


def _mm_kernel(a_ref, b_ref, o_ref, acc_ref, *, nk):
    k = pl.program_id(2)
    part = _dot(a_ref[...].astype(BF16), b_ref[...].astype(BF16))

    @pl.when(k == 0)
    def _():
        acc_ref[...] = part

    @pl.when(k > 0)
    def _():
        acc_ref[...] += part

    @pl.when(k == nk - 1)
    def _():
        o_ref[...] = acc_ref[...].astype(o_ref.dtype)


def matmul(a, b, tm, tn, tk):
    M, K = a.shape
    N = b.shape[1]
    nk = K // tk
    return pl.pallas_call(
        functools.partial(_mm_kernel, nk=nk),
        grid=(M // tm, N // tn, nk),
        in_specs=[pl.BlockSpec((tm, tk), lambda i, j, k: (i, k)),
                  pl.BlockSpec((tk, tn), lambda i, j, k: (k, j))],
        out_specs=pl.BlockSpec((tm, tn), lambda i, j, k: (i, j)),
        out_shape=jax.ShapeDtypeStruct((M, N), F32),
        scratch_shapes=[pltpu.VMEM((tm, tn), F32)],
        compiler_params=_params(("parallel", "parallel", "arbitrary"), 40),
        name="matmul",
    )(a, b)


def _ada_kernel(s_ref, w_ref, b_ref, o_ref):
    o_ref[...] = _dot(s_ref[...].astype(BF16), w_ref[...].astype(BF16)) + b_ref[...]


def ada_modulation(s, ada_w, ada_b):
    depth, D, N = ada_w.shape
    R = s.shape[0]
    tn = 1024
    return pl.pallas_call(
        _ada_kernel,
        grid=(depth, N // tn),
        in_specs=[pl.BlockSpec((R, D), lambda l, j: (0, 0)),
                  pl.BlockSpec((None, D, tn), lambda l, j: (l, 0, j)),
                  pl.BlockSpec((None, 1, tn), lambda l, j: (l, 0, j))],
        out_specs=pl.BlockSpec((None, R, tn), lambda l, j: (l, 0, j)),
        out_shape=jax.ShapeDtypeStruct((depth, R, N), F32),
        compiler_params=_params(("parallel", "parallel"), 32),
        name="ada_modulation",
    )(s, ada_w, ada_b.reshape(depth, 1, N))


def _in_proj_kernel(h_ref, g_ref, sc_ref, sh_ref, w_ref, cos_ref, sin_ref, qg_ref, kg_ref, *o_refs, hy_cols, has_q, width):
    a = _norm_mod(h_ref[0], g_ref[...], sc_ref[0], sh_ref[0]).astype(BF16)
    cos = cos_ref[...]
    sin = sin_ref[...]
    outs = list(o_refs)
    hy_ref = outs.pop(0) if hy_cols else None
    q_ref = outs.pop(0) if has_q else None
    k_ref, v_ref = outs
    q_col = hy_cols
    k_col = q_col + (width if has_q else 0)
    v_col = k_col + width

    def proj(c0, n):
        return _dot(a, w_ref[:, c0:c0 + n])

    pq = proj(q_col, width) if has_q else None
    pk = proj(k_col, width)
    if hy_cols:
        hy_ref[0] = proj(0, hy_cols)
    for hd in range(width // LANES):
        sl = slice(hd * LANES, (hd + 1) * LANES)
        if has_q:
            q_ref[0, :, sl] = _head_norm_rope(pq[:, sl], qg_ref[...], cos, sin, LOG2E * DIFF_HEAD_DIM ** -0.5).astype(BF16)
        k_ref[0, :, sl] = _head_norm_rope(pk[:, sl], kg_ref[...], cos, sin, 1.0).astype(BF16)
    pv = proj(v_col, width)
    ones = jnp.ones((pv.shape[0], LANES), BF16)
    for hd in range(width // LANES):
        v_ref[0, :, 2 * hd * LANES:(2 * hd + 1) * LANES] = pv[:, hd * LANES:(hd + 1) * LANES].astype(BF16)
        v_ref[0, :, (2 * hd + 1) * LANES:(2 * hd + 2) * LANES] = ones


def in_proj(h, g, scale, shift, w, hy_cols, has_q, cos, sin_signed, q_g, k_g, tm):
    B, S, D = h.shape
    N = w.shape[1]
    width = DIFF_HEADS * 2 * DIFF_HEAD_DIM
    g2 = lambda gg: jnp.concatenate([gg, gg]).reshape(1, LANES)
    mod = pl.BlockSpec((1, 1, D), lambda b, i: (b, 0, 0))
    tab = pl.BlockSpec((tm, LANES), lambda b, i: (i, 0))
    vec = pl.BlockSpec((1, LANES), lambda b, i: (0, 0))
    widths = ([hy_cols] if hy_cols else []) + ([width] if has_q else []) + [width, 2 * width]
    dtypes = ([F32] if hy_cols else []) + ([BF16] if has_q else []) + [BF16, BF16]
    return pl.pallas_call(
        functools.partial(_in_proj_kernel, hy_cols=hy_cols, has_q=has_q, width=width),
        grid=(B, S // tm),
        in_specs=[pl.BlockSpec((1, tm, D), lambda b, i: (b, i, 0)),
                  pl.BlockSpec((1, D), lambda b, i: (0, 0)), mod, mod,
                  pl.BlockSpec((D, N), lambda b, i: (0, 0)), tab, tab, vec, vec],
        out_specs=[pl.BlockSpec((1, tm, wd), lambda b, i: (b, i, 0)) for wd in widths],
        out_shape=[jax.ShapeDtypeStruct((B, S, wd), dt) for wd, dt in zip(widths, dtypes)],
        compiler_params=_params(("parallel", "parallel"), 56),
        name="in_proj",
    )(h, g.reshape(1, D), scale, shift, w, cos, sin_signed, g2(q_g), g2(k_g))


def _sconv_kernel(p_ref, w_ref, b_ref, o_ref):
    x = p_ref[0]
    S = x.shape[0]
    row = lax.broadcasted_iota(I32, x.shape, 0)
    xm = jnp.where(row == 0, 0.0, pltpu.roll(x, 1, 0))
    xp = jnp.where(row == S - 1, 0.0, pltpu.roll(x, S - 1, 0))
    w = w_ref[...]
    o_ref[0] = (xm * w[0:1] + x * w[1:2] + xp * w[2:3] + b_ref[...]).astype(o_ref.dtype)


def short_conv(p, conv_w, conv_b, col0, width, tc, out_dtype):
    B, S, _ = p.shape
    c0 = col0 // tc
    return pl.pallas_call(
        _sconv_kernel,
        grid=(B, width // tc),
        in_specs=[pl.BlockSpec((1, S, tc), lambda b, c: (b, 0, c + c0)),
                  pl.BlockSpec((3, tc), lambda b, c: (0, c + c0)),
                  pl.BlockSpec((1, tc), lambda b, c: (0, c + c0))],
        out_specs=pl.BlockSpec((1, S, tc), lambda b, c: (b, 0, c)),
        out_shape=jax.ShapeDtypeStruct((B, S, width), out_dtype),
        compiler_params=_params(("parallel", "parallel"), 48),
        name="short_conv",
    )(p, conv_w, conv_b.reshape(1, -1))


def _dft_fwd_kernel(c_ref, s_ref, u_ref, kre_ref, kim_ref, y_ref, *, nblk):
    c = c_ref[...]
    s = s_ref[...]
    C = u_ref.shape[3]
    groups = [slice(g * LANES * 2, (g + 1) * LANES * 2) for g in range(C // (2 * LANES))] if C % (2 * LANES) == 0 else [slice(0, C)]

    def transforms(ch):
        return [(_dot(c, u_ref[0, j, :, ch]), _dot(s, u_ref[0, j, :, ch])) for j in range(nblk)]

    def products(ch, spec):
        for i in range(nblk):
            yre = yim = None
            for j in range(nblk):
                d = i - j + nblk - 1
                kre = kre_ref[:, d * C + ch.start:d * C + ch.stop]
                kim = kim_ref[:, d * C + ch.start:d * C + ch.stop]
                ure, usn = spec[j]
                tre = kre * ure + kim * usn
                tim = kim * ure - kre * usn
                yre = tre if yre is None else yre + tre
                yim = tim if yim is None else yim + tim
            y_ref[0, i, 0, :, ch] = yre.astype(BF16)
            y_ref[0, i, 1, :, ch] = yim.astype(BF16)

    spec = transforms(groups[0])
    for g, ch in enumerate(groups):
        nxt = transforms(groups[g + 1]) if g + 1 < len(groups) else None
        products(ch, spec)
        spec = nxt


def dft_forward(cf, sf, u, kre, kim, order, tf):
    B, nblk, Lb, C = u.shape
    nd = 2 * nblk - 1
    return pl.pallas_call(
        functools.partial(_dft_fwd_kernel, nblk=nblk),
        grid=(Lb // tf, B),
        in_specs=[pl.BlockSpec((tf, Lb), lambda f, b: (f, 0)),
                  pl.BlockSpec((tf, Lb), lambda f, b: (f, 0)),
                  pl.BlockSpec((1, nblk, Lb, C), lambda f, b: (b, 0, 0, 0)),
                  pl.BlockSpec((tf, nd * C), lambda f, b: (f, order)),
                  pl.BlockSpec((tf, nd * C), lambda f, b: (f, order))],
        out_specs=pl.BlockSpec((1, nblk, 2, tf, C), lambda f, b: (b, 0, 0, f, 0)),
        out_shape=jax.ShapeDtypeStruct((B, nblk, 2, Lb, C), BF16),
        compiler_params=_params(("parallel", "parallel"), 56),
        name="dft_forward",
    )(cf, sf, u, kre, kim)


def _dft_inv_kernel(ct_ref, st_ref, y_ref, gate_ref, o_ref, *, scale):
    acc = _dot(ct_ref[...], y_ref[0, 0, 0]) - _dot(st_ref[...], y_ref[0, 0, 1])
    o_ref[0, 0] = (gate_ref[0, 0] * (acc * scale)).astype(BF16)


def dft_inverse(ct, st, y, gates, gate_col):
    B, nblk, _, Lb, C = y.shape
    return pl.pallas_call(
        functools.partial(_dft_inv_kernel, scale=1.0 / Lb),
        grid=(B, nblk),
        in_specs=[pl.BlockSpec((Lb, Lb), lambda b, i: (0, 0)),
                  pl.BlockSpec((Lb, Lb), lambda b, i: (0, 0)),
                  pl.BlockSpec((1, 1, 2, Lb, C), lambda b, i: (b, i, 0, 0, 0)),
                  pl.BlockSpec((1, 1, Lb, C), lambda b, i: (b, i, 0, gate_col))],
        out_specs=pl.BlockSpec((1, 1, Lb, C), lambda b, i: (b, i, 0, 0)),
        out_shape=jax.ShapeDtypeStruct((B, nblk, Lb, C), BF16),
        compiler_params=_params(("parallel", "parallel"), 48),
        name="dft_inverse",
    )(ct, st, y, gates)


def dft_tables(L):
    n = 2 * L
    f = lax.broadcasted_iota(I32, (L, L), 0)
    t = lax.broadcasted_iota(I32, (L, L), 1)
    m = ((2 * f + 1) * t) % (2 * n)
    ang = m.astype(F32) * (math.pi / n)
    cf = jnp.cos(ang)
    sf = jnp.sin(ang)
    return cf.astype(BF16), sf.astype(BF16), cf.T.astype(BF16), sf.T.astype(BF16)


def hyena_filter_taps(L, w1, b1, w2, b2, w3, b3, freq, wout, width):
    hp = lax.Precision.HIGHEST
    t = jnp.linspace(0.0, 1.0, L, dtype=F32)[:, None]
    bands = (FILTER_EMB - 1) // 2
    w = 2.0 * math.pi * jnp.arange(L, dtype=F32)[:, None] / L
    f = jnp.linspace(1e-4, bands - 1, bands, dtype=F32)[None, :]
    z = jnp.concatenate([t, jnp.cos(f * w), -jnp.sin(f * w)], axis=-1)
    h = jnp.sin(freq * (jnp.dot(z, w1, precision=hp) + b1))
    h = jnp.sin(freq * (jnp.dot(h, w2, precision=hp) + b2))
    h = jnp.sin(freq * (jnp.dot(h, w3, precision=hp) + b3))
    max_decay = math.log(FILTER_TARGET) / FILTER_FAST_DECAY
    min_decay = math.log(FILTER_TARGET) / FILTER_SLOW_DECAY
    deltas = jnp.abs(jnp.linspace(min_decay, max_decay, width, dtype=F32))

    def taps(hh, tt):
        return jnp.dot(hh, wout, precision=hp).reshape(L, HYENA_ORDER, 2, width) * jnp.exp(-tt * deltas[None, :])[:, None, None, :]

    return taps(h, t), taps(h[::-1], t[::-1])


def filter_spectra(cf, sf, taps, taps_rev, bias, nblk):
    L, C = taps.shape[0], taps.shape[3]
    Lb = L // nblk
    zero = jnp.zeros((1, C), F32)
    sums, diffs = [], []
    for o in range(HYENA_ORDER):
        h_fwd, h_bwd = taps[:, o, 0], taps[:, o, 1]
        h_fwd_rev, h_bwd_rev = taps_rev[:, o, 0], taps_rev[:, o, 1]
        k0 = (h_fwd[0] + h_bwd[0] + bias[o])[None, :]
        kfull = jnp.concatenate([zero, h_bwd_rev[:L - 1], k0, h_fwd[1:]], axis=0)
        krev = jnp.concatenate([h_fwd_rev[:L - 1], k0, h_bwd[1:], zero], axis=0)
        for dlt in range(-(nblk - 1), nblk):
            base = L + dlt * Lb
            kp = kfull[base:base + Lb]
            km = jnp.concatenate([zero, krev[2 * L - base:2 * L - base + Lb - 1]], axis=0)
            sums.append(kp + km)
            diffs.append(km - kp)
    tmm = min(Lb, 512)
    kre = matmul(cf, jnp.concatenate(sums, axis=1), tmm, tmm, tmm)
    kim = matmul(sf, jnp.concatenate(diffs, axis=1), tmm, tmm, tmm)
    return kre, kim


def hyena_operator(gates, v, tables, taps, taps_rev, bias, nblk, tf):
    cf, sf, ct, st = tables
    B, L, C = v.shape
    Lb = L // nblk
    kre, kim = filter_spectra(cf, sf, taps, taps_rev, bias, nblk)
    gates = gates.reshape(B, nblk, Lb, 2 * C)
    z = v.reshape(B, nblk, Lb, C)
    for o in range(HYENA_ORDER):
        y = dft_forward(cf, sf, z, kre, kim, o, tf)
        z = dft_inverse(ct, st, y, gates, o)
    return z.reshape(B, L, C)


def _head_norm_rope(x, g, cos, sin_signed, scale):
    lane = lax.broadcasted_iota(I32, x.shape, 1)
    lo = lane < DIFF_HEAD_DIM
    x2 = x * x
    s_lo = jnp.sum(jnp.where(lo, x2, 0.0), axis=-1, keepdims=True)
    s_hi = jnp.sum(jnp.where(lo, 0.0, x2), axis=-1, keepdims=True)
    ms = jnp.where(lo, s_lo, s_hi) * (1.0 / DIFF_HEAD_DIM)
    xn = x * lax.rsqrt(ms + NORM_EPS) * g
    first = (lane & 16) == 0
    partner = jnp.where(first, pltpu.roll(xn, LANES - 16, 1), pltpu.roll(xn, 16, 1))
    return (xn * cos + partner * sin_signed) * scale


def rope_tables(S, use_rope):
    if not use_rope:
        return jnp.ones((S, LANES), F32), jnp.zeros((S, LANES), F32)
    t = jnp.arange(S, dtype=I32)
    row = (t // GRID_W).astype(F32)[:, None]
    colp = (t % GRID_W).astype(F32)[:, None]
    nf = DIFF_HEAD_DIM // 4
    inv = ROPE_BASE ** (-jnp.arange(nf, dtype=F32) / nf)
    lane = jnp.arange(LANES)
    grp = (lane % DIFF_HEAD_DIM) // nf
    j = lane % nf
    pos = jnp.where((grp < 2)[None, :], row, colp)
    ang = pos * inv[j][None, :]
    sign = jnp.where((grp % 2 == 0)[None, :], -1.0, 1.0)
    return jnp.cos(ang), jnp.sin(ang) * sign


def _attn_kernel(lam_ref, q_ref, kc_ref, vc_ref, *rest, has_lat, out_scale, ck):
    if has_lat:
        kl_ref, vl_ref, g_ref, o_ref = rest
    else:
        g_ref, o_ref = rest
    lam = lam_ref[0]
    q = q_ref[0]
    lane = lax.broadcasted_iota(I32, q.shape, 1)
    nt = (((1,), (1,)), ((), ()))
    zero = jnp.zeros_like(q)
    qm = [jnp.where(lane < DIFF_HEAD_DIM, q, zero), jnp.where(lane >= DIFF_HEAD_DIM, q, zero)]
    chunks = [(kc_ref, vc_ref, 0, kc_ref.shape[1])]
    if has_lat:
        chunks += [(kl_ref, vl_ref, c * ck, ck) for c in range(kl_ref.shape[1] // ck)]

    def scores(ch):
        k = ch[0][0, ch[2]:ch[2] + ch[3], :]
        return [lax.dot_general(qm[mp], k, nt, preferred_element_type=F32) for mp in range(2)]

    m, acc = [None, None], [None, None]
    s_next = scores(chunks[0])
    for ci, ch in enumerate(chunks):
        s_cur = s_next
        if ci + 1 < len(chunks):
            s_next = scores(chunks[ci + 1])
        v1 = ch[1][0, ch[2]:ch[2] + ch[3], :]
        for mp in range(2):
            s = s_cur[mp]
            mx = jnp.max(s, axis=-1, keepdims=True)
            m_new = mx if ci == 0 else jnp.maximum(m[mp], mx)
            pv = _dot(jnp.exp2(s - m_new).astype(BF16), v1)
            acc[mp] = pv if ci == 0 else jnp.exp2(m[mp] - m_new) * acc[mp] + pv
            m[mp] = m_new
    o = acc[0][:, :LANES] / acc[0][:, LANES:] - lam * (acc[1][:, :LANES] / acc[1][:, LANES:])
    ms = jnp.mean(o * o, axis=-1, keepdims=True)
    o_ref[0] = ((o * lax.rsqrt(ms + NORM_EPS) * g_ref[...]) * out_scale).astype(BF16)


def diff_attention(lam, q, k_ctx, v_ctx, k_lat, v_lat, subln_g, out_scale, tq):
    B, Sq, W = q.shape
    H = W // LANES
    has_lat = k_lat is not None
    Sc = k_ctx.shape[1]
    head = lambda S, w: pl.BlockSpec((1, S, w), lambda b, h, i: (b, 0, h))
    in_specs = [pl.BlockSpec(memory_space=pltpu.SMEM),
                pl.BlockSpec((1, tq, LANES), lambda b, h, i: (b, i, h)), head(Sc, LANES), head(Sc, 2 * LANES)]
    args = [lam.reshape(1), q, k_ctx, v_ctx]
    if has_lat:
        in_specs += [head(k_lat.shape[1], LANES), head(k_lat.shape[1], 2 * LANES)]
        args += [k_lat, v_lat]
    in_specs.append(pl.BlockSpec((1, LANES), lambda b, h, i: (0, 0)))
    args.append(subln_g.reshape(1, LANES))
    return pl.pallas_call(
        functools.partial(_attn_kernel, has_lat=has_lat, out_scale=out_scale,
                          ck=min(ATTN_KEY_CHUNK, k_lat.shape[1]) if has_lat else 0),
        grid=(B, H, Sq // tq),
        in_specs=in_specs,
        out_specs=pl.BlockSpec((1, tq, LANES), lambda b, h, i: (b, i, h)),
        out_shape=jax.ShapeDtypeStruct((B, Sq, W), BF16),
        compiler_params=_params(("parallel", "parallel", "parallel"), 56),
        name="diff_attention",
    )(*args)


def _oproj_kernel(hy_ref, o_ref, w_ref, h_ref, gate_ref, *rest, half, n_exp):
    prep_in, (out_ref, *prep_out) = rest[:4], rest[4:]
    y = _dot(hy_ref[0].astype(BF16), w_ref[0:half, :]) + _dot(o_ref[0], w_ref[half:, :])
    h_new = h_ref[0] + gate_ref[0] * y
    out_ref[0] = h_new
    _ffn_prep_store(h_new, *prep_in, *prep_out, n_exp)


def out_proj(hy, o, w_out, h, gate, prep, tm):
    B, S, D = h.shape
    half = hy.shape[2]
    n_exp = prep[3].shape[1]
    p_in, p_out, p_shape = _ffn_prep_specs(B, S, D, tm, n_exp)
    return pl.pallas_call(
        functools.partial(_oproj_kernel, half=half, n_exp=n_exp),
        grid=(B, S // tm),
        in_specs=[pl.BlockSpec((1, tm, half), lambda b, i: (b, i, 0)),
                  pl.BlockSpec((1, tm, half), lambda b, i: (b, i, 0)),
                  pl.BlockSpec((2 * half, D), lambda b, i: (0, 0)),
                  pl.BlockSpec((1, tm, D), lambda b, i: (b, i, 0)),
                  pl.BlockSpec((1, 1, D), lambda b, i: (b, 0, 0))] + p_in,
        out_specs=[pl.BlockSpec((1, tm, D), lambda b, i: (b, i, 0))] + p_out,
        out_shape=[jax.ShapeDtypeStruct((B, S, D), F32)] + p_shape,
        compiler_params=_params(("parallel", "parallel"), 40),
        name="out_proj",
    )(hy, o, w_out, h, gate, *_ffn_prep_args(prep))


def _shift_rows(x, d):
    return pltpu.roll(x, (-d) % x.shape[0], 0)


def _pool_kernel(hp_ref, hc_ref, hn_ref, g_ref, sc_ref, sh_ref, gate_ref, pw_ref, ps_ref, *rest, T, L, n_exp):
    prep_in, (o_ref, *prep_out) = rest[:4], rest[4:]
    i = pl.program_id(1)
    nt = pl.num_programs(1)
    g, sc, sh = g_ref[...], sc_ref[0], sh_ref[0]
    hc = hc_ref[0]
    a_c = _norm_mod(hc, g, sc, sh)
    a_p = jnp.where(i == 0, 0.0, _norm_mod(hp_ref[0], g, sc, sh))
    a_n = jnp.where(i == nt - 1, 0.0, _norm_mod(hn_ref[0], g, sc, sh))
    ext = jnp.concatenate([a_p, a_c, a_n], axis=0)
    tok = i * T + lax.broadcasted_iota(I32, (T, 1), 0)
    G = ext.shape[1] // len(POOL_WINDOWS)
    ys = []
    for gi, w in enumerate(POOL_WINDOWS):
        xg = ext[:, gi * G:(gi + 1) * G]
        s = _shift_rows(xg, -1) + xg
        step = 1
        while 2 * step < w:
            s = _shift_rows(s, -step) + _shift_rows(s, step)
            step *= 2
        cnt = (jnp.minimum(tok + w // 2, L) - jnp.maximum(tok - w // 2, 0)).astype(F32)
        p = s[POOL_HALO:POOL_HALO + T] / cnt - a_c[:, gi * G:(gi + 1) * G]
        ys.append(_dot(p.astype(BF16), pw_ref[gi]))
    y = jnp.concatenate(ys, axis=1) * ps_ref[...]
    h_new = hc + gate_ref[0] * y
    o_ref[0] = h_new
    _ffn_prep_store(h_new, *prep_in, *prep_out, n_exp)


def pool_mixer(h, g, scale, shift, gate, pool_w, pool_scale, prep, T):
    B, S, D = h.shape
    nh = T // POOL_HALO
    last = S // POOL_HALO - 1
    n_exp = prep[3].shape[1]
    p_in, p_out, p_shape = _ffn_prep_specs(B, S, D, T, n_exp)
    mod = pl.BlockSpec((1, 1, D), lambda b, i: (b, 0, 0))
    return pl.pallas_call(
        functools.partial(_pool_kernel, T=T, L=S, n_exp=n_exp),
        grid=(B, S // T),
        in_specs=[pl.BlockSpec((1, POOL_HALO, D), lambda b, i: (b, jnp.maximum(i * nh - 1, 0), 0)),
                  pl.BlockSpec((1, T, D), lambda b, i: (b, i, 0)),
                  pl.BlockSpec((1, POOL_HALO, D), lambda b, i: (b, jnp.minimum((i + 1) * nh, last), 0)),
                  pl.BlockSpec((1, D), lambda b, i: (0, 0)), mod, mod, mod,
                  pl.BlockSpec(pool_w.shape, lambda b, i: (0, 0, 0)),
                  pl.BlockSpec((1, D), lambda b, i: (0, 0))] + p_in,
        out_specs=[pl.BlockSpec((1, T, D), lambda b, i: (b, i, 0))] + p_out,
        out_shape=[jax.ShapeDtypeStruct((B, S, D), F32)] + p_shape,
        compiler_params=_params(("parallel", "parallel"), 48),
        name="pool_mixer",
    )(h, h, h, g.reshape(1, D), scale, shift, gate, pool_w, pool_scale.reshape(1, D), *_ffn_prep_args(prep))


def _ffn_prep_store(h_new, g_ref, sc_ref, sh_ref, rw_ref, x_ref, aff_ref, afft_ref, n_exp):
    a = _norm_mod(h_new, g_ref[...], sc_ref[0], sh_ref[0]).astype(BF16)
    x_ref[0] = a
    logits = _dot(a, rw_ref[...])
    lane = lax.broadcasted_iota(I32, logits.shape, 1)
    valid = lane < n_exp
    mx = jnp.max(jnp.where(valid, logits, -jnp.inf), axis=-1, keepdims=True)
    e = jnp.where(valid, jnp.exp(logits - mx), 0.0)
    aff = e / jnp.sum(e, axis=-1, keepdims=True)
    aff_ref[0] = aff
    afft_ref[0] = aff.T[0:n_exp]


def _ffn_prep_specs(B, S, D, tm, n_exp):
    mod = pl.BlockSpec((1, 1, D), lambda b, i: (b, 0, 0))
    ins = [pl.BlockSpec((1, D), lambda b, i: (0, 0)), mod, mod, pl.BlockSpec((D, LANES), lambda b, i: (0, 0))]
    outs = [pl.BlockSpec((1, tm, D), lambda b, i: (b, i, 0)),
            pl.BlockSpec((1, tm, LANES), lambda b, i: (b, i, 0)),
            pl.BlockSpec((1, n_exp, tm), lambda b, i: (b, 0, i))]
    shapes = [jax.ShapeDtypeStruct((B, S, D), BF16), jax.ShapeDtypeStruct((B, S, LANES), F32),
              jax.ShapeDtypeStruct((B, n_exp, S), F32)]
    return ins, outs, shapes


def _ffn_prep_args(prep):
    g, scale, shift, router_w = prep
    rw = jnp.pad(router_w, ((0, 0), (0, LANES - router_w.shape[1]))).astype(BF16)
    return [g.reshape(1, -1), scale, shift, rw]


def _prefix_excl(m, tri, tb):
    S = m.shape[0]
    carry = jnp.zeros((1, m.shape[1]), F32)
    outs, carries = [], []
    for blk in range(S // tb):
        mb = m[blk * tb:(blk + 1) * tb]
        outs.append(_dot(tri, mb.astype(BF16)) + carry)
        carries.append(carry)
        carry = carry + jnp.sum(mb, axis=0, keepdims=True)
    carries.append(carry)
    return (jnp.concatenate(outs, axis=0) if len(outs) > 1 else outs[0]), jnp.concatenate(carries, axis=0)


def _select_kernel(aff_ref, afft_ref, tri_ref, pos_ref, post_ref, offs_ref, *, cap, tb, n_exp):
    bits = pltpu.bitcast(afft_ref[0], I32)

    def body(i, cur):
        cand = cur | (jnp.int32(1) << (30 - i))
        cnt = jnp.sum((bits >= cand).astype(F32), axis=1, keepdims=True)
        return jnp.where(cnt >= cap, cand, cur)

    thr_bits = lax.fori_loop(0, 31, body, jnp.zeros((n_exp, 1), I32))
    thr_col = pltpu.bitcast(jnp.broadcast_to(thr_bits, (n_exp, LANES)), F32)
    sub = lax.broadcasted_iota(I32, (n_exp, LANES), 0)
    lane = lax.broadcasted_iota(I32, (n_exp, LANES), 1)
    thr = jnp.sum(jnp.where(sub == lane, thr_col, 0.0), axis=0, keepdims=True)
    aff = aff_ref[0]
    tri = tri_ref[...]
    gt = (aff > thr).astype(F32)
    eq = (aff == thr).astype(F32)
    need = cap - jnp.sum(gt, axis=0, keepdims=True)
    sel = gt + eq * (_prefix_excl(eq, tri, tb)[0] < need).astype(F32)
    slot, offs = _prefix_excl(sel, tri, tb)
    sp = jnp.where(sel > 0.0, slot, -1.0)
    pos_ref[0] = sp
    offs_ref[0] = offs
    for blk in range(sp.shape[0] // tb):
        post_ref[0, :, blk * tb:(blk + 1) * tb] = sp[blk * tb:(blk + 1) * tb].T[0:n_exp]


def select_tokens(aff, afft, cap):
    B, S, _ = aff.shape
    n_exp = afft.shape[1]
    tb = min(S, 256)
    r = lax.broadcasted_iota(I32, (tb, tb), 0)
    c = lax.broadcasted_iota(I32, (tb, tb), 1)
    tri = (c < r).astype(BF16)
    nblk = S // tb
    pos, post, offs = pl.pallas_call(
        functools.partial(_select_kernel, cap=cap, tb=tb, n_exp=n_exp),
        grid=(B,),
        in_specs=[pl.BlockSpec((1, S, LANES), lambda b: (b, 0, 0)),
                  pl.BlockSpec((1, n_exp, S), lambda b: (b, 0, 0)),
                  pl.BlockSpec((tb, tb), lambda b: (0, 0))],
        out_specs=[pl.BlockSpec((1, S, LANES), lambda b: (b, 0, 0)),
                   pl.BlockSpec((1, n_exp, S), lambda b: (b, 0, 0)),
                   pl.BlockSpec((1, nblk + 1, LANES), lambda b: (b, 0, 0))],
        out_shape=[jax.ShapeDtypeStruct((B, S, LANES), F32), jax.ShapeDtypeStruct((B, n_exp, S), F32),
                   jax.ShapeDtypeStruct((B, nblk + 1, LANES), F32)],
        compiler_params=_params(("parallel",), 48),
        name="select_tokens",
    )(aff, afft, tri)
    offs = jnp.swapaxes(offs[:, :, :n_exp], 1, 2).astype(I32).reshape(-1)
    return pos, post, offs, tb


def _gather_kernel(offs_ref, x_ref, post_ref, afft_ref, o_ref, g_ref, *, nblk, n_exp, win):
    b = pl.program_id(0)
    t = pl.program_id(1)
    slots = o_ref.shape[2]

    @pl.when(t == 0)
    def _():
        o_ref[...] = jnp.zeros_like(o_ref)
        g_ref[...] = jnp.zeros_like(g_ref)

    x = x_ref[0]
    T = x.shape[0]

    def full():
        r = lax.broadcasted_iota(I32, (slots, T), 0).astype(F32)
        for e in range(n_exp):
            hit = r == post_ref[0, e:e + 1, :]
            o_ref[e, 0] = (o_ref[e, 0].astype(F32) + _dot(hit.astype(BF16), x)).astype(BF16)
            g = jnp.sum(jnp.where(hit, afft_ref[0, e:e + 1, :], 0.0), axis=1, keepdims=True)
            g_ref[e, 0] = g_ref[e, 0] + jnp.broadcast_to(g, (slots, LANES))

    if win >= slots:
        full()
        return
    starts = []
    narrow = None
    for e in range(n_exp):
        base = (b * n_exp + e) * (nblk + 1) + t
        lo = offs_ref[base]
        hi = offs_ref[base + 1]
        a0 = pl.multiple_of(jnp.minimum((lo // BF16_ROWS) * BF16_ROWS, slots - win), BF16_ROWS)
        ok = hi - a0 <= win
        narrow = ok if narrow is None else jnp.logical_and(narrow, ok)
        starts.append(a0)

    @pl.when(narrow)
    def _():
        r = lax.broadcasted_iota(I32, (win, T), 0).astype(F32)
        hits = [r + starts[e].astype(F32) == post_ref[0, e:e + 1, :] for e in range(n_exp)]
        per = n_exp // GATHER_GROUPS
        rows = [_dot(jnp.concatenate([hits[e].astype(BF16) for e in range(gi * per, (gi + 1) * per)], axis=0), x)
                for gi in range(GATHER_GROUPS)]
        for gi in range(GATHER_GROUPS):
            members = range(gi * per, (gi + 1) * per)
            old = [o_ref[e, 0, pl.ds(starts[e], win), :] for e in members]
            old_g = [g_ref[e, 0, pl.ds(starts[e], win), :] for e in members]
            for k, e in enumerate(members):
                o_ref[e, 0, pl.ds(starts[e], win), :] = (old[k].astype(F32) + rows[gi][k * win:(k + 1) * win]).astype(BF16)
                g = jnp.sum(jnp.where(hits[e], afft_ref[0, e:e + 1, :], 0.0), axis=1, keepdims=True)
                g_ref[e, 0, pl.ds(starts[e], win), :] = old_g[k] + jnp.broadcast_to(g, (win, LANES))

    pl.when(jnp.logical_not(narrow))(full)


def gather_tokens(x, post, afft, offs, slots, T):
    B, S, D = x.shape
    E = post.shape[1]
    row = pl.BlockSpec((1, E, T), lambda b, t, offs: (b, 0, t))
    return pl.pallas_call(
        functools.partial(_gather_kernel, nblk=S // T, n_exp=E, win=min(slots, GATHER_WINDOW)),
        grid_spec=pltpu.PrefetchScalarGridSpec(
            num_scalar_prefetch=1,
            grid=(B, S // T),
            in_specs=[pl.BlockSpec((1, T, D), lambda b, t, offs: (b, t, 0)), row, row],
            out_specs=[pl.BlockSpec((E, 1, slots, D), lambda b, t, offs: (0, b, 0, 0)),
                       pl.BlockSpec((E, 1, slots, LANES), lambda b, t, offs: (0, b, 0, 0))]),
        out_shape=[jax.ShapeDtypeStruct((E, B, slots, D), BF16), jax.ShapeDtypeStruct((E, B, slots, LANES), F32)],
        compiler_params=_params(("parallel", "arbitrary"), 56),
        name="gather_tokens",
    )(offs, x, post, afft)


def _ffn_kernel(xs_ref, g_ref, wg_ref, wu_ref, wd_ref, o_ref, hm_ref, *, nf):
    f = pl.program_id(2)
    x = xs_ref[0]
    a = _dot(x, wg_ref[...].astype(BF16))
    u = _dot(x, wu_ref[...].astype(BF16))
    hm_ref[f] = (a * jax.nn.sigmoid(a) * u).astype(BF16)

    @pl.when(f == nf - 1)
    def _():
        hm = jnp.concatenate([hm_ref[c] for c in range(nf)], axis=1) if nf > 1 else hm_ref[0]
        o_ref[0] = (_dot(hm, wd_ref[...].astype(BF16)) * g_ref[0, :, 0:1]).astype(BF16)


def expert_ffn(xs, gs, w_gate, w_up, w_down, layer, tm, tf):
    E, M, D = xs.shape
    F = w_gate.shape[3]
    nf = F // tf
    return pl.pallas_call(
        functools.partial(_ffn_kernel, nf=nf),
        grid=(E, M // tm, nf),
        in_specs=[pl.BlockSpec((1, tm, D), lambda e, m, f: (e, m, 0)),
                  pl.BlockSpec((1, tm, LANES), lambda e, m, f: (e, m, 0)),
                  pl.BlockSpec((None, None, D, tf), lambda e, m, f: (layer, e, 0, f)),
                  pl.BlockSpec((None, None, D, tf), lambda e, m, f: (layer, e, 0, f)),
                  pl.BlockSpec((None, None, F, D), lambda e, m, f: (layer, e, 0, 0))],
        out_specs=pl.BlockSpec((1, tm, D), lambda e, m, f: (e, m, 0)),
        out_shape=jax.ShapeDtypeStruct((E, M, D), BF16),
        scratch_shapes=[pltpu.VMEM((nf, tm, tf), BF16)],
        compiler_params=_params(("parallel", "parallel", "arbitrary"), 56),
        name="expert_ffn",
    )(xs, gs, w_gate, w_up, w_down)


def _combine_kernel(offs_ref, y_ref, pos_ref, h_ref, gate_ref, o_ref, *ybuf_refs, nblk, n_exp, win):
    b = pl.program_id(0)
    t = pl.program_id(1)
    T = pos_ref.shape[1]
    slots = y_ref.shape[2]

    def full():
        if slots < LANES:
            K = n_exp * slots
            col = lax.broadcasted_iota(I32, (LANES, K), 1).astype(F32)
            owner = jnp.floor((col + 0.5) * (1.0 / slots))
            rep = owner == lax.broadcasted_iota(I32, (LANES, K), 0).astype(F32)
            slot_rep = _dot(pos_ref[0].astype(BF16), rep.astype(BF16))
            want = (col - owner * slots)[0:1]
            acc = _dot((slot_rep == want).astype(BF16), y_ref[:, 0].reshape(K, y_ref.shape[3]))
        else:
            r = lax.broadcasted_iota(I32, (T, slots), 1).astype(F32)
            acc = None
            for e in range(n_exp):
                part = _dot((pos_ref[0, :, e:e + 1] == r).astype(BF16), y_ref[e, 0])
                acc = part if acc is None else acc + part
        o_ref[0] = h_ref[0] + gate_ref[0] * acc

    if win >= slots:
        full()
        return
    starts = []
    narrow = None
    for e in range(n_exp):
        base = (b * n_exp + e) * (nblk + 1) + t
        lo = offs_ref[base]
        hi = offs_ref[base + 1]
        a0 = jnp.minimum((lo // BF16_ROWS) * BF16_ROWS, slots - win)
        ok = hi - a0 <= win
        narrow = ok if narrow is None else jnp.logical_and(narrow, ok)
        starts.append(a0)

    @pl.when(narrow)
    def _():
        r = lax.broadcasted_iota(I32, (T, win), 1).astype(F32)
        acc = None
        per = n_exp // len(ybuf_refs)
        for gi, ybuf_ref in enumerate(ybuf_refs):
            pieces = []
            for k, e in enumerate(range(gi * per, (gi + 1) * per)):
                a0 = starts[e]
                ybuf_ref[k * win:(k + 1) * win, :] = y_ref[e, 0, pl.ds(pl.multiple_of(a0, BF16_ROWS), win), :]
                pieces.append((pos_ref[0, :, e:e + 1] - a0.astype(F32) == r).astype(BF16))
            part = _dot(jnp.concatenate(pieces, axis=1), ybuf_ref[...])
            acc = part if acc is None else acc + part
        o_ref[0] = h_ref[0] + gate_ref[0] * acc

    pl.when(jnp.logical_not(narrow))(full)


def combine_tokens(y, pos, offs, h, gate, T):
    E, B, slots, D = y.shape
    S = h.shape[1]
    win = min(slots, COMBINE_WINDOW)
    return pl.pallas_call(
        functools.partial(_combine_kernel, nblk=S // T, n_exp=E, win=win),
        grid_spec=pltpu.PrefetchScalarGridSpec(
            num_scalar_prefetch=1,
            grid=(B, S // T),
            in_specs=[pl.BlockSpec((E, 1, slots, D), lambda b, t, offs: (0, b, 0, 0)),
                      pl.BlockSpec((1, T, LANES), lambda b, t, offs: (b, t, 0)),
                      pl.BlockSpec((1, T, D), lambda b, t, offs: (b, t, 0)),
                      pl.BlockSpec((1, 1, D), lambda b, t, offs: (b, 0, 0))],
            out_specs=pl.BlockSpec((1, T, D), lambda b, t, offs: (b, t, 0)),
            scratch_shapes=[pltpu.VMEM((E // COMBINE_GROUPS * win, D), BF16)] * COMBINE_GROUPS),
        out_shape=jax.ShapeDtypeStruct((B, S, D), F32),
        compiler_params=_params(("parallel", "parallel"), 56),
        name="combine_tokens",
    )(offs, y, pos, h, gate)


def moe_block(mixed, gate, w_gate, w_up, w_down, layer):
    h, x, aff, afft = mixed
    B, S, D = h.shape
    E = afft.shape[1]
    cap = EC_CAPACITY * S // E
    slots = cap
    pos, post, offs, tb = select_tokens(aff, afft, cap)
    xs, gs = gather_tokens(x, post, afft, offs, slots, tb)
    M = B * slots
    y = expert_ffn(xs.reshape(E, M, D), gs.reshape(E, M, LANES), w_gate, w_up, w_down, layer, min(M, 1024), 512)
    return combine_tokens(y.reshape(E, B, slots, D), pos, offs, h, gate, tb)


def kernel(x, c, ctx, c_ctx, ada_w, ada_b, norm_mix_g, norm_ffn_g, w_in, w_out, hy_conv_w, hy_conv_b, hy_f_w1, hy_f_b1, hy_f_w2, hy_f_b2, hy_f_w3, hy_f_b3, hy_f_freq, hy_f_wout, hy_bias, q_norm_g, k_norm_g, diff_lambda, subln_g, pool_w, pool_scale, router_w, exp_w_gate, exp_w_up, exp_w_down):
    B, S, D = x.shape
    Lc = ctx.shape[1]
    depth = ada_w.shape[0]
    hy_width = hy_bias.shape[2]
    hy_proj = (HYENA_ORDER + 1) * hy_width
    qk_width = DIFF_HEADS * 2 * DIFF_HEAD_DIM
    last_attn = ((depth - 1) // 2) * 2

    s_all = jnp.concatenate([jax.nn.silu(c), jax.nn.silu(c_ctx)[None, :]], axis=0)
    rows = -(-s_all.shape[0] // SUBLANES) * SUBLANES
    mods = ada_modulation(jnp.pad(s_all, ((0, rows - s_all.shape[0]), (0, 0))), ada_w, ada_b)

    nblk_lat = max(1, S // HYENA_BLOCK)
    tables_lat = dft_tables(S // nblk_lat)
    tables_ctx = dft_tables(Lc)
    rope_lat = rope_tables(S, True)
    rope_ctx = rope_tables(Lc, False)

    h, hc = x, ctx
    for l in range(depth):
        m = [mods[l, :B, i * D:(i + 1) * D].reshape(B, 1, D) for i in range(6)]
        mc = [jnp.broadcast_to(mods[l, B, i * D:(i + 1) * D].reshape(1, 1, D), (B, 1, D)) for i in range(6)]
        ctx_full = l < last_attn
        prep = (norm_ffn_g[l], m[4], m[3], router_w[l])
        prep_c = (norm_ffn_g[l], mc[4], mc[3], router_w[l])
        if l % 2 == 0:
            e = l // 2
            lam_init = 0.8 - 0.6 * math.exp(-0.3 * l)
            lv = diff_lambda[e]
            lam = jnp.exp(jnp.sum(lv[0] * lv[1])) - jnp.exp(jnp.sum(lv[2] * lv[3])) + lam_init
            filt = (hy_f_w1[e], hy_f_b1[e], hy_f_w2[e], hy_f_b2[e], hy_f_w3[e], hy_f_b3[e], hy_f_freq[e], hy_f_wout[e])
            w_in_b = w_in[e].astype(BF16)
            w_out_b = w_out[e].astype(BF16)

            p, q, k, v = in_proj(h, norm_mix_g[l], m[1], m[0], w_in_b, hy_proj, True, *rope_lat,
                                 q_norm_g[e], k_norm_g[e], 512)
            gts = short_conv(p, hy_conv_w[e], hy_conv_b[e], 0, 2 * hy_width, 256, F32)
            hv = short_conv(p, hy_conv_w[e], hy_conv_b[e], 2 * hy_width, hy_width, 256, BF16)
            hy = hyena_operator(gts, hv, tables_lat, *hyena_filter_taps(S, *filt, hy_width), hy_bias[e], nblk_lat, 256)

            if l <= last_attn:
                if ctx_full:
                    pc, qc, kc, vc = in_proj(hc, norm_mix_g[l], mc[1], mc[0], w_in_b, hy_proj, True, *rope_ctx,
                                             q_norm_g[e], k_norm_g[e], Lc)
                else:
                    kc, vc = in_proj(hc, norm_mix_g[l], mc[1], mc[0], w_in_b[:, hy_proj + qk_width:], 0, False,
                                     *rope_ctx, q_norm_g[e], k_norm_g[e], Lc)
            o = diff_attention(lam, q, kc, vc, k, v, subln_g[e], 1.0 - lam_init, min(S, 512))
            mixed = out_proj(hy, o, w_out_b, h, m[2], prep, 512)
            if ctx_full:
                gtc = short_conv(pc, hy_conv_w[e], hy_conv_b[e], 0, 2 * hy_width, 256, F32)
                hvc = short_conv(pc, hy_conv_w[e], hy_conv_b[e], 2 * hy_width, hy_width, 256, BF16)
                hyc = hyena_operator(gtc, hvc, tables_ctx, *hyena_filter_taps(Lc, *filt, hy_width), hy_bias[e], 1, Lc)
                oc = diff_attention(lam, qc, kc, vc, None, None, subln_g[e], 1.0 - lam_init, Lc)
                mixed_c = out_proj(hyc, oc, w_out_b, hc, mc[2], prep_c, Lc)
        else:
            o_idx = l // 2
            pw = pool_w[o_idx].astype(BF16)
            mixed = pool_mixer(h, norm_mix_g[l], m[1], m[0], m[2], pw, pool_scale[o_idx], prep, 512)
            if ctx_full:
                mixed_c = pool_mixer(hc, norm_mix_g[l], mc[1], mc[0], mc[2], pw, pool_scale[o_idx], prep_c, Lc)
        h = moe_block(mixed, m[5], exp_w_gate, exp_w_up, exp_w_down, l)
        if ctx_full:
            hc = moe_block(mixed_c, mc[5], exp_w_gate, exp_w_up, exp_w_down, l)
    return h
```

```python
import functools
import math

import jax
import jax.numpy as jnp
from jax import lax
from jax.experimental import pallas as pl
from jax.experimental.pallas import tpu as pltpu

F32 = jnp.float32
BF16 = jnp.bfloat16
I32 = jnp.int32

NORM_EPS = 1e-6
GRID_W = 64
HYENA_ORDER = 2
FILTER_EMB = 33
FILTER_FAST_DECAY = 0.3
FILTER_SLOW_DECAY = 1.5
FILTER_TARGET = 1e-2
DIFF_HEADS = 4
DIFF_HEAD_DIM = 64
ROPE_BASE = 10000.0
POOL_WINDOWS = (2, 4, 8, 16)
EC_CAPACITY = 2
LANES = 128
SUBLANES = 8
POOL_HALO = 8
LOG2E = 1.4426950408889634
ATTN_KEY_CHUNK = 2048
GATHER_WINDOW = 80
COMBINE_WINDOW = 128
COMBINE_GROUPS = 4
GATHER_GROUPS = 4
BF16_ROWS = 16
HYENA_BLOCK = 1024


def _params(sem, vmem_mb):
    return pltpu.CompilerParams(dimension_semantics=sem, vmem_limit_bytes=vmem_mb * 1024 * 1024)


def _dot(a, b):
    return jnp.dot(a, b, preferred_element_type=F32)


def _norm_mod(x, g, scale, shift):
    ms = jnp.mean(x * x, axis=-1, keepdims=True)
    return (x * lax.rsqrt(ms + NORM_EPS) * g) * (1.0 + scale) + shift


def _mm_kernel(a_ref, b_ref, o_ref, acc_ref, *, nk):
    k = pl.program_id(2)
    part = _dot(a_ref[...].astype(BF16), b_ref[...].astype(BF16))

    @pl.when(k == 0)
    def _():
        acc_ref[...] = part

    @pl.when(k > 0)
    def _():
        acc_ref[...] += part

    @pl.when(k == nk - 1)
    def _():
        o_ref[...] = acc_ref[...].astype(o_ref.dtype)


def matmul(a, b, tm, tn, tk):
    M, K = a.shape
    N = b.shape[1]
    nk = K // tk
    return pl.pallas_call(
        functools.partial(_mm_kernel, nk=nk),
        grid=(M // tm, N // tn, nk),
        in_specs=[pl.BlockSpec((tm, tk), lambda i, j, k: (i, k)),
                  pl.BlockSpec((tk, tn), lambda i, j, k: (k, j))],
        out_specs=pl.BlockSpec((tm, tn), lambda i, j, k: (i, j)),
        out_shape=jax.ShapeDtypeStruct((M, N), F32),
        scratch_shapes=[pltpu.VMEM((tm, tn), F32)],
        compiler_params=_params(("parallel", "parallel", "arbitrary"), 40),
        name="matmul",
    )(a, b)


def _ada_kernel(s_ref, w_ref, b_ref, o_ref):
    o_ref[...] = _dot(s_ref[...].astype(BF16), w_ref[...].astype(BF16)) + b_ref[...]


def ada_modulation(s, ada_w, ada_b):
    depth, D, N = ada_w.shape
    R = s.shape[0]
    tn = 1024
    return pl.pallas_call(
        _ada_kernel,
        grid=(depth, N // tn),
        in_specs=[pl.BlockSpec((R, D), lambda l, j: (0, 0)),
                  pl.BlockSpec((None, D, tn), lambda l, j: (l, 0, j)),
                  pl.BlockSpec((None, 1, tn), lambda l, j: (l, 0, j))],
        out_specs=pl.BlockSpec((None, R, tn), lambda l, j: (l, 0, j)),
        out_shape=jax.ShapeDtypeStruct((depth, R, N), F32),
        compiler_params=_params(("parallel", "parallel"), 32),
        name="ada_modulation",
    )(s, ada_w, ada_b.reshape(depth, 1, N))


def _in_proj_kernel(h_ref, g_ref, sc_ref, sh_ref, w_ref, cos_ref, sin_ref, qg_ref, kg_ref, *o_refs, hy_cols, has_q, width):
    a = _norm_mod(h_ref[0], g_ref[...], sc_ref[0], sh_ref[0]).astype(BF16)
    cos = cos_ref[...]
    sin = sin_ref[...]
    outs = list(o_refs)
    hy_ref = outs.pop(0) if hy_cols else None
    q_ref = outs.pop(0) if has_q else None
    k_ref, v_ref = outs
    q_col = hy_cols
    k_col = q_col + (width if has_q else 0)
    v_col = k_col + width

    def proj(c0, n):
        return _dot(a, w_ref[:, c0:c0 + n])

    pq = proj(q_col, width) if has_q else None
    pk = proj(k_col, width)
    if hy_cols:
        hy_ref[0] = proj(0, hy_cols)
    for hd in range(width // LANES):
        sl = slice(hd * LANES, (hd + 1) * LANES)
        if has_q:
            q_ref[0, :, sl] = _head_norm_rope(pq[:, sl], qg_ref[...], cos, sin, LOG2E * DIFF_HEAD_DIM ** -0.5).astype(BF16)
        k_ref[0, :, sl] = _head_norm_rope(pk[:, sl], kg_ref[...], cos, sin, 1.0).astype(BF16)
    pv = proj(v_col, width)
    ones = jnp.ones((pv.shape[0], LANES), BF16)
    for hd in range(width // LANES):
        v_ref[0, :, 2 * hd * LANES:(2 * hd + 1) * LANES] = pv[:, hd * LANES:(hd + 1) * LANES].astype(BF16)
        v_ref[0, :, (2 * hd + 1) * LANES:(2 * hd + 2) * LANES] = ones


def in_proj(h, g, scale, shift, w, hy_cols, has_q, cos, sin_signed, q_g, k_g, tm):
    B, S, D = h.shape
    N = w.shape[1]
    width = DIFF_HEADS * 2 * DIFF_HEAD_DIM
    g2 = lambda gg: jnp.concatenate([gg, gg]).reshape(1, LANES)
    mod = pl.BlockSpec((1, 1, D), lambda b, i: (b, 0, 0))
    tab = pl.BlockSpec((tm, LANES), lambda b, i: (i, 0))
    vec = pl.BlockSpec((1, LANES), lambda b, i: (0, 0))
    widths = ([hy_cols] if hy_cols else []) + ([width] if has_q else []) + [width, 2 * width]
    dtypes = ([F32] if hy_cols else []) + ([BF16] if has_q else []) + [BF16, BF16]
    return pl.pallas_call(
        functools.partial(_in_proj_kernel, hy_cols=hy_cols, has_q=has_q, width=width),
        grid=(B, S // tm),
        in_specs=[pl.BlockSpec((1, tm, D), lambda b, i: (b, i, 0)),
                  pl.BlockSpec((1, D), lambda b, i: (0, 0)), mod, mod,
                  pl.BlockSpec((D, N), lambda b, i: (0, 0)), tab, tab, vec, vec],
        out_specs=[pl.BlockSpec((1, tm, wd), lambda b, i: (b, i, 0)) for wd in widths],
        out_shape=[jax.ShapeDtypeStruct((B, S, wd), dt) for wd, dt in zip(widths, dtypes)],
        compiler_params=_params(("parallel", "parallel"), 56),
        name="in_proj",
    )(h, g.reshape(1, D), scale, shift, w, cos, sin_signed, g2(q_g), g2(k_g))


def _sconv_kernel(p_ref, w_ref, b_ref, o_ref):
    x = p_ref[0]
    S = x.shape[0]
    row = lax.broadcasted_iota(I32, x.shape, 0)
    xm = jnp.where(row == 0, 0.0, pltpu.roll(x, 1, 0))
    xp = jnp.where(row == S - 1, 0.0, pltpu.roll(x, S - 1, 0))
    w = w_ref[...]
    o_ref[0] = (xm * w[0:1] + x * w[1:2] + xp * w[2:3] + b_ref[...]).astype(o_ref.dtype)


def short_conv(p, conv_w, conv_b, col0, width, tc, out_dtype):
    B, S, _ = p.shape
    c0 = col0 // tc
    return pl.pallas_call(
        _sconv_kernel,
        grid=(B, width // tc),
        in_specs=[pl.BlockSpec((1, S, tc), lambda b, c: (b, 0, c + c0)),
                  pl.BlockSpec((3, tc), lambda b, c: (0, c + c0)),
                  pl.BlockSpec((1, tc), lambda b, c: (0, c + c0))],
        out_specs=pl.BlockSpec((1, S, tc), lambda b, c: (b, 0, c)),
        out_shape=jax.ShapeDtypeStruct((B, S, width), out_dtype),
        compiler_params=_params(("parallel", "parallel"), 48),
        name="short_conv",
    )(p, conv_w, conv_b.reshape(1, -1))


def _dft_fwd_kernel(c_ref, s_ref, u_ref, kre_ref, kim_ref, y_ref, *, nblk):
    c = c_ref[...]
    s = s_ref[...]
    C = u_ref.shape[3]
    groups = [slice(g * LANES * 2, (g + 1) * LANES * 2) for g in range(C // (2 * LANES))] if C % (2 * LANES) == 0 else [slice(0, C)]

    def transforms(ch):
        return [(_dot(c, u_ref[0, j, :, ch]), _dot(s, u_ref[0, j, :, ch])) for j in range(nblk)]

    def products(ch, spec):
        for i in range(nblk):
            yre = yim = None
            for j in range(nblk):
                d = i - j + nblk - 1
                kre = kre_ref[:, d * C + ch.start:d * C + ch.stop]
                kim = kim_ref[:, d * C + ch.start:d * C + ch.stop]
                ure, usn = spec[j]
                tre = kre * ure + kim * usn
                tim = kim * ure - kre * usn
                yre = tre if yre is None else yre + tre
                yim = tim if yim is None else yim + tim
            y_ref[0, i, 0, :, ch] = yre.astype(BF16)
            y_ref[0, i, 1, :, ch] = yim.astype(BF16)

    spec = transforms(groups[0])
    for g, ch in enumerate(groups):
        nxt = transforms(groups[g + 1]) if g + 1 < len(groups) else None
        products(ch, spec)
        spec = nxt


def dft_forward(cf, sf, u, kre, kim, order, tf):
    B, nblk, Lb, C = u.shape
    nd = 2 * nblk - 1
    return pl.pallas_call(
        functools.partial(_dft_fwd_kernel, nblk=nblk),
        grid=(Lb // tf, B),
        in_specs=[pl.BlockSpec((tf, Lb), lambda f, b: (f, 0)),
                  pl.BlockSpec((tf, Lb), lambda f, b: (f, 0)),
                  pl.BlockSpec((1, nblk, Lb, C), lambda f, b: (b, 0, 0, 0)),
                  pl.BlockSpec((tf, nd * C), lambda f, b: (f, order)),
                  pl.BlockSpec((tf, nd * C), lambda f, b: (f, order))],
        out_specs=pl.BlockSpec((1, nblk, 2, tf, C), lambda f, b: (b, 0, 0, f, 0)),
        out_shape=jax.ShapeDtypeStruct((B, nblk, 2, Lb, C), BF16),
        compiler_params=_params(("parallel", "parallel"), 56),
        name="dft_forward",
    )(cf, sf, u, kre, kim)


def _dft_inv_kernel(ct_ref, st_ref, y_ref, gate_ref, o_ref, *, scale):
    acc = _dot(ct_ref[...], y_ref[0, 0, 0]) - _dot(st_ref[...], y_ref[0, 0, 1])
    o_ref[0, 0] = (gate_ref[0, 0] * (acc * scale)).astype(BF16)


def dft_inverse(ct, st, y, gates, gate_col):
    B, nblk, _, Lb, C = y.shape
    return pl.pallas_call(
        functools.partial(_dft_inv_kernel, scale=1.0 / Lb),
        grid=(B, nblk),
        in_specs=[pl.BlockSpec((Lb, Lb), lambda b, i: (0, 0)),
                  pl.BlockSpec((Lb, Lb), lambda b, i: (0, 0)),
                  pl.BlockSpec((1, 1, 2, Lb, C), lambda b, i: (b, i, 0, 0, 0)),
                  pl.BlockSpec((1, 1, Lb, C), lambda b, i: (b, i, 0, gate_col))],
        out_specs=pl.BlockSpec((1, 1, Lb, C), lambda b, i: (b, i, 0, 0)),
        out_shape=jax.ShapeDtypeStruct((B, nblk, Lb, C), BF16),
        compiler_params=_params(("parallel", "parallel"), 48),
        name="dft_inverse",
    )(ct, st, y, gates)


def dft_tables(L):
    n = 2 * L
    f = lax.broadcasted_iota(I32, (L, L), 0)
    t = lax.broadcasted_iota(I32, (L, L), 1)
    m = ((2 * f + 1) * t) % (2 * n)
    ang = m.astype(F32) * (math.pi / n)
    cf = jnp.cos(ang)
    sf = jnp.sin(ang)
    return cf.astype(BF16), sf.astype(BF16), cf.T.astype(BF16), sf.T.astype(BF16)


def hyena_filter_taps(L, w1, b1, w2, b2, w3, b3, freq, wout, width):
    hp = lax.Precision.HIGHEST
    t = jnp.linspace(0.0, 1.0, L, dtype=F32)[:, None]
    bands = (FILTER_EMB - 1) // 2
    w = 2.0 * math.pi * jnp.arange(L, dtype=F32)[:, None] / L
    f = jnp.linspace(1e-4, bands - 1, bands, dtype=F32)[None, :]
    z = jnp.concatenate([t, jnp.cos(f * w), -jnp.sin(f * w)], axis=-1)
    h = jnp.sin(freq * (jnp.dot(z, w1, precision=hp) + b1))
    h = jnp.sin(freq * (jnp.dot(h, w2, precision=hp) + b2))
    h = jnp.sin(freq * (jnp.dot(h, w3, precision=hp) + b3))
    max_decay = math.log(FILTER_TARGET) / FILTER_FAST_DECAY
    min_decay = math.log(FILTER_TARGET) / FILTER_SLOW_DECAY
    deltas = jnp.abs(jnp.linspace(min_decay, max_decay, width, dtype=F32))

    def taps(hh, tt):
        return jnp.dot(hh, wout, precision=hp).reshape(L, HYENA_ORDER, 2, width) * jnp.exp(-tt * deltas[None, :])[:, None, None, :]

    return taps(h, t), taps(h[::-1], t[::-1])


def filter_spectra(cf, sf, taps, taps_rev, bias, nblk):
    L, C = taps.shape[0], taps.shape[3]
    Lb = L // nblk
    zero = jnp.zeros((1, C), F32)
    sums, diffs = [], []
    for o in range(HYENA_ORDER):
        h_fwd, h_bwd = taps[:, o, 0], taps[:, o, 1]
        h_fwd_rev, h_bwd_rev = taps_rev[:, o, 0], taps_rev[:, o, 1]
        k0 = (h_fwd[0] + h_bwd[0] + bias[o])[None, :]
        kfull = jnp.concatenate([zero, h_bwd_rev[:L - 1], k0, h_fwd[1:]], axis=0)
        krev = jnp.concatenate([h_fwd_rev[:L - 1], k0, h_bwd[1:], zero], axis=0)
        for dlt in range(-(nblk - 1), nblk):
            base = L + dlt * Lb
            kp = kfull[base:base + Lb]
            km = jnp.concatenate([zero, krev[2 * L - base:2 * L - base + Lb - 1]], axis=0)
            sums.append(kp + km)
            diffs.append(km - kp)
    tmm = min(Lb, 512)
    kre = matmul(cf, jnp.concatenate(sums, axis=1), tmm, tmm, tmm)
    kim = matmul(sf, jnp.concatenate(diffs, axis=1), tmm, tmm, tmm)
    return kre, kim


def hyena_operator(gates, v, tables, taps, taps_rev, bias, nblk, tf):
    cf, sf, ct, st = tables
    B, L, C = v.shape
    Lb = L // nblk
    kre, kim = filter_spectra(cf, sf, taps, taps_rev, bias, nblk)
    gates = gates.reshape(B, nblk, Lb, 2 * C)
    z = v.reshape(B, nblk, Lb, C)
    for o in range(HYENA_ORDER):
        y = dft_forward(cf, sf, z, kre, kim, o, tf)
        z = dft_inverse(ct, st, y, gates, o)
    return z.reshape(B, L, C)


def _head_norm_rope(x, g, cos, sin_signed, scale):
    lane = lax.broadcasted_iota(I32, x.shape, 1)
    lo = lane < DIFF_HEAD_DIM
    x2 = x * x
    s_lo = jnp.sum(jnp.where(lo, x2, 0.0), axis=-1, keepdims=True)
    s_hi = jnp.sum(jnp.where(lo, 0.0, x2), axis=-1, keepdims=True)
    ms = jnp.where(lo, s_lo, s_hi) * (1.0 / DIFF_HEAD_DIM)
    xn = x * lax.rsqrt(ms + NORM_EPS) * g
    first = (lane & 16) == 0
    partner = jnp.where(first, pltpu.roll(xn, LANES - 16, 1), pltpu.roll(xn, 16, 1))
    return (xn * cos + partner * sin_signed) * scale


def rope_tables(S, use_rope):
    if not use_rope:
        return jnp.ones((S, LANES), F32), jnp.zeros((S, LANES), F32)
    t = jnp.arange(S, dtype=I32)
    row = (t // GRID_W).astype(F32)[:, None]
    colp = (t % GRID_W).astype(F32)[:, None]
    nf = DIFF_HEAD_DIM // 4
    inv = ROPE_BASE ** (-jnp.arange(nf, dtype=F32) / nf)
    lane = jnp.arange(LANES)
    grp = (lane % DIFF_HEAD_DIM) // nf
    j = lane % nf
    pos = jnp.where((grp < 2)[None, :], row, colp)
    ang = pos * inv[j][None, :]
    sign = jnp.where((grp % 2 == 0)[None, :], -1.0, 1.0)
    return jnp.cos(ang), jnp.sin(ang) * sign


def _attn_kernel(lam_ref, q_ref, kc_ref, vc_ref, *rest, has_lat, out_scale, ck):
    if has_lat:
        kl_ref, vl_ref, g_ref, o_ref = rest
    else:
        g_ref, o_ref = rest
    lam = lam_ref[0]
    q = q_ref[0]
    lane = lax.broadcasted_iota(I32, q.shape, 1)
    nt = (((1,), (1,)), ((), ()))
    zero = jnp.zeros_like(q)
    tq = q.shape[0]
    qs = jnp.concatenate([jnp.where(lane < DIFF_HEAD_DIM, q, zero), jnp.where(lane >= DIFF_HEAD_DIM, q, zero)], axis=0)
    chunks = [(kc_ref, vc_ref, 0, kc_ref.shape[1])]
    if has_lat:
        chunks += [(kl_ref, vl_ref, c * ck, ck) for c in range(kl_ref.shape[1] // ck)]

    def scores(ch):
        return lax.dot_general(qs, ch[0][0, ch[2]:ch[2] + ch[3], :], nt, preferred_element_type=F32)

    m = acc = None
    s_next = scores(chunks[0])
    for ci, ch in enumerate(chunks):
        s = s_next
        if ci + 1 < len(chunks):
            s_next = scores(chunks[ci + 1])
        mx = jnp.max(s, axis=-1, keepdims=True)
        m_new = mx if ci == 0 else jnp.maximum(m, mx)
        pv = _dot(jnp.exp2(s - m_new).astype(BF16), ch[1][0, ch[2]:ch[2] + ch[3], :])
        acc = pv if ci == 0 else jnp.exp2(m - m_new) * acc + pv
        m = m_new
    both = acc[:, :LANES] / acc[:, LANES:]
    o = both[:tq] - lam * both[tq:]
    ms = jnp.mean(o * o, axis=-1, keepdims=True)
    o_ref[0] = ((o * lax.rsqrt(ms + NORM_EPS) * g_ref[...]) * out_scale).astype(BF16)


def diff_attention(lam, q, k_ctx, v_ctx, k_lat, v_lat, subln_g, out_scale, tq):
    B, Sq, W = q.shape
    H = W // LANES
    has_lat = k_lat is not None
    Sc = k_ctx.shape[1]
    head = lambda S, w: pl.BlockSpec((1, S, w), lambda b, h, i: (b, 0, h))
    in_specs = [pl.BlockSpec(memory_space=pltpu.SMEM),
                pl.BlockSpec((1, tq, LANES), lambda b, h, i: (b, i, h)), head(Sc, LANES), head(Sc, 2 * LANES)]
    args = [lam.reshape(1), q, k_ctx, v_ctx]
    if has_lat:
        in_specs += [head(k_lat.shape[1], LANES), head(k_lat.shape[1], 2 * LANES)]
        args += [k_lat, v_lat]
    in_specs.append(pl.BlockSpec((1, LANES), lambda b, h, i: (0, 0)))
    args.append(subln_g.reshape(1, LANES))
    return pl.pallas_call(
        functools.partial(_attn_kernel, has_lat=has_lat, out_scale=out_scale,
                          ck=min(ATTN_KEY_CHUNK, k_lat.shape[1]) if has_lat else 0),
        grid=(B, H, Sq // tq),
        in_specs=in_specs,
        out_specs=pl.BlockSpec((1, tq, LANES), lambda b, h, i: (b, i, h)),
        out_shape=jax.ShapeDtypeStruct((B, Sq, W), BF16),
        compiler_params=_params(("parallel", "parallel", "parallel"), 56),
        name="diff_attention",
    )(*args)


def _oproj_kernel(hy_ref, o_ref, w_ref, h_ref, gate_ref, *rest, half, n_exp):
    prep_in, (out_ref, *prep_out) = rest[:4], rest[4:]
    y = _dot(hy_ref[0].astype(BF16), w_ref[0:half, :]) + _dot(o_ref[0], w_ref[half:, :])
    h_new = h_ref[0] + gate_ref[0] * y
    out_ref[0] = h_new
    _ffn_prep_store(h_new, *prep_in, *prep_out, n_exp)


def out_proj(hy, o, w_out, h, gate, prep, tm):
    B, S, D = h.shape
    half = hy.shape[2]
    n_exp = prep[3].shape[1]
    p_in, p_out, p_shape = _ffn_prep_specs(B, S, D, tm, n_exp)
    return pl.pallas_call(
        functools.partial(_oproj_kernel, half=half, n_exp=n_exp),
        grid=(B, S // tm),
        in_specs=[pl.BlockSpec((1, tm, half), lambda b, i: (b, i, 0)),
                  pl.BlockSpec((1, tm, half), lambda b, i: (b, i, 0)),
                  pl.BlockSpec((2 * half, D), lambda b, i: (0, 0)),
                  pl.BlockSpec((1, tm, D), lambda b, i: (b, i, 0)),
                  pl.BlockSpec((1, 1, D), lambda b, i: (b, 0, 0))] + p_in,
        out_specs=[pl.BlockSpec((1, tm, D), lambda b, i: (b, i, 0))] + p_out,
        out_shape=[jax.ShapeDtypeStruct((B, S, D), F32)] + p_shape,
        compiler_params=_params(("parallel", "parallel"), 40),
        name="out_proj",
    )(hy, o, w_out, h, gate, *_ffn_prep_args(prep))


def _shift_rows(x, d):
    return pltpu.roll(x, (-d) % x.shape[0], 0)


def _pool_kernel(hp_ref, hc_ref, hn_ref, g_ref, sc_ref, sh_ref, gate_ref, pw_ref, ps_ref, *rest, T, L, n_exp):
    prep_in, (o_ref, *prep_out) = rest[:4], rest[4:]
    i = pl.program_id(1)
    nt = pl.num_programs(1)
    g, sc, sh = g_ref[...], sc_ref[0], sh_ref[0]
    hc = hc_ref[0]
    a_c = _norm_mod(hc, g, sc, sh)
    a_p = jnp.where(i == 0, 0.0, _norm_mod(hp_ref[0], g, sc, sh))
    a_n = jnp.where(i == nt - 1, 0.0, _norm_mod(hn_ref[0], g, sc, sh))
    ext = jnp.concatenate([a_p, a_c, a_n], axis=0)
    tok = i * T + lax.broadcasted_iota(I32, (T, 1), 0)
    G = ext.shape[1] // len(POOL_WINDOWS)
    ys = []
    for gi, w in enumerate(POOL_WINDOWS):
        xg = ext[:, gi * G:(gi + 1) * G]
        s = _shift_rows(xg, -1) + xg
        step = 1
        while 2 * step < w:
            s = _shift_rows(s, -step) + _shift_rows(s, step)
            step *= 2
        cnt = (jnp.minimum(tok + w // 2, L) - jnp.maximum(tok - w // 2, 0)).astype(F32)
        p = s[POOL_HALO:POOL_HALO + T] / cnt - a_c[:, gi * G:(gi + 1) * G]
        ys.append(_dot(p.astype(BF16), pw_ref[gi]))
    y = jnp.concatenate(ys, axis=1) * ps_ref[...]
    h_new = hc + gate_ref[0] * y
    o_ref[0] = h_new
    _ffn_prep_store(h_new, *prep_in, *prep_out, n_exp)


def pool_mixer(h, g, scale, shift, gate, pool_w, pool_scale, prep, T):
    B, S, D = h.shape
    nh = T // POOL_HALO
    last = S // POOL_HALO - 1
    n_exp = prep[3].shape[1]
    p_in, p_out, p_shape = _ffn_prep_specs(B, S, D, T, n_exp)
    mod = pl.BlockSpec((1, 1, D), lambda b, i: (b, 0, 0))
    return pl.pallas_call(
        functools.partial(_pool_kernel, T=T, L=S, n_exp=n_exp),
        grid=(B, S // T),
        in_specs=[pl.BlockSpec((1, POOL_HALO, D), lambda b, i: (b, jnp.maximum(i * nh - 1, 0), 0)),
                  pl.BlockSpec((1, T, D), lambda b, i: (b, i, 0)),
                  pl.BlockSpec((1, POOL_HALO, D), lambda b, i: (b, jnp.minimum((i + 1) * nh, last), 0)),
                  pl.BlockSpec((1, D), lambda b, i: (0, 0)), mod, mod, mod,
                  pl.BlockSpec(pool_w.shape, lambda b, i: (0, 0, 0)),
                  pl.BlockSpec((1, D), lambda b, i: (0, 0))] + p_in,
        out_specs=[pl.BlockSpec((1, T, D), lambda b, i: (b, i, 0))] + p_out,
        out_shape=[jax.ShapeDtypeStruct((B, S, D), F32)] + p_shape,
        compiler_params=_params(("parallel", "parallel"), 48),
        name="pool_mixer",
    )(h, h, h, g.reshape(1, D), scale, shift, gate, pool_w, pool_scale.reshape(1, D), *_ffn_prep_args(prep))


def _ffn_prep_store(h_new, g_ref, sc_ref, sh_ref, rw_ref, x_ref, aff_ref, afft_ref, n_exp):
    a = _norm_mod(h_new, g_ref[...], sc_ref[0], sh_ref[0]).astype(BF16)
    x_ref[0] = a
    logits = _dot(a, rw_ref[...])
    lane = lax.broadcasted_iota(I32, logits.shape, 1)
    valid = lane < n_exp
    mx = jnp.max(jnp.where(valid, logits, -jnp.inf), axis=-1, keepdims=True)
    e = jnp.where(valid, jnp.exp(logits - mx), 0.0)
    aff = e / jnp.sum(e, axis=-1, keepdims=True)
    aff_ref[0] = aff
    afft_ref[0] = aff.T[0:n_exp]


def _ffn_prep_specs(B, S, D, tm, n_exp):
    mod = pl.BlockSpec((1, 1, D), lambda b, i: (b, 0, 0))
    ins = [pl.BlockSpec((1, D), lambda b, i: (0, 0)), mod, mod, pl.BlockSpec((D, LANES), lambda b, i: (0, 0))]
    outs = [pl.BlockSpec((1, tm, D), lambda b, i: (b, i, 0)),
            pl.BlockSpec((1, tm, LANES), lambda b, i: (b, i, 0)),
            pl.BlockSpec((1, n_exp, tm), lambda b, i: (b, 0, i))]
    shapes = [jax.ShapeDtypeStruct((B, S, D), BF16), jax.ShapeDtypeStruct((B, S, LANES), F32),
              jax.ShapeDtypeStruct((B, n_exp, S), F32)]
    return ins, outs, shapes


def _ffn_prep_args(prep):
    g, scale, shift, router_w = prep
    rw = jnp.pad(router_w, ((0, 0), (0, LANES - router_w.shape[1]))).astype(BF16)
    return [g.reshape(1, -1), scale, shift, rw]


def _prefix_excl(m, tri, tb):
    S = m.shape[0]
    carry = jnp.zeros((1, m.shape[1]), F32)
    outs, carries = [], []
    for blk in range(S // tb):
        mb = m[blk * tb:(blk + 1) * tb]
        outs.append(_dot(tri, mb.astype(BF16)) + carry)
        carries.append(carry)
        carry = carry + jnp.sum(mb, axis=0, keepdims=True)
    carries.append(carry)
    return (jnp.concatenate(outs, axis=0) if len(outs) > 1 else outs[0]), jnp.concatenate(carries, axis=0)


def _select_kernel(aff_ref, afft_ref, tri_ref, pos_ref, post_ref, offs_ref, *, cap, tb, n_exp):
    bits = pltpu.bitcast(afft_ref[0], I32)

    def body(i, cur):
        cand = cur | (jnp.int32(1) << (30 - i))
        cnt = jnp.sum((bits >= cand).astype(F32), axis=1, keepdims=True)
        return jnp.where(cnt >= cap, cand, cur)

    thr_bits = lax.fori_loop(0, 31, body, jnp.zeros((n_exp, 1), I32))
    thr_col = pltpu.bitcast(jnp.broadcast_to(thr_bits, (n_exp, LANES)), F32)
    sub = lax.broadcasted_iota(I32, (n_exp, LANES), 0)
    lane = lax.broadcasted_iota(I32, (n_exp, LANES), 1)
    thr = jnp.sum(jnp.where(sub == lane, thr_col, 0.0), axis=0, keepdims=True)
    aff = aff_ref[0]
    tri = tri_ref[...]
    gt = (aff > thr).astype(F32)
    eq = (aff == thr).astype(F32)
    need = cap - jnp.sum(gt, axis=0, keepdims=True)
    sel = gt + eq * (_prefix_excl(eq, tri, tb)[0] < need).astype(F32)
    slot, offs = _prefix_excl(sel, tri, tb)
    sp = jnp.where(sel > 0.0, slot, -1.0)
    pos_ref[0] = sp
    offs_ref[0] = offs
    for blk in range(sp.shape[0] // tb):
        post_ref[0, :, blk * tb:(blk + 1) * tb] = sp[blk * tb:(blk + 1) * tb].T[0:n_exp]


def select_tokens(aff, afft, cap):
    B, S, _ = aff.shape
    n_exp = afft.shape[1]
    tb = min(S, 256)
    r = lax.broadcasted_iota(I32, (tb, tb), 0)
    c = lax.broadcasted_iota(I32, (tb, tb), 1)
    tri = (c < r).astype(BF16)
    nblk = S // tb
    pos, post, offs = pl.pallas_call(
        functools.partial(_select_kernel, cap=cap, tb=tb, n_exp=n_exp),
        grid=(B,),
        in_specs=[pl.BlockSpec((1, S, LANES), lambda b: (b, 0, 0)),
                  pl.BlockSpec((1, n_exp, S), lambda b: (b, 0, 0)),
                  pl.BlockSpec((tb, tb), lambda b: (0, 0))],
        out_specs=[pl.BlockSpec((1, S, LANES), lambda b: (b, 0, 0)),
                   pl.BlockSpec((1, n_exp, S), lambda b: (b, 0, 0)),
                   pl.BlockSpec((1, nblk + 1, LANES), lambda b: (b, 0, 0))],
        out_shape=[jax.ShapeDtypeStruct((B, S, LANES), F32), jax.ShapeDtypeStruct((B, n_exp, S), F32),
                   jax.ShapeDtypeStruct((B, nblk + 1, LANES), F32)],
        compiler_params=_params(("parallel",), 48),
        name="select_tokens",
    )(aff, afft, tri)
    offs = jnp.swapaxes(offs[:, :, :n_exp], 1, 2).astype(I32).reshape(-1)
    return pos, post, offs, tb


def _gather_kernel(offs_ref, x_ref, post_ref, afft_ref, o_ref, g_ref, *, nblk, n_exp, win):
    b = pl.program_id(0)
    t = pl.program_id(1)
    slots = o_ref.shape[2]

    @pl.when(t == 0)
    def _():
        o_ref[...] = jnp.zeros_like(o_ref)
        g_ref[...] = jnp.zeros_like(g_ref)

    x = x_ref[0]
    T = x.shape[0]

    def full():
        r = lax.broadcasted_iota(I32, (slots, T), 0).astype(F32)
        for e in range(n_exp):
            hit = r == post_ref[0, e:e + 1, :]
            o_ref[e, 0] = (o_ref[e, 0].astype(F32) + _dot(hit.astype(BF16), x)).astype(BF16)
            g = jnp.sum(jnp.where(hit, afft_ref[0, e:e + 1, :], 0.0), axis=1, keepdims=True)
            g_ref[e, 0] = g_ref[e, 0] + jnp.broadcast_to(g, (slots, LANES))

    if win >= slots:
        full()
        return
    starts = []
    narrow = None
    for e in range(n_exp):
        base = (b * n_exp + e) * (nblk + 1) + t
        lo = offs_ref[base]
        hi = offs_ref[base + 1]
        a0 = pl.multiple_of(jnp.minimum((lo // BF16_ROWS) * BF16_ROWS, slots - win), BF16_ROWS)
        ok = hi - a0 <= win
        narrow = ok if narrow is None else jnp.logical_and(narrow, ok)
        starts.append(a0)

    @pl.when(narrow)
    def _():
        r = lax.broadcasted_iota(I32, (win, T), 0).astype(F32)
        hits = [r + starts[e].astype(F32) == post_ref[0, e:e + 1, :] for e in range(n_exp)]
        per = n_exp // GATHER_GROUPS
        rows = [_dot(jnp.concatenate([hits[e].astype(BF16) for e in range(gi * per, (gi + 1) * per)], axis=0), x)
                for gi in range(GATHER_GROUPS)]
        for gi in range(GATHER_GROUPS):
            members = range(gi * per, (gi + 1) * per)
            old = [o_ref[e, 0, pl.ds(starts[e], win), :] for e in members]
            old_g = [g_ref[e, 0, pl.ds(starts[e], win), :] for e in members]
            for k, e in enumerate(members):
                o_ref[e, 0, pl.ds(starts[e], win), :] = (old[k].astype(F32) + rows[gi][k * win:(k + 1) * win]).astype(BF16)
                g = jnp.sum(jnp.where(hits[e], afft_ref[0, e:e + 1, :], 0.0), axis=1, keepdims=True)
                g_ref[e, 0, pl.ds(starts[e], win), :] = old_g[k] + jnp.broadcast_to(g, (win, LANES))

    pl.when(jnp.logical_not(narrow))(full)


def gather_tokens(x, post, afft, offs, slots, T):
    B, S, D = x.shape
    E = post.shape[1]
    row = pl.BlockSpec((1, E, T), lambda b, t, offs: (b, 0, t))
    return pl.pallas_call(
        functools.partial(_gather_kernel, nblk=S // T, n_exp=E, win=min(slots, GATHER_WINDOW)),
        grid_spec=pltpu.PrefetchScalarGridSpec(
            num_scalar_prefetch=1,
            grid=(B, S // T),
            in_specs=[pl.BlockSpec((1, T, D), lambda b, t, offs: (b, t, 0)), row, row],
            out_specs=[pl.BlockSpec((E, 1, slots, D), lambda b, t, offs: (0, b, 0, 0)),
                       pl.BlockSpec((E, 1, slots, LANES), lambda b, t, offs: (0, b, 0, 0))]),
        out_shape=[jax.ShapeDtypeStruct((E, B, slots, D), BF16), jax.ShapeDtypeStruct((E, B, slots, LANES), F32)],
        compiler_params=_params(("parallel", "arbitrary"), 56),
        name="gather_tokens",
    )(offs, x, post, afft)


def _ffn_kernel(xs_ref, g_ref, wg_ref, wu_ref, wd_ref, o_ref, hm_ref, *, nf):
    f = pl.program_id(2)
    x = xs_ref[0]
    a = _dot(x, wg_ref[...].astype(BF16))
    u = _dot(x, wu_ref[...].astype(BF16))
    hm_ref[f] = (a * jax.nn.sigmoid(a) * u).astype(BF16)

    @pl.when(f == nf - 1)
    def _():
        hm = jnp.concatenate([hm_ref[c] for c in range(nf)], axis=1) if nf > 1 else hm_ref[0]
        o_ref[0] = (_dot(hm, wd_ref[...].astype(BF16)) * g_ref[0, :, 0:1]).astype(BF16)


def expert_ffn(xs, gs, w_gate, w_up, w_down, layer, tm, tf):
    E, M, D = xs.shape
    F = w_gate.shape[3]
    nf = F // tf
    return pl.pallas_call(
        functools.partial(_ffn_kernel, nf=nf),
        grid=(E, M // tm, nf),
        in_specs=[pl.BlockSpec((1, tm, D), lambda e, m, f: (e, m, 0)),
                  pl.BlockSpec((1, tm, LANES), lambda e, m, f: (e, m, 0)),
                  pl.BlockSpec((None, None, D, tf), lambda e, m, f: (layer, e, 0, f)),
                  pl.BlockSpec((None, None, D, tf), lambda e, m, f: (layer, e, 0, f)),
                  pl.BlockSpec((None, None, F, D), lambda e, m, f: (layer, e, 0, 0))],
        out_specs=pl.BlockSpec((1, tm, D), lambda e, m, f: (e, m, 0)),
        out_shape=jax.ShapeDtypeStruct((E, M, D), BF16),
        scratch_shapes=[pltpu.VMEM((nf, tm, tf), BF16)],
        compiler_params=_params(("parallel", "parallel", "arbitrary"), 56),
        name="expert_ffn",
    )(xs, gs, w_gate, w_up, w_down)


def _combine_kernel(offs_ref, y_ref, pos_ref, h_ref, gate_ref, o_ref, *ybuf_refs, nblk, n_exp, win):
    b = pl.program_id(0)
    t = pl.program_id(1)
    T = pos_ref.shape[1]
    slots = y_ref.shape[2]

    def full():
        if slots < LANES:
            K = n_exp * slots
            col = lax.broadcasted_iota(I32, (LANES, K), 1).astype(F32)
            owner = jnp.floor((col + 0.5) * (1.0 / slots))
            rep = owner == lax.broadcasted_iota(I32, (LANES, K), 0).astype(F32)
            slot_rep = _dot(pos_ref[0].astype(BF16), rep.astype(BF16))
            want = (col - owner * slots)[0:1]
            acc = _dot((slot_rep == want).astype(BF16), y_ref[:, 0].reshape(K, y_ref.shape[3]))
        else:
            r = lax.broadcasted_iota(I32, (T, slots), 1).astype(F32)
            acc = None
            for e in range(n_exp):
                part = _dot((pos_ref[0, :, e:e + 1] == r).astype(BF16), y_ref[e, 0])
                acc = part if acc is None else acc + part
        o_ref[0] = h_ref[0] + gate_ref[0] * acc

    if win >= slots:
        full()
        return
    starts = []
    narrow = None
    for e in range(n_exp):
        base = (b * n_exp + e) * (nblk + 1) + t
        lo = offs_ref[base]
        hi = offs_ref[base + 1]
        a0 = jnp.minimum((lo // BF16_ROWS) * BF16_ROWS, slots - win)
        ok = hi - a0 <= win
        narrow = ok if narrow is None else jnp.logical_and(narrow, ok)
        starts.append(a0)

    @pl.when(narrow)
    def _():
        r = lax.broadcasted_iota(I32, (T, win), 1).astype(F32)
        acc = None
        per = n_exp // len(ybuf_refs)
        for gi, ybuf_ref in enumerate(ybuf_refs):
            pieces = []
            for k, e in enumerate(range(gi * per, (gi + 1) * per)):
                a0 = starts[e]
                ybuf_ref[k * win:(k + 1) * win, :] = y_ref[e, 0, pl.ds(pl.multiple_of(a0, BF16_ROWS), win), :]
                pieces.append((pos_ref[0, :, e:e + 1] - a0.astype(F32) == r).astype(BF16))
            part = _dot(jnp.concatenate(pieces, axis=1), ybuf_ref[...])
            acc = part if acc is None else acc + part
        o_ref[0] = h_ref[0] + gate_ref[0] * acc

    pl.when(jnp.logical_not(narrow))(full)


def combine_tokens(y, pos, offs, h, gate, T):
    E, B, slots, D = y.shape
    S = h.shape[1]
    win = min(slots, COMBINE_WINDOW)
    return pl.pallas_call(
        functools.partial(_combine_kernel, nblk=S // T, n_exp=E, win=win),
        grid_spec=pltpu.PrefetchScalarGridSpec(
            num_scalar_prefetch=1,
            grid=(B, S // T),
            in_specs=[pl.BlockSpec((E, 1, slots, D), lambda b, t, offs: (0, b, 0, 0)),
                      pl.BlockSpec((1, T, LANES), lambda b, t, offs: (b, t, 0)),
                      pl.BlockSpec((1, T, D), lambda b, t, offs: (b, t, 0)),
                      pl.BlockSpec((1, 1, D), lambda b, t, offs: (b, 0, 0))],
            out_specs=pl.BlockSpec((1, T, D), lambda b, t, offs: (b, t, 0)),
            scratch_shapes=[pltpu.VMEM((E // COMBINE_GROUPS * win, D), BF16)] * COMBINE_GROUPS),
        out_shape=jax.ShapeDtypeStruct((B, S, D), F32),
        compiler_params=_params(("parallel", "parallel"), 56),
        name="combine_tokens",
    )(offs, y, pos, h, gate)


def moe_block(mixed, gate, w_gate, w_up, w_down, layer):
    h, x, aff, afft = mixed
    B, S, D = h.shape
    E = afft.shape[1]
    cap = EC_CAPACITY * S // E
    slots = cap
    pos, post, offs, tb = select_tokens(aff, afft, cap)
    xs, gs = gather_tokens(x, post, afft, offs, slots, tb)
    M = B * slots
    y = expert_ffn(xs.reshape(E, M, D), gs.reshape(E, M, LANES), w_gate, w_up, w_down, layer, min(M, 1024), 512)
    return combine_tokens(y.reshape(E, B, slots, D), pos, offs, h, gate, tb)


def kernel(x, c, ctx, c_ctx, ada_w, ada_b, norm_mix_g, norm_ffn_g, w_in, w_out, hy_conv_w, hy_conv_b, hy_f_w1, hy_f_b1, hy_f_w2, hy_f_b2, hy_f_w3, hy_f_b3, hy_f_freq, hy_f_wout, hy_bias, q_norm_g, k_norm_g, diff_lambda, subln_g, pool_w, pool_scale, router_w, exp_w_gate, exp_w_up, exp_w_down):
    B, S, D = x.shape
    Lc = ctx.shape[1]
    depth = ada_w.shape[0]
    hy_width = hy_bias.shape[2]
    hy_proj = (HYENA_ORDER + 1) * hy_width
    qk_width = DIFF_HEADS * 2 * DIFF_HEAD_DIM
    last_attn = ((depth - 1) // 2) * 2

    s_all = jnp.concatenate([jax.nn.silu(c), jax.nn.silu(c_ctx)[None, :]], axis=0)
    rows = -(-s_all.shape[0] // SUBLANES) * SUBLANES
    mods = ada_modulation(jnp.pad(s_all, ((0, rows - s_all.shape[0]), (0, 0))), ada_w, ada_b)

    nblk_lat = max(1, S // HYENA_BLOCK)
    tables_lat = dft_tables(S // nblk_lat)
    tables_ctx = dft_tables(Lc)
    rope_lat = rope_tables(S, True)
    rope_ctx = rope_tables(Lc, False)

    h, hc = x, ctx
    for l in range(depth):
        m = [mods[l, :B, i * D:(i + 1) * D].reshape(B, 1, D) for i in range(6)]
        mc = [jnp.broadcast_to(mods[l, B, i * D:(i + 1) * D].reshape(1, 1, D), (B, 1, D)) for i in range(6)]
        ctx_full = l < last_attn
        prep = (norm_ffn_g[l], m[4], m[3], router_w[l])
        prep_c = (norm_ffn_g[l], mc[4], mc[3], router_w[l])
        if l % 2 == 0:
            e = l // 2
            lam_init = 0.8 - 0.6 * math.exp(-0.3 * l)
            lv = diff_lambda[e]
            lam = jnp.exp(jnp.sum(lv[0] * lv[1])) - jnp.exp(jnp.sum(lv[2] * lv[3])) + lam_init
            filt = (hy_f_w1[e], hy_f_b1[e], hy_f_w2[e], hy_f_b2[e], hy_f_w3[e], hy_f_b3[e], hy_f_freq[e], hy_f_wout[e])
            w_in_b = w_in[e].astype(BF16)
            w_out_b = w_out[e].astype(BF16)

            p, q, k, v = in_proj(h, norm_mix_g[l], m[1], m[0], w_in_b, hy_proj, True, *rope_lat,
                                 q_norm_g[e], k_norm_g[e], 512)
            gts = short_conv(p, hy_conv_w[e], hy_conv_b[e], 0, 2 * hy_width, 256, F32)
            hv = short_conv(p, hy_conv_w[e], hy_conv_b[e], 2 * hy_width, hy_width, 256, BF16)
            hy = hyena_operator(gts, hv, tables_lat, *hyena_filter_taps(S, *filt, hy_width), hy_bias[e], nblk_lat, 256)

            if l <= last_attn:
                if ctx_full:
                    pc, qc, kc, vc = in_proj(hc, norm_mix_g[l], mc[1], mc[0], w_in_b, hy_proj, True, *rope_ctx,
                                             q_norm_g[e], k_norm_g[e], Lc)
                else:
                    kc, vc = in_proj(hc, norm_mix_g[l], mc[1], mc[0], w_in_b[:, hy_proj + qk_width:], 0, False,
                                     *rope_ctx, q_norm_g[e], k_norm_g[e], Lc)
            o = diff_attention(lam, q, kc, vc, k, v, subln_g[e], 1.0 - lam_init, min(S, 512))
            mixed = out_proj(hy, o, w_out_b, h, m[2], prep, 512)
            if ctx_full:
                gtc = short_conv(pc, hy_conv_w[e], hy_conv_b[e], 0, 2 * hy_width, 256, F32)
                hvc = short_conv(pc, hy_conv_w[e], hy_conv_b[e], 2 * hy_width, hy_width, 256, BF16)
                hyc = hyena_operator(gtc, hvc, tables_ctx, *hyena_filter_taps(Lc, *filt, hy_width), hy_bias[e], 1, Lc)
                oc = diff_attention(lam, qc, kc, vc, None, None, subln_g[e], 1.0 - lam_init, Lc)
                mixed_c = out_proj(hyc, oc, w_out_b, hc, mc[2], prep_c, Lc)
        else:
            o_idx = l // 2
            pw = pool_w[o_idx].astype(BF16)
            mixed = pool_mixer(h, norm_mix_g[l], m[1], m[0], m[2], pw, pool_scale[o_idx], prep, 512)
            if ctx_full:
                mixed_c = pool_mixer(hc, norm_mix_g[l], mc[1], mc[0], mc[2], pw, pool_scale[o_idx], prep_c, Lc)
        h = moe_block(mixed, m[5], exp_w_gate, exp_w_up, exp_w_down, l)
        if ctx_full:
            hc = moe_block(mixed_c, mc[5], exp_w_gate, exp_w_up, exp_w_down, l)
    return h
```
